```python
import functools
import jax, jax.numpy as jnp
from jax import lax
import numpy as np

D_MODEL = 1024
BATCH = 2
SEQ = 8192
DEPTH = 1
DEC_BATCH = 128
DEC_SEQ = 1
PAST_LEN = 8192
PAGE_SIZE = 128

A_HEADS = 8
A_KV_HEADS = 4
A_GROUP = A_HEADS // A_KV_HEADS
A_HEAD_DIM = 64
A_QW = A_HEADS * A_HEAD_DIM
A_KVW = A_KV_HEADS * A_HEAD_DIM
A_BLOCK = 256
A_TOPK = 3
A_QCHUNK = 128
G_HEADS = 4
G_KEY_DIM = D_MODEL // 2
G_VAL_DIM = D_MODEL
G_DK = G_KEY_DIM // G_HEADS
G_DV = G_VAL_DIM // G_HEADS
G_LOWRANK = 16
G_NORMALIZER = 16.0
G_CHUNK = 64
N_EXPERTS = 32
TOP_K = 4
D_FF = D_MODEL
SWIGLU_LIMIT = 7.0
SWIGLU_ALPHA = 1.702
MOE_BLOCK = 128
EPS = 1e-6
SPLITS = (A_QW, A_KVW, A_KVW, G_KEY_DIM, G_KEY_DIM, G_VAL_DIM, G_VAL_DIM, G_LOWRANK, D_MODEL, D_MODEL)
IN_WIDTH = A_QW + 2 * A_KVW + 2 * G_KEY_DIM + 2 * G_VAL_DIM + G_LOWRANK + 2 * D_MODEL

kernel_name = 'hybrid_moba_gla_moe_adaln_step'


def rmsnorm(x, g):
    xf = x.astype(jnp.float32)
    y = xf * lax.rsqrt(jnp.mean(xf * xf, axis=-1, keepdims=True) + EPS)
    return (y * g.astype(jnp.float32)).astype(x.dtype)


def alibi_slopes():
    return jnp.exp2(-8.0 * (jnp.arange(A_HEADS, dtype=jnp.float32) + 1.0) / A_HEADS)


def to_blocks(k):
    b, l = k.shape[:2]
    nb = max(-(-l // A_BLOCK), A_TOPK)
    kp = jnp.pad(k, ((0, 0), (0, nb * A_BLOCK - l), (0, 0), (0, 0)))
    return kp.reshape(b, nb, A_BLOCK, A_KV_HEADS, A_HEAD_DIM).transpose(0, 3, 1, 2, 4)


def moba_attend(q, pos, kbt, vbt, km):
    b, nq = q.shape[:2]
    nb = kbt.shape[2]
    qf = q.astype(jnp.float32)
    gate = jnp.einsum('bqkgd,bknd->bqkgn', qf.reshape(b, nq, A_KV_HEADS, A_GROUP, A_HEAD_DIM), km).reshape(b, nq, A_HEADS, nb)
    qblk = pos // A_BLOCK
    is_past = jnp.arange(nb)[None, :] < qblk[:, None]
    gate = jnp.where(is_past[None, :, None, :], gate, -jnp.inf)
    _, sel = lax.top_k(gate, A_TOPK)
    own = jnp.broadcast_to(qblk[None, :, None, None], (b, nq, A_HEADS, 1)).astype(sel.dtype)
    idx = jnp.concatenate([sel, own], axis=-1)
    slot_ok = jnp.concatenate([jnp.arange(A_TOPK)[None, :] < qblk[:, None], jnp.ones((nq, 1), bool)], axis=-1)
    bi = jnp.arange(b)[:, None, None, None]
    hi = (jnp.arange(A_HEADS) // A_GROUP)[None, None, :, None]
    kg = kbt[bi, hi, idx].astype(jnp.float32)
    vg = vbt[bi, hi, idx].astype(jnp.float32)
    kpos = idx[..., None] * A_BLOCK + jnp.arange(A_BLOCK)
    dist = (pos[None, :, None, None, None] - kpos).astype(jnp.float32)
    logits = jnp.einsum('bqhd,bqhsld->bqhsl', qf, kg) * (A_HEAD_DIM ** -0.5) - alibi_slopes()[None, None, :, None, None] * dist
    logits = jnp.where(slot_ok[None, :, None, :, None] & (dist >= 0.0), logits, -jnp.inf)
    p = jax.nn.softmax(logits.reshape(b, nq, A_HEADS, -1), axis=-1).reshape(logits.shape)
    out = jnp.einsum('bqhsl,bqhsld->bqhd', p, vg)
    return out.reshape(b, nq, A_QW)


def moba_prompt(q, k, v):
    b, t = q.shape[:2]
    kbt, vbt = to_blocks(k), to_blocks(v)
    km = jnp.mean(kbt.astype(jnp.float32), axis=3)
    nc = t // A_QCHUNK
    qc = q.reshape(b, nc, A_QCHUNK, A_HEADS, A_HEAD_DIM).swapaxes(0, 1)
    pc = jnp.arange(t, dtype=jnp.int32).reshape(nc, A_QCHUNK)
    out = lax.map(lambda qp: moba_attend(qp[0], qp[1], kbt, vbt, km), (qc, pc))
    return out.swapaxes(0, 1).reshape(b, t, A_QW)


def moba_sample(q, k, v, cache_k, cache_v, page_table):
    n_pages = page_table.shape[1]
    past = n_pages * PAGE_SIZE
    tn = q.shape[1]
    pos = past + jnp.arange(tn, dtype=jnp.int32)

    def one(args):
        qs, ks, vs, pt = args
        k_all = jnp.concatenate([cache_k[pt].reshape(past, A_KV_HEADS, A_HEAD_DIM).astype(ks.dtype), ks], axis=0)[None]
        v_all = jnp.concatenate([cache_v[pt].reshape(past, A_KV_HEADS, A_HEAD_DIM).astype(vs.dtype), vs], axis=0)[None]
        kbt, vbt = to_blocks(k_all), to_blocks(v_all)
        km = jnp.mean(kbt.astype(jnp.float32), axis=3)
        return moba_attend(qs[None], pos, kbt, vbt, km)[0]

    return lax.map(one, (q, k, v, page_table))


def gla_chunked(q, k, v, log_alpha, s0):
    b, t = q.shape[:2]
    c = G_CHUNK if t >= G_CHUNK else t
    nc = -(-t // c)
    pad = nc * c - t

    def prep(a):
        a = jnp.pad(a.astype(jnp.float32), ((0, 0), (0, pad), (0, 0), (0, 0)))
        return a.reshape(b, nc, c, *a.shape[2:]).swapaxes(0, 1)

    qs, ks, vs, gs = prep(q) * (G_DK ** -0.5), prep(k), prep(v), prep(log_alpha)
    causal = jnp.tril(jnp.ones((c, c), bool))

    def step(s, inp):
        qc, kc, vc, gc = inp
        cum = jnp.cumsum(gc, axis=1)
        last = cum[:, -1]
        qe = qc * jnp.exp(cum)
        ke = kc * jnp.exp(-cum)
        kd = kc * jnp.exp(last[:, None] - cum)
        att = jnp.where(causal, jnp.einsum('bihd,bjhd->bhij', qe, ke), 0.0)
        o = jnp.einsum('bihd,bhdv->bihv', qe, s) + jnp.einsum('bhij,bjhv->bihv', att, vc)
        s = s * jnp.exp(last)[..., None] + jnp.einsum('bjhd,bjhv->bhdv', kd, vc)
        return s, o

    s, o = lax.scan(step, s0.astype(jnp.float32), (qs, ks, vs, gs))
    o = o.swapaxes(0, 1).reshape(b, nc * c, G_HEADS, G_DV)[:, :t]
    return o, s


def mixer_branches(h, attn_fn, s0, w_in, w_gk2, b_gk2, g_gla_norm, w_br_a, w_br_b, w_out):
    b, t, _ = h.shape
    split_points = [int(i) for i in np.cumsum(SPLITS)[:-1]]
    aq, ak, av, gq, gk, gv, gg, glr, za, zb = jnp.split(h @ w_in, split_points, axis=-1)
    ak = ak.reshape(b, t, A_KV_HEADS, A_HEAD_DIM)
    av = av.reshape(b, t, A_KV_HEADS, A_HEAD_DIM)
    a = attn_fn(aq.reshape(b, t, A_HEADS, A_HEAD_DIM), ak, av).astype(h.dtype)
    log_alpha = jax.nn.log_sigmoid((glr @ w_gk2 + b_gk2).astype(jnp.float32)) / G_NORMALIZER
    o, s = gla_chunked(gq.reshape(b, t, G_HEADS, G_DK), gk.reshape(b, t, G_HEADS, G_DK),
                       gv.reshape(b, t, G_HEADS, G_DV), log_alpha.reshape(b, t, G_HEADS, G_DK), s0)
    o = o * lax.rsqrt(jnp.mean(o * o, axis=-1, keepdims=True) + EPS) * g_gla_norm.astype(jnp.float32)
    o = (o.reshape(b, t, G_VAL_DIM) * jax.nn.silu(gg.astype(jnp.float32))).astype(h.dtype)
    merged = jax.nn.sigmoid(za) * (a @ w_br_a) + jax.nn.sigmoid(zb) * (o @ w_br_b)
    return merged @ w_out, ak, av, s.astype(h.dtype)


def moe_ffn(x, w_router, b_router, w_gu, b_gu, w_down, b_down):
    n = x.shape[0]
    logits = (x @ w_router + b_router).astype(jnp.float32)
    topv, topi = lax.top_k(logits, TOP_K)
    wts = jax.nn.softmax(topv, axis=-1)
    flat_e = topi.reshape(-1)
    flat_tok = jnp.arange(n * TOP_K, dtype=jnp.int32) // TOP_K
    flat_w = wts.reshape(-1)
    order = jnp.argsort(flat_e)
    se = flat_e[order]
    counts = jnp.bincount(flat_e, length=N_EXPERTS)
    padded = (counts + MOE_BLOCK - 1) // MOE_BLOCK * MOE_BLOCK
    pad_end = jnp.cumsum(padded)
    pad_start = pad_end - padded
    start = jnp.cumsum(counts) - counts
    dest = pad_start[se] + jnp.arange(n * TOP_K) - start[se]
    n_blocks = -(-(n * TOP_K) // MOE_BLOCK) + N_EXPERTS
    rows = n_blocks * MOE_BLOCK
    row_tok = jnp.full((rows,), n, jnp.int32).at[dest].set(flat_tok[order])
    row_w = jnp.zeros((rows,), jnp.float32).at[dest].set(flat_w[order])
    blk_e = jnp.minimum(jnp.searchsorted(pad_end, jnp.arange(n_blocks) * MOE_BLOCK, side='right'), N_EXPERTS - 1)
    xb = jnp.concatenate([x, jnp.zeros((1, x.shape[1]), x.dtype)], axis=0)[row_tok].reshape(n_blocks, MOE_BLOCK, -1)

    def expert_block(args):
        xe, e = args
        gu = (xe @ w_gu[e] + b_gu[e]).astype(jnp.float32)
        g = jnp.minimum(gu[:, :D_FF], SWIGLU_LIMIT)
        u = jnp.clip(gu[:, D_FF:], -SWIGLU_LIMIT, SWIGLU_LIMIT)
        act = ((u + 1.0) * g * jax.nn.sigmoid(SWIGLU_ALPHA * g)).astype(xe.dtype)
        return act @ w_down[e] + b_down[e]

    yb = lax.map(expert_block, (xb, blk_e)).reshape(rows, -1)
    out = jax.ops.segment_sum(yb.astype(jnp.float32) * row_w[:, None], row_tok, num_segments=n + 1)[:n]
    return out.astype(x.dtype)


def trunk_layer(x, c, attn_fn, s0, w_ada, b_ada, g_norm_mix, w_in, w_gk2, b_gk2, g_gla_norm, w_br_a, w_br_b,
                w_out, g_norm_ffn, w_router, b_router, w_gu, b_gu, w_down, b_down):
    b, t, d = x.shape
    mod = (jax.nn.silu(c) @ w_ada + b_ada)[:, None, :]
    sh1, sc1, ga1, sh2, sc2, ga2 = jnp.split(mod, 6, axis=-1)
    h = rmsnorm(x, g_norm_mix) * (1.0 + sc1) + sh1
    mix, k, v, s = mixer_branches(h, attn_fn, s0, w_in, w_gk2, b_gk2, g_gla_norm, w_br_a, w_br_b, w_out)
    x = x + ga1 * mix
    h = rmsnorm(x, g_norm_ffn) * (1.0 + sc2) + sh2
    x = x + ga2 * moe_ffn(h.reshape(b * t, d), w_router, b_router, w_gu, b_gu, w_down, b_down).reshape(b, t, d)
    return x, k, v, s


def setup_inputs(seed: int = 0) -> dict:
    key = jax.random.key(seed)
    ks = jax.random.split(key, 26)
    n_pages = PAST_LEN // PAGE_SIZE
    n_used = DEC_BATCH * n_pages
    n_pool = n_used + n_used // 4
    f32 = jnp.float32

    def nrm(k, shape, scale):
        return jax.random.normal(k, shape, f32) * scale

    L = DEPTH
    return {
        'x_prompt': nrm(ks[0], (BATCH, SEQ, D_MODEL), 1.0),
        'x_sample': nrm(ks[1], (DEC_BATCH, DEC_SEQ, D_MODEL), 1.0),
        'c_prompt': nrm(ks[2], (BATCH, D_MODEL), 1.0),
        'c_sample': nrm(ks[3], (DEC_BATCH, D_MODEL), 1.0),
        'cache_k': nrm(ks[4], (L, n_pool, PAGE_SIZE, A_KV_HEADS, A_HEAD_DIM), 1.0),
        'cache_v': nrm(ks[5], (L, n_pool, PAGE_SIZE, A_KV_HEADS, A_HEAD_DIM), 1.0),
        'state_gla': nrm(ks[6], (L, DEC_BATCH, G_HEADS, G_DK, G_DV), 0.5),
        'page_table': jax.random.permutation(ks[7], n_pool)[:n_used].reshape(DEC_BATCH, n_pages).astype(jnp.int32),
        'w_ada': nrm(ks[8], (L, D_MODEL, 6 * D_MODEL), 0.5 * D_MODEL ** -0.5),
        'b_ada': nrm(ks[9], (L, 6 * D_MODEL), 0.02),
        'g_norm_mix': 1.0 + nrm(ks[10], (L, D_MODEL), 0.01),
        'w_in': nrm(ks[11], (L, D_MODEL, IN_WIDTH), D_MODEL ** -0.5),
        'w_gk2': nrm(ks[12], (L, G_LOWRANK, G_KEY_DIM), G_LOWRANK ** -0.5),
        'b_gk2': nrm(ks[13], (L, G_KEY_DIM), 0.1),
        'g_gla_norm': 1.0 + nrm(ks[14], (L, G_DV), 0.01),
        'w_br_a': nrm(ks[15], (L, A_QW, D_MODEL), A_QW ** -0.5),
        'w_br_b': nrm(ks[16], (L, G_VAL_DIM, D_MODEL), G_VAL_DIM ** -0.5),
        'w_out': nrm(ks[17], (L, D_MODEL, D_MODEL), D_MODEL ** -0.5),
        'g_norm_ffn': 1.0 + nrm(ks[18], (L, D_MODEL), 0.01),
        'w_router': nrm(ks[19], (L, D_MODEL, N_EXPERTS), D_MODEL ** -0.5),
        'b_router': nrm(ks[20], (L, N_EXPERTS), 0.01),
        'w_gu': nrm(ks[21], (L, N_EXPERTS, D_MODEL, 2 * D_FF), D_MODEL ** -0.5),
        'b_gu': nrm(ks[22], (L, N_EXPERTS, 2 * D_FF), 0.01),
        'w_down': nrm(ks[23], (L, N_EXPERTS, D_FF, D_MODEL), D_FF ** -0.5),
        'b_down': nrm(ks[24], (L, N_EXPERTS, D_MODEL), 0.01),
        'g_final': 1.0 + nrm(ks[25], (D_MODEL,), 0.01),
    }


def reference(x_prompt, x_sample, c_prompt, c_sample, cache_k, cache_v, state_gla, page_table, w_ada, b_ada,
              g_norm_mix, w_in, w_gk2, b_gk2, g_gla_norm, w_br_a, w_br_b, w_out, g_norm_ffn, w_router, b_router,
              w_gu, b_gu, w_down, b_down, g_final):
    xp, xs = x_prompt, x_sample
    kp_l, vp_l, sp_l, ks_l, vs_l, ss_l = [], [], [], [], [], []
    for l in range(DEPTH):
        lw = (w_ada[l], b_ada[l], g_norm_mix[l], w_in[l], w_gk2[l], b_gk2[l], g_gla_norm[l], w_br_a[l], w_br_b[l],
              w_out[l], g_norm_ffn[l], w_router[l], b_router[l], w_gu[l], b_gu[l], w_down[l], b_down[l])
        s0 = jnp.zeros((xp.shape[0], G_HEADS, G_DK, G_DV), jnp.float32)
        xp, kp, vp, sp = trunk_layer(xp, c_prompt, moba_prompt, s0, *lw)
        attn_s = functools.partial(moba_sample, cache_k=cache_k[l], cache_v=cache_v[l], page_table=page_table)
        xs, ksm, vsm, ssm = trunk_layer(xs, c_sample, attn_s, state_gla[l], *lw)
        kp_l.append(kp); vp_l.append(vp); sp_l.append(sp)
        ks_l.append(ksm); vs_l.append(vsm); ss_l.append(ssm)
    y_prompt = rmsnorm(xp, g_final)
    y_sample = rmsnorm(xs, g_final)
    return (y_prompt, y_sample, jnp.stack(kp_l), jnp.stack(vp_l), jnp.stack(sp_l),
            jnp.stack(ks_l), jnp.stack(vs_l), jnp.stack(ss_l))
```

```python
import functools

import jax
import jax.numpy as jnp
import numpy as np
from jax import lax
from jax.experimental import pallas as pl
from jax.experimental.pallas import tpu as pltpu

F32 = jnp.float32
BF16 = jnp.bfloat16

D_MODEL = 1024
A_HEADS = 8
A_KV_HEADS = 4
A_GROUP = A_HEADS // A_KV_HEADS
A_HEAD_DIM = 64
A_QW = A_HEADS * A_HEAD_DIM
A_KVW = A_KV_HEADS * A_HEAD_DIM
A_BLOCK = 256
A_TOPK = 3
PAGE_SIZE = 128
G_HEADS = 4
G_KEY_DIM = D_MODEL // 2
G_VAL_DIM = D_MODEL
G_DK = G_KEY_DIM // G_HEADS
G_DV = G_VAL_DIM // G_HEADS
G_LOWRANK = 16
G_NORMALIZER = 16.0
G_CHUNK = 64
N_EXPERTS = 32
TOP_K = 4
D_FF = D_MODEL
SWIGLU_LIMIT = 7.0
SWIGLU_ALPHA = 1.702
MOE_BLOCK = 128
EPS = 1e-6

LANES = 128
COL_AQ = 0
COL_AK = COL_AQ + A_QW
COL_AV = COL_AK + A_KVW
COL_GQ = COL_AV + A_KVW
COL_GK = COL_GQ + G_KEY_DIM
COL_GV = COL_GK + G_KEY_DIM
COL_GG = COL_GV + G_VAL_DIM
COL_ZA = COL_GG + G_VAL_DIM
COL_ZB = COL_ZA + D_MODEL
COL_LR = COL_ZB + D_MODEL
PROJ_W = COL_LR + LANES
PROJ_TN = 896
NEG_BIG = -1e30
VMEM_LIMIT = 56 * 1024 * 1024

NT_DIMS = (((1,), (1,)), ((), ()))
NN_DIMS = (((1,), (0,)), ((), ()))


def _params(sem):
    return pltpu.CompilerParams(dimension_semantics=sem, vmem_limit_bytes=VMEM_LIMIT)


def _split(a):
    hi = a.astype(BF16)
    lo = (a - hi.astype(F32)).astype(BF16)
    return hi, lo


def _dot3(a, b, dims=NN_DIMS):
    ah, al = _split(a)
    bh, bl = _split(b)
    d = lambda x, y: lax.dot_general(x, y, dims, preferred_element_type=F32)
    return d(ah, bh) + d(al, bh) + d(ah, bl)


def _sigmoid(x):
    return 1.0 / (1.0 + jnp.exp(-x))


def _log_sigmoid(x):
    return -(jnp.maximum(-x, 0.0) + jnp.log(1.0 + jnp.exp(-jnp.abs(x))))


def _rmsnorm(x, g):
    return x * lax.rsqrt(jnp.mean(x * x, axis=-1, keepdims=True) + EPS) * g


def _top_select(score, colf, n_pick, n_valid=None):
    picked = jnp.zeros(score.shape, jnp.bool_)
    vals, cols = [], []
    big = float(score.shape[1])
    for r in range(n_pick):
        mx = jnp.max(score, axis=1, keepdims=True)
        first = jnp.min(jnp.where(score == mx, colf, big), axis=1, keepdims=True)
        hit = colf == first
        if n_valid is not None:
            picked = picked | (hit & ((jnp.zeros_like(colf) + n_valid) > r))
        else:
            picked = picked | hit
        vals.append(mx)
        cols.append(first)
        score = jnp.where(hit, -jnp.inf, score)
    return picked, vals, cols


def _ada_kernel(c_ref, w_ref, b_ref, o_ref):
    c = c_ref[...]
    o_ref[...] = _dot3(c * _sigmoid(c), w_ref[...]) + b_ref[...]


def _ada_mod(c, w_ada, b_ada):
    rows = c.shape[0]
    tn = 1536
    return pl.pallas_call(
        _ada_kernel,
        grid=(6 * D_MODEL // tn,),
        in_specs=[
            pl.BlockSpec((rows, D_MODEL), lambda j: (0, 0)),
            pl.BlockSpec((D_MODEL, tn), lambda j: (0, j)),
            pl.BlockSpec((1, tn), lambda j: (0, j)),
        ],
        out_specs=pl.BlockSpec((rows, tn), lambda j: (0, j)),
        out_shape=jax.ShapeDtypeStruct((rows, 6 * D_MODEL), F32),
        compiler_params=_params(("arbitrary",)),
        name="ada_mod",
    )(c, w_ada, b_ada.reshape(1, -1))


def _inproj_kernel(x_ref, sc_ref, sh_ref, g_ref, w_ref, o_ref, h_ref):
    @pl.when(pl.program_id(2) == 0)
    def _():
        y = _rmsnorm(x_ref[0], g_ref[...])
        h_ref[...] = (y * (1.0 + sc_ref[0]) + sh_ref[0]).astype(BF16)

    o_ref[0] = jnp.dot(h_ref[...], w_ref[...], preferred_element_type=F32)


def _inproj(x, sc, sh, g, w_perm, tm):
    G, T, _ = x.shape
    R = sc.shape[1]
    rm = 1 if R == 1 else tm
    mod_map = (lambda b, i, j: (b, 0, 0)) if R == 1 else (lambda b, i, j: (b, i, 0))
    return pl.pallas_call(
        _inproj_kernel,
        grid=(G, T // tm, PROJ_W // PROJ_TN),
        in_specs=[
            pl.BlockSpec((1, tm, D_MODEL), lambda b, i, j: (b, i, 0)),
            pl.BlockSpec((1, rm, D_MODEL), mod_map),
            pl.BlockSpec((1, rm, D_MODEL), mod_map),
            pl.BlockSpec((1, D_MODEL), lambda b, i, j: (0, 0)),
            pl.BlockSpec((D_MODEL, PROJ_TN), lambda b, i, j: (0, j)),
        ],
        out_specs=pl.BlockSpec((1, tm, PROJ_TN), lambda b, i, j: (b, i, j)),
        out_shape=jax.ShapeDtypeStruct((G, T, PROJ_W), F32),
        scratch_shapes=[pltpu.VMEM((tm, D_MODEL), BF16)],
        compiler_params=_params(("arbitrary", "arbitrary", "arbitrary")),
        name="inproj",
    )(x, sc, sh, g.reshape(1, -1), w_perm)


PREP_T = 2048
AUG_BIAS = A_HEAD_DIM
AUG_SEL = A_HEAD_DIM + 4


def _moba_prep_kernel(k_ref, v_ref, ka_ref, va_ref, km_ref):
    t = pl.program_id(1)
    k = k_ref[0]
    v = v_ref[0]
    nblk = PREP_T // A_BLOCK
    km_ref[0] = jnp.mean(k.reshape(nblk, A_BLOCK, A_KVW), axis=1)
    half = LANES - A_HEAD_DIM
    row = lax.broadcasted_iota(jnp.int32, (PREP_T, half), 0)
    col = lax.broadcasted_iota(jnp.int32, (PREP_T, half), 1)
    blk = t * nblk + row // A_BLOCK
    kk = (row % A_BLOCK).astype(F32)
    kextra = jnp.where(col < 2, 1.0, 0.0)
    kextra = jnp.where(col == 2, kk, kextra)
    kextra = jnp.where(col == 3, (blk * A_BLOCK).astype(F32), kextra)
    kextra = jnp.where(col - 4 == blk, 1.0, kextra)
    vextra = jnp.where(col == 0, 1.0, 0.0)
    for h in range(A_KV_HEADS):
        sl = slice(h * A_HEAD_DIM, (h + 1) * A_HEAD_DIM)
        ka_ref[0, h] = jnp.concatenate([k[:, sl], kextra], axis=1).astype(BF16)
        va_ref[0, h] = jnp.concatenate([v[:, sl], vextra], axis=1).astype(BF16)


def _moba_prep(proj):
    B, T, _ = proj.shape
    nblk = PREP_T // A_BLOCK
    return pl.pallas_call(
        _moba_prep_kernel,
        grid=(B, T // PREP_T),
        in_specs=[
            pl.BlockSpec((1, PREP_T, A_KVW), lambda b, t: (b, t, COL_AK // A_KVW)),
            pl.BlockSpec((1, PREP_T, A_KVW), lambda b, t: (b, t, COL_AV // A_KVW)),
        ],
        out_specs=[
            pl.BlockSpec((1, A_KV_HEADS, PREP_T, LANES), lambda b, t: (b, 0, t, 0)),
            pl.BlockSpec((1, A_KV_HEADS, PREP_T, LANES), lambda b, t: (b, 0, t, 0)),
            pl.BlockSpec((1, nblk, A_KVW), lambda b, t: (b, t, 0)),
        ],
        out_shape=[
            jax.ShapeDtypeStruct((B, A_KV_HEADS, T, LANES), BF16),
            jax.ShapeDtypeStruct((B, A_KV_HEADS, T, LANES), BF16),
            jax.ShapeDtypeStruct((B, T // A_BLOCK, A_KVW), F32),
        ],
        compiler_params=_params(("arbitrary", "arbitrary")),
        name="moba_prep",
    )(proj, proj)


def _slope_of(head_plus_one, shape):
    bits = (127 - head_plus_one) << 23
    return lax.bitcast_convert_type(jnp.full(shape, bits, jnp.int32), F32)


def _moba_attn_kernel(q_ref, km_ref, ka_ref, va_ref, o_ref, qa_ref, m_ref, acc_ref):
    kvh = pl.program_id(1)
    i = pl.program_id(2)
    nb = km_ref.shape[2]
    q = q_ref[0]
    km = km_ref[0, 0]
    half = LANES - A_HEAD_DIM
    colf = lax.broadcasted_iota(jnp.int32, (A_BLOCK, nb), 1).astype(F32)
    blk_f = i.astype(F32)
    ecol = lax.broadcasted_iota(jnp.int32, (A_BLOCK, half), 1)
    qq = lax.broadcasted_iota(jnp.int32, (A_BLOCK, half), 0).astype(F32)
    for hh in range(A_GROUP):
        qh = q[:, hh * A_HEAD_DIM:(hh + 1) * A_HEAD_DIM]
        gate = _dot3(qh, km, NT_DIMS)
        gate = jnp.where(colf < blk_f, gate, -jnp.inf)
        picked, _, _ = _top_select(gate, colf, A_TOPK, n_valid=blk_f)
        picked = picked | (colf == blk_f)
        selbias = jnp.where(picked, 0.0, NEG_BIG)
        slope = _slope_of(kvh * A_GROUP + hh + 1, (A_BLOCK, half))
        extra = jnp.where(ecol == 0, -slope * qq, 0.0)
        extra = jnp.where(ecol == 1, -slope * (blk_f * A_BLOCK), extra)
        extra = jnp.where((ecol == 2) | (ecol == 3), slope, extra)
        pad = jnp.zeros((A_BLOCK, half - 4 - nb), F32)
        extra = jnp.concatenate([extra[:, :4], selbias, pad], axis=1)
        qa_ref[hh] = jnp.concatenate([qh * (A_HEAD_DIM ** -0.5), extra], axis=1).astype(BF16)

    r0 = pl.multiple_of(i * A_BLOCK, A_BLOCK)
    kb = ka_ref[0, 0, pl.ds(r0, A_BLOCK), :]
    vb = va_ref[0, 0, pl.ds(r0, A_BLOCK), :]
    rr = lax.broadcasted_iota(jnp.int32, (A_BLOCK, A_BLOCK), 0)
    cc = lax.broadcasted_iota(jnp.int32, (A_BLOCK, A_BLOCK), 1)
    for hh in range(A_GROUP):
        s = lax.dot_general(qa_ref[hh], kb, NT_DIMS, preferred_element_type=F32)
        s = jnp.where(cc <= rr, s, NEG_BIG)
        m = jnp.max(s, axis=1, keepdims=True)
        p = jnp.exp(s - m)
        m_ref[hh] = m
        acc_ref[hh] = jnp.dot(p.astype(BF16), vb, preferred_element_type=F32)

    def past_block(j, carry):
        rj = pl.multiple_of(j * A_BLOCK, A_BLOCK)
        kj = ka_ref[0, 0, pl.ds(rj, A_BLOCK), :]
        vj = va_ref[0, 0, pl.ds(rj, A_BLOCK), :]
        for hh in range(A_GROUP):
            s = lax.dot_general(qa_ref[hh], kj, NT_DIMS, preferred_element_type=F32)
            m_old = m_ref[hh]
            m_new = jnp.maximum(m_old, jnp.max(s, axis=1, keepdims=True))
            p = jnp.exp(s - m_new)
            acc_ref[hh] = jnp.exp(m_old - m_new) * acc_ref[hh] + jnp.dot(
                p.astype(BF16), vj, preferred_element_type=F32)
            m_ref[hh] = m_new
        return carry

    lax.fori_loop(0, i, past_block, 0)
    outs = []
    for hh in range(A_GROUP):
        acc = acc_ref[hh]
        outs.append(acc[:, :A_HEAD_DIM] / acc[:, A_HEAD_DIM:A_HEAD_DIM + 1])
    o_ref[0] = jnp.concatenate(outs, axis=1).astype(BF16)


def _moba_prompt(proj):
    B, T, _ = proj.shape
    nb = T // A_BLOCK
    kaug, vaug, km = _moba_prep(proj)
    km = km.reshape(B, nb, A_KV_HEADS, A_HEAD_DIM).transpose(0, 2, 1, 3)
    qw = A_GROUP * A_HEAD_DIM
    return pl.pallas_call(
        _moba_attn_kernel,
        grid=(B, A_KV_HEADS, nb),
        in_specs=[
            pl.BlockSpec((1, A_BLOCK, qw), lambda b, h, i: (b, i, h)),
            pl.BlockSpec((1, 1, nb, A_HEAD_DIM), lambda b, h, i: (b, h, 0, 0)),
            pl.BlockSpec((1, 1, T, LANES), lambda b, h, i: (b, h, 0, 0)),
            pl.BlockSpec((1, 1, T, LANES), lambda b, h, i: (b, h, 0, 0)),
        ],
        out_specs=pl.BlockSpec((1, A_BLOCK, qw), lambda b, h, i: (b, i, h)),
        out_shape=jax.ShapeDtypeStruct((B, T, A_QW), BF16),
        scratch_shapes=[
            pltpu.VMEM((A_GROUP, A_BLOCK, LANES), BF16),
            pltpu.VMEM((A_GROUP, A_BLOCK, 1), F32),
            pltpu.VMEM((A_GROUP, A_BLOCK, LANES), F32),
        ],
        compiler_params=_params(("arbitrary", "arbitrary", "arbitrary")),
        name="moba_attn",
    )(proj, km, kaug, vaug)


GLA_T = 1024


def _gla_kernel(q_ref, k_ref, v_ref, gg_ref, lr_ref, w2_ref, b2_ref, gn_ref, og_ref, st_ref, s_ref, g_ref):
    t = pl.program_id(2)

    @pl.when(t == 0)
    def _():
        s_ref[...] = jnp.zeros_like(s_ref)

    z = _dot3(lr_ref[0][:, :G_LOWRANK], w2_ref[...]) + b2_ref[...]
    g_ref[...] = _log_sigmoid(z) * (1.0 / G_NORMALIZER)
    C = G_CHUNK
    rr = lax.broadcasted_iota(jnp.int32, (C, C), 0)
    cc = lax.broadcasted_iota(jnp.int32, (C, C), 1)
    causal = cc <= rr
    tril = jnp.where(causal, 1.0, 0.0).astype(BF16)
    er = lax.broadcasted_iota(jnp.int32, (G_DK, G_DK), 0)
    ec = lax.broadcasted_iota(jnp.int32, (G_DK, G_DK), 1)
    eye = er == ec
    gn = gn_ref[...]

    def chunk(c, carry):
        sl = pl.ds(pl.multiple_of(c * C, C), C)
        gc = g_ref[sl, :]
        gh, gl = _split(gc)
        cum = jnp.dot(tril, gh, preferred_element_type=F32) + jnp.dot(tril, gl, preferred_element_type=F32)
        last = cum[C - 1:C, :]
        kc = k_ref[0, sl, :]
        vc = v_ref[0, sl, :].astype(BF16)
        qe = (q_ref[0, sl, :] * (G_DK ** -0.5) * jnp.exp(cum)).astype(BF16)
        ke = (kc * jnp.exp(-cum)).astype(BF16)
        kd = kc * jnp.exp(last - cum)
        att = lax.dot_general(qe, ke, NT_DIMS, preferred_element_type=F32)
        att = jnp.where(causal, att, 0.0).astype(BF16)
        s = s_ref[...]
        o = jnp.dot(qe, s.astype(BF16), preferred_element_type=F32) + jnp.dot(att, vc, preferred_element_type=F32)
        decay = jnp.sum(jnp.where(eye, jnp.exp(last), 0.0), axis=1, keepdims=True)
        s_ref[...] = s * decay + jnp.dot(kd.T.astype(BF16), vc, preferred_element_type=F32)
        on = _rmsnorm(o, gn)
        gate = gg_ref[0, sl, :]
        og_ref[0, sl, :] = (on * (gate * _sigmoid(gate))).astype(BF16)
        return carry

    lax.fori_loop(0, GLA_T // C, chunk, 0)

    @pl.when(t == pl.num_programs(2) - 1)
    def _():
        st_ref[0, 0] = s_ref[...]


def _gla_prompt(proj, w_gk2, b_gk2, g_gla_norm):
    B, T, _ = proj.shape
    tt = min(GLA_T, T)
    assert tt == GLA_T
    return pl.pallas_call(
        _gla_kernel,
        grid=(B, G_HEADS, T // tt),
        in_specs=[
            pl.BlockSpec((1, tt, G_DK), lambda b, h, t: (b, t, COL_GQ // G_DK + h)),
            pl.BlockSpec((1, tt, G_DK), lambda b, h, t: (b, t, COL_GK // G_DK + h)),
            pl.BlockSpec((1, tt, G_DV), lambda b, h, t: (b, t, COL_GV // G_DV + h)),
            pl.BlockSpec((1, tt, G_DV), lambda b, h, t: (b, t, COL_GG // G_DV + h)),
            pl.BlockSpec((1, tt, LANES), lambda b, h, t: (b, t, COL_LR // LANES)),
            pl.BlockSpec((G_LOWRANK, G_DK), lambda b, h, t: (0, h)),
            pl.BlockSpec((1, G_DK), lambda b, h, t: (0, h)),
            pl.BlockSpec((1, G_DV), lambda b, h, t: (0, 0)),
        ],
        out_specs=[
            pl.BlockSpec((1, tt, G_DV), lambda b, h, t: (b, t, h)),
            pl.BlockSpec((1, 1, G_DK, G_DV), lambda b, h, t: (b, h, 0, 0)),
        ],
        out_shape=[
            jax.ShapeDtypeStruct((B, T, G_VAL_DIM), BF16),
            jax.ShapeDtypeStruct((B, G_HEADS, G_DK, G_DV), F32),
        ],
        scratch_shapes=[pltpu.VMEM((G_DK, G_DV), F32), pltpu.VMEM((tt, G_DK), F32)],
        compiler_params=_params(("arbitrary", "arbitrary", "arbitrary")),
        name="gla_prompt",
    )(proj, proj, proj, proj, proj, w_gk2, b_gk2.reshape(1, -1), g_gla_norm.reshape(1, -1))


def _smoba_kernel(pt_ref, qm_ref, kn_ref, vn_ref, ex_ref, ck_ref, cv_ref, o_ref, kbuf, vbuf, sem):
    s = pl.program_id(0)
    n_seq = pl.num_programs(0)
    n_pages = kbuf.shape[1]
    past = n_pages * PAGE_SIZE
    nb = past // A_BLOCK

    def copies(seq, slot, p):
        return (
            pltpu.make_async_copy(ck_ref.at[pt_ref[seq, p]], kbuf.at[slot, p], sem.at[0, slot]),
            pltpu.make_async_copy(cv_ref.at[pt_ref[seq, p]], vbuf.at[slot, p], sem.at[1, slot]),
        )

    def start(seq, slot):
        for p in range(n_pages):
            for c in copies(seq, slot, p):
                c.start()

    @pl.when(s == 0)
    def _():
        start(0, 0)

    slot = s % 2

    @pl.when(s + 1 < n_seq)
    def _():
        start(s + 1, 1 - slot)

    pltpu.make_async_copy(ck_ref.at[pl.ds(0, n_pages)], kbuf.at[slot], sem.at[0, slot]).wait()
    pltpu.make_async_copy(cv_ref.at[pl.ds(0, n_pages)], vbuf.at[slot], sem.at[1, slot]).wait()

    kf = kbuf[slot].reshape(past, A_KVW)
    vf = vbuf[slot].reshape(past, A_KVW)
    km = jnp.mean(kf.reshape(nb, A_BLOCK, A_KVW), axis=1)
    qm = qm_ref[0]
    gate = _dot3(qm, km, NT_DIMS)
    colf = lax.broadcasted_iota(jnp.int32, (A_HEADS, nb), 1).astype(F32)
    picked, _, _ = _top_select(gate, colf, A_TOPK)
    sel = jnp.where(picked, 1.0, 0.0).astype(BF16)
    keymask = jnp.dot(sel, ex_ref[...], preferred_element_type=F32) > 0.5
    qs = qm * (A_HEAD_DIM ** -0.5)
    logits = lax.dot_general(qs.astype(BF16), kf.astype(BF16), NT_DIMS, preferred_element_type=F32)
    tpos = lax.broadcasted_iota(jnp.int32, (A_HEADS, past), 1)
    dist = (past - tpos).astype(F32)
    head1 = lax.broadcasted_iota(jnp.int32, (A_HEADS, 1), 0) + 1
    slope = lax.bitcast_convert_type((127 - head1) << 23, F32)
    logits = jnp.where(keymask, logits - slope * dist, NEG_BIG)
    own = jnp.sum(qs * kn_ref[0], axis=1, keepdims=True)
    m = jnp.maximum(jnp.max(logits, axis=1, keepdims=True), own)
    p = jnp.exp(logits - m)
    p_own = jnp.exp(own - m)
    denom = jnp.sum(p, axis=1, keepdims=True) + p_own
    num = jnp.dot(p.astype(BF16), vf.astype(BF16), preferred_element_type=F32) + p_own * vn_ref[0]
    o_ref[0] = num / denom


def _moba_sample(q, k_new, v_new, cache_k, cache_v, page_table):
    S = q.shape[0]
    n_pages = page_table.shape[1]
    past = n_pages * PAGE_SIZE
    nb = past // A_BLOCK
    qh = q.reshape(S, A_HEADS, A_HEAD_DIM)
    place = (jnp.arange(A_KV_HEADS)[None, :] == (jnp.arange(A_HEADS) // A_GROUP)[:, None]).astype(F32)
    qm = (qh[:, :, None, :] * place[None, :, :, None]).reshape(S, A_HEADS, A_KVW)
    expand = (jnp.arange(past)[None, :] // A_BLOCK == jnp.arange(nb)[:, None]).astype(BF16)
    out = pl.pallas_call(
        _smoba_kernel,
        grid_spec=pltpu.PrefetchScalarGridSpec(
            num_scalar_prefetch=1,
            grid=(S,),
            in_specs=[
                pl.BlockSpec((1, A_HEADS, A_KVW), lambda s, pt: (s, 0, 0)),
                pl.BlockSpec((1, 1, A_KVW), lambda s, pt: (s, 0, 0)),
                pl.BlockSpec((1, 1, A_KVW), lambda s, pt: (s, 0, 0)),
                pl.BlockSpec((nb, past), lambda s, pt: (0, 0)),
                pl.BlockSpec(memory_space=pl.ANY),
                pl.BlockSpec(memory_space=pl.ANY),
            ],
            out_specs=pl.BlockSpec((1, A_HEADS, A_KVW), lambda s, pt: (s, 0, 0)),
            scratch_shapes=[
                pltpu.VMEM((2, n_pages, PAGE_SIZE, A_KVW), F32),
                pltpu.VMEM((2, n_pages, PAGE_SIZE, A_KVW), F32),
                pltpu.SemaphoreType.DMA((2, 2)),
            ],
        ),
        out_shape=jax.ShapeDtypeStruct((S, A_HEADS, A_KVW), F32),
        compiler_params=_params(("arbitrary",)),
        name="moba_sample",
    )(page_table, qm, k_new.reshape(S, 1, A_KVW), v_new.reshape(S, 1, A_KVW), expand, cache_k, cache_v)
    o4 = out.reshape(S, A_HEADS, A_KV_HEADS, A_HEAD_DIM)
    return jnp.einsum("shkd,hk->shd", o4, place).reshape(S, A_QW)


def _sgla_kernel(p_ref, s0_ref, w2_ref, b2_ref, gn_ref, og_ref, s1_ref):
    row = p_ref[0]
    er = lax.broadcasted_iota(jnp.int32, (G_DK, G_DK), 0)
    ec = lax.broadcasted_iota(jnp.int32, (G_DK, G_DK), 1)
    eye = er == ec
    col_of = lambda r: jnp.sum(jnp.where(eye, r, 0.0), axis=1, keepdims=True)
    lr = row[:, COL_LR:COL_LR + G_LOWRANK]
    z = _dot3(lr, w2_ref[...]) + b2_ref[...]
    g_all = _log_sigmoid(z) * (1.0 / G_NORMALIZER)
    outs = []
    for h in range(G_HEADS):
        g = g_all[:, h * G_DK:(h + 1) * G_DK]
        q = row[:, COL_GQ + h * G_DK:COL_GQ + (h + 1) * G_DK] * (G_DK ** -0.5)
        k = row[:, COL_GK + h * G_DK:COL_GK + (h + 1) * G_DK]
        v = row[:, COL_GV + h * G_DV:COL_GV + (h + 1) * G_DV]
        gate = row[:, COL_GG + h * G_DV:COL_GG + (h + 1) * G_DV]
        qe = q * jnp.exp(g)
        ke = k * jnp.exp(-g)
        att = jnp.sum(qe * ke, axis=1, keepdims=True)
        s0 = s0_ref[0, h]
        o = _dot3(qe, s0) + att * v
        s1_ref[0, h] = s0 * col_of(jnp.exp(g)) + col_of(k) * v
        on = _rmsnorm(o, gn_ref[...])
        outs.append(on * (gate * _sigmoid(gate)))
    og_ref[0] = jnp.concatenate(outs, axis=1)


def _gla_sample(proj_rows, state, w_gk2, b_gk2, g_gla_norm):
    S = proj_rows.shape[0]
    return pl.pallas_call(
        _sgla_kernel,
        grid=(S,),
        in_specs=[
            pl.BlockSpec((1, 1, PROJ_W), lambda s: (s, 0, 0)),
            pl.BlockSpec((1, G_HEADS, G_DK, G_DV), lambda s: (s, 0, 0, 0)),
            pl.BlockSpec((G_LOWRANK, G_KEY_DIM), lambda s: (0, 0)),
            pl.BlockSpec((1, G_KEY_DIM), lambda s: (0, 0)),
            pl.BlockSpec((1, G_DV), lambda s: (0, 0)),
        ],
        out_specs=[
            pl.BlockSpec((1, 1, G_VAL_DIM), lambda s: (s, 0, 0)),
            pl.BlockSpec((1, G_HEADS, G_DK, G_DV), lambda s: (s, 0, 0, 0)),
        ],
        out_shape=[
            jax.ShapeDtypeStruct((S, 1, G_VAL_DIM), F32),
            jax.ShapeDtypeStruct((S, G_HEADS, G_DK, G_DV), F32),
        ],
        compiler_params=_params(("arbitrary",)),
        name="gla_sample",
    )(proj_rows, state, w_gk2, b_gk2.reshape(1, -1), g_gla_norm.reshape(1, -1))


def _merge_kernel(x_ref, a_ref, og_ref, za_ref, zb_ref, ga1_ref, sc2_ref, sh2_ref, g2_ref, wa_ref, wb_ref,
                  wo_ref, wr_ref, br_ref, x1_ref, h2_ref, ti_ref, tw_ref):
    pa = jnp.dot(a_ref[0], wa_ref[...], preferred_element_type=F32)
    pb = jnp.dot(og_ref[0], wb_ref[...], preferred_element_type=F32)
    merged = _sigmoid(za_ref[0]) * pa + _sigmoid(zb_ref[0]) * pb
    mix = jnp.dot(merged.astype(BF16), wo_ref[...], preferred_element_type=F32)
    x1 = x_ref[0] + ga1_ref[0] * mix
    x1_ref[0] = x1
    h2 = _rmsnorm(x1, g2_ref[...]) * (1.0 + sc2_ref[0]) + sh2_ref[0]
    h2_ref[0] = h2
    logits = _dot3(h2, wr_ref[...]) + br_ref[...]
    tm = logits.shape[0]
    colf = lax.broadcasted_iota(jnp.int32, (tm, N_EXPERTS), 1).astype(F32)
    _, vals, cols = _top_select(logits, colf, TOP_K)
    ex = [jnp.exp(v - vals[0]) for v in vals]
    tot = ex[0] + ex[1] + ex[2] + ex[3]
    lane = lax.broadcasted_iota(jnp.int32, (tm, LANES), 1)
    ti = jnp.zeros((tm, LANES), F32)
    tw = jnp.zeros((tm, LANES), F32)
    for r in range(TOP_K):
        ti = jnp.where(lane == r, cols[r], ti)
        tw = jnp.where(lane == r, ex[r] / tot, tw)
    ti_ref[0] = ti.astype(jnp.int32)
    tw_ref[0] = tw


def _merge_router(x, a, og, proj, ga1, sc2, sh2, g2, wa, wb, wo, wr, br, tm):
    G, T, _ = x.shape
    R = ga1.shape[1]
    rm = 1 if R == 1 else tm
    mod_map = (lambda b, i: (b, 0, 0)) if R == 1 else (lambda b, i: (b, i, 0))
    tok = lambda w: pl.BlockSpec((1, tm, w), lambda b, i: (b, i, 0))
    full = lambda r, c: pl.BlockSpec((r, c), lambda b, i: (0, 0))
    return pl.pallas_call(
        _merge_kernel,
        grid=(G, T // tm),
        in_specs=[
            tok(D_MODEL), tok(A_QW), tok(G_VAL_DIM),
            pl.BlockSpec((1, tm, D_MODEL), lambda b, i: (b, i, COL_ZA // D_MODEL)),
            pl.BlockSpec((1, tm, D_MODEL), lambda b, i: (b, i, COL_ZB // D_MODEL)),
            pl.BlockSpec((1, rm, D_MODEL), mod_map),
            pl.BlockSpec((1, rm, D_MODEL), mod_map),
            pl.BlockSpec((1, rm, D_MODEL), mod_map),
            full(1, D_MODEL), full(A_QW, D_MODEL), full(G_VAL_DIM, D_MODEL), full(D_MODEL, D_MODEL),
            full(D_MODEL, N_EXPERTS), full(1, N_EXPERTS),
        ],
        out_specs=[tok(D_MODEL), tok(D_MODEL), tok(LANES), tok(LANES)],
        out_shape=[
            jax.ShapeDtypeStruct((G, T, D_MODEL), F32),
            jax.ShapeDtypeStruct((G, T, D_MODEL), F32),
            jax.ShapeDtypeStruct((G, T, LANES), jnp.int32),
            jax.ShapeDtypeStruct((G, T, LANES), F32),
        ],
        compiler_params=_params(("arbitrary", "arbitrary")),
        name="merge_router",
    )(x, a, og, proj, proj, ga1, sc2, sh2, g2.reshape(1, -1), wa, wb, wo, wr, br.reshape(1, -1))


def _moe_kernel(be_ref, rt_ref, nu_ref, h_ref, rw_ref, wgu_ref, bgu_ref, wdn_ref, bdn_ref, y_ref,
                xbuf, wgu_bf, wdn_bf, sem):
    i = pl.program_id(0)
    n_used = nu_ref[0]

    def start_gather(blk, slot):
        def body(r, carry):
            tok = rt_ref[blk * MOE_BLOCK + r]
            pltpu.make_async_copy(h_ref.at[pl.ds(tok, 1), :], xbuf.at[slot, pl.ds(r, 1), :], sem.at[slot]).start()
            return carry

        lax.fori_loop(0, MOE_BLOCK, body, 0)

    @pl.when(i == 0)
    def _():
        start_gather(0, 0)

    slot = i % 2

    @pl.when(i + 1 < n_used)
    def _():
        start_gather(i + 1, 1 - slot)

    @pl.when(i < n_used)
    def _():
        pltpu.make_async_copy(h_ref.at[pl.ds(0, MOE_BLOCK), :], xbuf.at[slot], sem.at[slot]).wait()

        @pl.when((i == 0) | (be_ref[i] != be_ref[jnp.maximum(i - 1, 0)]))
        def _():
            wgu_bf[...] = wgu_ref[0].astype(BF16)
            wdn_bf[...] = wdn_ref[0].astype(BF16)

        x = xbuf[slot].astype(BF16)
        gu = jnp.dot(x, wgu_bf[...], preferred_element_type=F32) + bgu_ref[0]
        g = jnp.minimum(gu[:, :D_FF], SWIGLU_LIMIT)
        u = jnp.clip(gu[:, D_FF:], -SWIGLU_LIMIT, SWIGLU_LIMIT)
        act = ((u + 1.0) * g * _sigmoid(SWIGLU_ALPHA * g)).astype(BF16)
        y = jnp.dot(act, wdn_bf[...], preferred_element_type=F32) + bdn_ref[0]
        y_ref[...] = y * rw_ref[...]

    @pl.when(i >= n_used)
    def _():
        y_ref[...] = jnp.zeros_like(y_ref)


def _moe_experts(h2, row_tok, row_w, blk_e, n_used, w_gu, b_gu, w_down, b_down):
    n_blocks = blk_e.shape[0]
    rows = n_blocks * MOE_BLOCK
    return pl.pallas_call(
        _moe_kernel,
        grid_spec=pltpu.PrefetchScalarGridSpec(
            num_scalar_prefetch=3,
            grid=(n_blocks,),
            in_specs=[
                pl.BlockSpec(memory_space=pl.ANY),
                pl.BlockSpec((MOE_BLOCK, 1), lambda i, be, rt, nu: (i, 0)),
                pl.BlockSpec((1, D_MODEL, 2 * D_FF), lambda i, be, rt, nu: (be[i], 0, 0)),
                pl.BlockSpec((1, 1, 2 * D_FF), lambda i, be, rt, nu: (be[i], 0, 0)),
                pl.BlockSpec((1, D_FF, D_MODEL), lambda i, be, rt, nu: (be[i], 0, 0)),
                pl.BlockSpec((1, 1, D_MODEL), lambda i, be, rt, nu: (be[i], 0, 0)),
            ],
            out_specs=pl.BlockSpec((MOE_BLOCK, D_MODEL), lambda i, be, rt, nu: (i, 0)),
            scratch_shapes=[
                pltpu.VMEM((2, MOE_BLOCK, D_MODEL), F32),
                pltpu.VMEM((D_MODEL, 2 * D_FF), BF16),
                pltpu.VMEM((D_FF, D_MODEL), BF16),
                pltpu.SemaphoreType.DMA((2,)),
            ],
        ),
        out_shape=jax.ShapeDtypeStruct((rows, D_MODEL), F32),
        compiler_params=_params(("arbitrary",)),
        name="moe_experts",
    )(blk_e, row_tok, n_used, h2, row_w.reshape(rows, 1), w_gu, b_gu.reshape(N_EXPERTS, 1, -1), w_down,
      b_down.reshape(N_EXPERTS, 1, -1))


COMB_T = 128


def _combine_kernel(dest_ref, yb_ref, x1_ref, ga2_ref, gf_ref, y_ref, gbuf, sem, *, tok_off):
    i = pl.program_id(1)
    b = pl.program_id(0)
    n_i = pl.num_programs(1)
    step = b * n_i + i
    n_steps = pl.num_programs(0) * n_i
    tm = gbuf.shape[2]

    def start_gather(stp, slot):
        base = (tok_off + stp * tm) * TOP_K

        def body(r, carry):
            for k in range(TOP_K):
                row = dest_ref[base + r * TOP_K + k]
                pltpu.make_async_copy(yb_ref.at[pl.ds(row, 1), :], gbuf.at[slot, k, pl.ds(r, 1), :],
                                      sem.at[slot]).start()
            return carry

        lax.fori_loop(0, tm, body, 0)

    @pl.when(step == 0)
    def _():
        start_gather(0, 0)

    slot = step % 2

    @pl.when(step + 1 < n_steps)
    def _():
        start_gather(step + 1, 1 - slot)

    for k in range(TOP_K):
        pltpu.make_async_copy(yb_ref.at[pl.ds(0, tm), :], gbuf.at[slot, k], sem.at[slot]).wait()
    moe = gbuf[slot, 0] + gbuf[slot, 1] + gbuf[slot, 2] + gbuf[slot, 3]
    y_ref[0] = _rmsnorm(x1_ref[0] + ga2_ref[0] * moe, gf_ref[...])


def _combine(yb, dest, x1, ga2, g_final, tok_off):
    G, T, _ = x1.shape
    R = ga2.shape[1]
    tm = min(COMB_T, T)
    rm = 1 if R == 1 else tm
    mod_map = (lambda b, i, d: (b, 0, 0)) if R == 1 else (lambda b, i, d: (b, i, 0))
    return pl.pallas_call(
        functools.partial(_combine_kernel, tok_off=tok_off),
        grid_spec=pltpu.PrefetchScalarGridSpec(
            num_scalar_prefetch=1,
            grid=(G, T // tm),
            in_specs=[
                pl.BlockSpec(memory_space=pl.ANY),
                pl.BlockSpec((1, tm, D_MODEL), lambda b, i, d: (b, i, 0)),
                pl.BlockSpec((1, rm, D_MODEL), mod_map),
                pl.BlockSpec((1, D_MODEL), lambda b, i, d: (0, 0)),
            ],
            out_specs=pl.BlockSpec((1, tm, D_MODEL), lambda b, i, d: (b, i, 0)),
            scratch_shapes=[
                pltpu.VMEM((2, TOP_K, tm, D_MODEL), F32),
                pltpu.SemaphoreType.DMA((2,)),
            ],
        ),
        out_shape=jax.ShapeDtypeStruct((G, T, D_MODEL), F32),
        compiler_params=_params(("arbitrary", "arbitrary")),
        name="combine",
    )(dest, yb, x1, ga2, g_final.reshape(1, -1))


def _routing(topi, topw):
    n = topi.shape[0]
    na = n * TOP_K
    flat_e = topi.reshape(-1)
    order = jnp.argsort(flat_e)
    se = flat_e[order]
    counts = jnp.bincount(flat_e, length=N_EXPERTS)
    padded = (counts + MOE_BLOCK - 1) // MOE_BLOCK * MOE_BLOCK
    pad_end = jnp.cumsum(padded)
    pad_start = pad_end - padded
    start = jnp.cumsum(counts) - counts
    dest_sorted = (pad_start[se] + jnp.arange(na) - start[se]).astype(jnp.int32)
    n_blocks = -(-na // MOE_BLOCK) + N_EXPERTS
    rows = n_blocks * MOE_BLOCK
    flat_tok = (jnp.arange(na, dtype=jnp.int32) // TOP_K)[order]
    row_tok = jnp.zeros((rows,), jnp.int32).at[dest_sorted].set(flat_tok)
    row_w = jnp.zeros((rows,), F32).at[dest_sorted].set(topw.reshape(-1)[order])
    blk_e = jnp.minimum(jnp.searchsorted(pad_end, jnp.arange(n_blocks) * MOE_BLOCK, side="right"),
                        N_EXPERTS - 1).astype(jnp.int32)
    n_used = (pad_end[-1] // MOE_BLOCK).astype(jnp.int32).reshape(1)
    dest = jnp.zeros((na,), jnp.int32).at[order].set(dest_sorted)
    return row_tok, row_w, blk_e, n_used, dest


def _mods(mod):
    return [m[:, None, :] if m.ndim == 2 else m for m in jnp.split(mod, 6, axis=-1)]


def _layer(xp, xs, cp, cs, cache_k, cache_v, state, page_table, w_ada, b_ada, g_norm_mix, w_in, w_gk2, b_gk2,
           g_gla_norm, w_br_a, w_br_b, w_out, g_norm_ffn, w_router, b_router, w_gu, b_gu, w_down, b_down,
           g_final):
    B, T, _ = xp.shape
    S = xs.shape[0]
    c_all = jnp.concatenate([cp, cs], axis=0)
    pad = -c_all.shape[0] % 8
    mod = _ada_mod(jnp.pad(c_all, ((0, pad), (0, 0))), w_ada, b_ada)
    sh1p, sc1p, ga1p, sh2p, sc2p, ga2p = _mods(mod[:B])
    sh1s, sc1s, ga1s, sh2s, sc2s, ga2s = [m[None] for m in jnp.split(mod[B:B + S], 6, axis=-1)]

    w_perm = jnp.concatenate(
        [w_in[:, :COL_ZA], w_in[:, COL_ZA + G_LOWRANK:], w_in[:, COL_ZA:COL_ZA + G_LOWRANK],
         jnp.zeros((D_MODEL, LANES - G_LOWRANK), F32)], axis=1).astype(BF16)
    wa, wb, wo = w_br_a.astype(BF16), w_br_b.astype(BF16), w_out.astype(BF16)

    xs3 = xs.reshape(1, S, D_MODEL)
    proj_p = _inproj(xp, sc1p, sh1p, g_norm_mix, w_perm, tm=1024)
    proj_s = _inproj(xs3, sc1s, sh1s, g_norm_mix, w_perm, tm=S)

    a_p = _moba_prompt(proj_p)
    og_p, s_p = _gla_prompt(proj_p, w_gk2, b_gk2, g_gla_norm)
    ps = proj_s[0]
    k_s = ps[:, COL_AK:COL_AK + A_KVW]
    v_s = ps[:, COL_AV:COL_AV + A_KVW]
    a_s = _moba_sample(ps[:, :A_QW], k_s, v_s, cache_k.reshape(-1, PAGE_SIZE, A_KVW),
                       cache_v.reshape(-1, PAGE_SIZE, A_KVW), page_table)
    og_s, s_s = _gla_sample(proj_s.reshape(S, 1, PROJ_W), state, w_gk2, b_gk2, g_gla_norm)

    x1p, h2p, tip, twp = _merge_router(xp, a_p, og_p, proj_p, ga1p, sc2p, sh2p, g_norm_ffn, wa, wb, wo,
                                       w_router, b_router, tm=512)
    x1s, h2s, tis, tws = _merge_router(xs3, a_s.astype(BF16)[None], og_s.reshape(1, S, -1).astype(BF16), proj_s,
                                       ga1s, sc2s, sh2s, g_norm_ffn, wa, wb, wo, w_router, b_router, tm=S)

    n_p = B * T
    h2 = jnp.concatenate([h2p.reshape(n_p, D_MODEL), h2s.reshape(S, D_MODEL)], axis=0)
    topi = jnp.concatenate([tip.reshape(n_p, LANES)[:, :TOP_K], tis.reshape(S, LANES)[:, :TOP_K]], axis=0)
    topw = jnp.concatenate([twp.reshape(n_p, LANES)[:, :TOP_K], tws.reshape(S, LANES)[:, :TOP_K]], axis=0)
    row_tok, row_w, blk_e, n_used, dest = _routing(topi, topw)
    yb = _moe_experts(h2, row_tok, row_w, blk_e, n_used, w_gu, b_gu, w_down, b_down)
    y_p = _combine(yb, dest, x1p, ga2p, g_final, tok_off=0)
    y_s = _combine(yb, dest, x1s, ga2s, g_final, tok_off=n_p)

    kp = proj_p[:, :, COL_AK:COL_AK + A_KVW].reshape(B, T, A_KV_HEADS, A_HEAD_DIM)
    vp = proj_p[:, :, COL_AV:COL_AV + A_KVW].reshape(B, T, A_KV_HEADS, A_HEAD_DIM)
    ks = k_s.reshape(S, 1, A_KV_HEADS, A_HEAD_DIM)
    vs = v_s.reshape(S, 1, A_KV_HEADS, A_HEAD_DIM)
    return y_p, y_s.reshape(S, 1, D_MODEL), kp, vp, s_p, ks, vs, s_s


def kernel(x_prompt, x_sample, c_prompt, c_sample, cache_k, cache_v, state_gla, page_table, w_ada, b_ada, g_norm_mix, w_in, w_gk2, b_gk2, g_gla_norm, w_br_a, w_br_b, w_out, g_norm_ffn, w_router, b_router, w_gu, b_gu, w_down, b_down, g_final):
    assert w_ada.shape[0] == 1, "single-layer step"
    outs = _layer(x_prompt, x_sample.reshape(x_sample.shape[0], D_MODEL), c_prompt, c_sample, cache_k[0],
                  cache_v[0], state_gla[0], page_table, w_ada[0], b_ada[0], g_norm_mix[0], w_in[0], w_gk2[0],
                  b_gk2[0], g_gla_norm[0], w_br_a[0], w_br_b[0], w_out[0], g_norm_ffn[0], w_router[0],
                  b_router[0], w_gu[0], b_gu[0], w_down[0], b_down[0], g_final)
    y_p, y_s, kp, vp, s_p, ks, vs, s_s = outs
    return (y_p, y_s, kp[None], vp[None], s_p[None], ks[None], vs[None], s_s[None])
```

```python
import functools

import jax
import jax.numpy as jnp
from jax import lax
from jax.experimental import pallas as pl
from jax.experimental.pallas import tpu as pltpu

F32 = jnp.float32
BF16 = jnp.bfloat16

D_MODEL = 1024
A_HEADS = 8
A_KV_HEADS = 4
A_GROUP = A_HEADS // A_KV_HEADS
A_HEAD_DIM = 64
A_QW = A_HEADS * A_HEAD_DIM
A_KVW = A_KV_HEADS * A_HEAD_DIM
A_BLOCK = 256
A_TOPK = 3
PAGE_SIZE = 128
G_HEADS = 4
G_KEY_DIM = D_MODEL // 2
G_VAL_DIM = D_MODEL
G_DK = G_KEY_DIM // G_HEADS
G_DV = G_VAL_DIM // G_HEADS
G_LOWRANK = 16
G_NORMALIZER = 16.0
G_CHUNK = 64
N_EXPERTS = 32
TOP_K = 4
D_FF = D_MODEL
SWIGLU_LIMIT = 7.0
SWIGLU_ALPHA = 1.702
EPS = 1e-6

LANES = 128
COL_AQ = 0
COL_AK = COL_AQ + A_QW
COL_AV = COL_AK + A_KVW
COL_GQ = COL_AV + A_KVW
COL_GK = COL_GQ + G_KEY_DIM
COL_GV = COL_GK + G_KEY_DIM
COL_GG = COL_GV + G_VAL_DIM
COL_ZA = COL_GG + G_VAL_DIM
COL_ZB = COL_ZA + D_MODEL
COL_LR = COL_ZB + D_MODEL
PROJ_W = COL_LR + LANES
PROJ_TN = 896
NEG_BIG = -1e30
VMEM_LIMIT = 56 * 1024 * 1024

NT_DIMS = (((1,), (1,)), ((), ()))
NN_DIMS = (((1,), (0,)), ((), ()))


def _params(sem):
    return pltpu.CompilerParams(dimension_semantics=sem, vmem_limit_bytes=VMEM_LIMIT)


def _split(a):
    hi = a.astype(BF16)
    lo = (a - hi.astype(F32)).astype(BF16)
    return hi, lo


def _dot3(a, b, dims=NN_DIMS):
    ah, al = _split(a)
    bh, bl = _split(b)
    d = lambda x, y: lax.dot_general(x, y, dims, preferred_element_type=F32)
    return d(ah, bh) + d(al, bh) + d(ah, bl)


def _sigmoid(x):
    return 1.0 / (1.0 + jnp.exp(-x))


def _log_sigmoid(x):
    return -(jnp.maximum(-x, 0.0) + jnp.log(1.0 + jnp.exp(-jnp.abs(x))))


def _rmsnorm(x, g):
    return x * lax.rsqrt(jnp.mean(x * x, axis=-1, keepdims=True) + EPS) * g


def _top_cols(score, colf, n_pick):
    vals, cols = [], []
    big = float(score.shape[1])
    for _ in range(n_pick):
        mx = jnp.max(score, axis=1, keepdims=True)
        first = jnp.min(jnp.where(score == mx, colf, big), axis=1, keepdims=True)
        vals.append(mx)
        cols.append(first)
        score = jnp.where(colf == first, -jnp.inf, score)
    return vals, cols


def _ada_kernel(c_ref, w_ref, b_ref, o_ref):
    c = c_ref[...]
    o_ref[...] = _dot3(c * _sigmoid(c), w_ref[...]) + b_ref[...]


def _ada_mod(c, w_ada, b_ada):
    rows = c.shape[0]
    tn = 1536
    return pl.pallas_call(
        _ada_kernel,
        grid=(6 * D_MODEL // tn,),
        in_specs=[
            pl.BlockSpec((rows, D_MODEL), lambda j: (0, 0)),
            pl.BlockSpec((D_MODEL, tn), lambda j: (0, j)),
            pl.BlockSpec((1, tn), lambda j: (0, j)),
        ],
        out_specs=pl.BlockSpec((rows, tn), lambda j: (0, j)),
        out_shape=jax.ShapeDtypeStruct((rows, 6 * D_MODEL), F32),
        compiler_params=_params(("arbitrary",)),
        name="ada_mod",
    )(c, w_ada, b_ada.reshape(1, -1))


def _inproj_kernel(x_ref, sc_ref, sh_ref, g_ref, w_ref, o_ref, h_ref):
    @pl.when(pl.program_id(2) == 0)
    def _():
        y = _rmsnorm(x_ref[0], g_ref[...])
        h_ref[...] = (y * (1.0 + sc_ref[0]) + sh_ref[0]).astype(BF16)

    o_ref[0] = jnp.dot(h_ref[...], w_ref[...], preferred_element_type=F32)


def _inproj(x, sc, sh, g, w_perm, tm):
    G, T, _ = x.shape
    R = sc.shape[1]
    rm = 1 if R == 1 else tm
    mod_map = (lambda b, i, j: (b, 0, 0)) if R == 1 else (lambda b, i, j: (b, i, 0))
    return pl.pallas_call(
        _inproj_kernel,
        grid=(G, T // tm, PROJ_W // PROJ_TN),
        in_specs=[
            pl.BlockSpec((1, tm, D_MODEL), lambda b, i, j: (b, i, 0)),
            pl.BlockSpec((1, rm, D_MODEL), mod_map),
            pl.BlockSpec((1, rm, D_MODEL), mod_map),
            pl.BlockSpec((1, D_MODEL), lambda b, i, j: (0, 0)),
            pl.BlockSpec((D_MODEL, PROJ_TN), lambda b, i, j: (0, j)),
        ],
        out_specs=pl.BlockSpec((1, tm, PROJ_TN), lambda b, i, j: (b, i, j)),
        out_shape=jax.ShapeDtypeStruct((G, T, PROJ_W), F32),
        scratch_shapes=[pltpu.VMEM((tm, D_MODEL), BF16)],
        compiler_params=_params(("arbitrary", "arbitrary", "arbitrary")),
        name="inproj",
    )(x, sc, sh, g.reshape(1, -1), w_perm)


PREP_T = 2048
AUG_NB = 32
AUG_REST = LANES - A_HEAD_DIM - AUG_NB


def _moba_prep_kernel(k_ref, v_ref, ka_ref, vt_ref, km_ref):
    t = pl.program_id(1)
    k = k_ref[0]
    v = v_ref[0]
    nblk = PREP_T // A_BLOCK
    km_ref[0] = jnp.mean(k.reshape(nblk, A_BLOCK, A_KVW), axis=1)
    half = LANES - A_HEAD_DIM
    row = lax.broadcasted_iota(jnp.int32, (A_BLOCK, half), 0)
    col = lax.broadcasted_iota(jnp.int32, (A_BLOCK, half), 1)
    vextra = jnp.where(col == 0, 1.0, 0.0)
    for jb in range(nblk):
        blk = t * nblk + jb
        kextra = jnp.where(col == blk, 1.0, 0.0)
        kextra = jnp.where((col == AUG_NB) | (col == AUG_NB + 1), 1.0, kextra)
        kextra = jnp.where(col == AUG_NB + 2, row.astype(F32), kextra)
        kextra = jnp.where(col == AUG_NB + 3, (blk * A_BLOCK).astype(F32), kextra)
        rows = slice(jb * A_BLOCK, (jb + 1) * A_BLOCK)
        for h in range(A_KV_HEADS):
            cols = slice(h * A_HEAD_DIM, (h + 1) * A_HEAD_DIM)
            ka_ref[0, h, jb] = jnp.concatenate([k[rows, cols], kextra], axis=1).astype(BF16)
            vt_ref[0, h, jb] = jnp.concatenate([v[rows, cols], vextra], axis=1).T.astype(BF16)


def _moba_prep(proj):
    B, T, _ = proj.shape
    nblk = PREP_T // A_BLOCK
    nb = T // A_BLOCK
    assert nb <= AUG_NB
    return pl.pallas_call(
        _moba_prep_kernel,
        grid=(B, T // PREP_T),
        in_specs=[
            pl.BlockSpec((1, PREP_T, A_KVW), lambda b, t: (b, t, COL_AK // A_KVW)),
            pl.BlockSpec((1, PREP_T, A_KVW), lambda b, t: (b, t, COL_AV // A_KVW)),
        ],
        out_specs=[
            pl.BlockSpec((1, A_KV_HEADS, nblk, A_BLOCK, LANES), lambda b, t: (b, 0, t, 0, 0)),
            pl.BlockSpec((1, A_KV_HEADS, nblk, LANES, A_BLOCK), lambda b, t: (b, 0, t, 0, 0)),
            pl.BlockSpec((1, nblk, A_KVW), lambda b, t: (b, t, 0)),
        ],
        out_shape=[
            jax.ShapeDtypeStruct((B, A_KV_HEADS, nb, A_BLOCK, LANES), BF16),
            jax.ShapeDtypeStruct((B, A_KV_HEADS, nb, LANES, A_BLOCK), BF16),
            jax.ShapeDtypeStruct((B, nb, A_KVW), F32),
        ],
        compiler_params=_params(("arbitrary", "arbitrary")),
        name="moba_prep",
    )(proj, proj)


def _slope_of(head_plus_one, shape):
    bits = (127 - head_plus_one) << 23
    return lax.bitcast_convert_type(jnp.full(shape, bits, jnp.int32), F32)


def _top_rows(score, rowf, n_pick, n_valid):
    picked = jnp.zeros(score.shape, jnp.bool_)
    big = float(score.shape[0])
    for r in range(n_pick):
        mx = jnp.max(score, axis=0, keepdims=True)
        first = jnp.min(jnp.where(score == mx, rowf, big), axis=0, keepdims=True)
        hit = rowf == first
        picked = picked | (hit & ((jnp.zeros_like(rowf) + n_valid) > r))
        score = jnp.where(hit, -jnp.inf, score)
    return picked


def _moba_attn_kernel(q_ref, km_ref, ka_ref, vt_ref, o_ref, qa_ref, m_ref, acc_ref):
    kvh = pl.program_id(1)
    i = pl.program_id(2)
    nb = km_ref.shape[2]
    qt = q_ref[0].T
    km = km_ref[0, 0]
    blk_f = i.astype(F32)
    rowf = lax.broadcasted_iota(jnp.int32, (nb, A_BLOCK), 0).astype(F32)
    erow = lax.broadcasted_iota(jnp.int32, (AUG_REST, A_BLOCK), 0)
    qq = lax.broadcasted_iota(jnp.int32, (AUG_REST, A_BLOCK), 1).astype(F32)
    for hh in range(A_GROUP):
        qh = qt[hh * A_HEAD_DIM:(hh + 1) * A_HEAD_DIM, :]
        gate = _dot3(km, qh)
        gate = jnp.where(rowf < blk_f, gate, -jnp.inf)
        picked = _top_rows(gate, rowf, A_TOPK, blk_f) | (rowf == blk_f)
        selbias = jnp.where(picked, 0.0, NEG_BIG)
        if nb < AUG_NB:
            selbias = jnp.concatenate([selbias, jnp.zeros((AUG_NB - nb, A_BLOCK), F32)], axis=0)
        slope = _slope_of(kvh * A_GROUP + hh + 1, (AUG_REST, A_BLOCK))
        alibi = jnp.where(erow == 0, -slope * qq, 0.0)
        alibi = jnp.where(erow == 1, -slope * (blk_f * A_BLOCK), alibi)
        alibi = jnp.where((erow == 2) | (erow == 3), slope, alibi)
        qa_ref[hh] = jnp.concatenate([qh * (A_HEAD_DIM ** -0.5), selbias, alibi], axis=0).astype(BF16)

    kb = ka_ref[0, 0, i]
    vb = vt_ref[0, 0, i]
    key = lax.broadcasted_iota(jnp.int32, (A_BLOCK, A_BLOCK), 0)
    qry = lax.broadcasted_iota(jnp.int32, (A_BLOCK, A_BLOCK), 1)
    for hh in range(A_GROUP):
        s = jnp.dot(kb, qa_ref[hh], preferred_element_type=F32)
        s = jnp.where(key <= qry, s, NEG_BIG)
        m = jnp.max(s, axis=0, keepdims=True)
        p = jnp.exp(s - m)
        m_ref[hh] = m
        acc_ref[hh] = jnp.dot(vb, p.astype(BF16), preferred_element_type=F32)

    def past_block(j, carry):
        kj = ka_ref[0, 0, j]
        vj = vt_ref[0, 0, j]
        for hh in range(A_GROUP):
            s = jnp.dot(kj, qa_ref[hh], preferred_element_type=F32)
            m_old = m_ref[hh]
            m_new = jnp.maximum(m_old, jnp.max(s, axis=0, keepdims=True))
            p = jnp.exp(s - m_new)
            acc_ref[hh] = jnp.exp(m_old - m_new) * acc_ref[hh] + jnp.dot(
                vj, p.astype(BF16), preferred_element_type=F32)
            m_ref[hh] = m_new
        return carry

    lax.fori_loop(0, i, past_block, 0)
    outs = []
    for hh in range(A_GROUP):
        acc = acc_ref[hh]
        outs.append((acc[:A_HEAD_DIM] / acc[A_HEAD_DIM:A_HEAD_DIM + 1]).T)
    o_ref[0] = jnp.concatenate(outs, axis=1).astype(BF16)


def _moba_prompt(proj):
    B, T, _ = proj.shape
    nb = T // A_BLOCK
    kaug, vaug_t, km = _moba_prep(proj)
    km = km.reshape(B, nb, A_KV_HEADS, A_HEAD_DIM).transpose(0, 2, 1, 3)
    qw = A_GROUP * A_HEAD_DIM
    return pl.pallas_call(
        _moba_attn_kernel,
        grid=(B, A_KV_HEADS, nb),
        in_specs=[
            pl.BlockSpec((1, A_BLOCK, qw), lambda b, h, i: (b, i, h)),
            pl.BlockSpec((1, 1, nb, A_HEAD_DIM), lambda b, h, i: (b, h, 0, 0)),
            pl.BlockSpec((1, 1, nb, A_BLOCK, LANES), lambda b, h, i: (b, h, 0, 0, 0)),
            pl.BlockSpec((1, 1, nb, LANES, A_BLOCK), lambda b, h, i: (b, h, 0, 0, 0)),
        ],
        out_specs=pl.BlockSpec((1, A_BLOCK, qw), lambda b, h, i: (b, i, h)),
        out_shape=jax.ShapeDtypeStruct((B, T, A_QW), BF16),
        scratch_shapes=[
            pltpu.VMEM((A_GROUP, LANES, A_BLOCK), BF16),
            pltpu.VMEM((A_GROUP, 1, A_BLOCK), F32),
            pltpu.VMEM((A_GROUP, LANES, A_BLOCK), F32),
        ],
        compiler_params=_params(("arbitrary", "arbitrary", "arbitrary")),
        name="moba_attn",
    )(proj, km, kaug, vaug_t)


GLA_T = 1024


def _gla_kernel(q_ref, k_ref, v_ref, gg_ref, lr_ref, w2_ref, b2_ref, gn_ref, og_ref, st_ref, s_ref, g_ref):
    t = pl.program_id(2)

    @pl.when(t == 0)
    def _():
        s_ref[...] = jnp.zeros_like(s_ref)

    z = _dot3(lr_ref[0][:, :G_LOWRANK], w2_ref[...]) + b2_ref[...]
    g_ref[...] = _log_sigmoid(z) * (1.0 / G_NORMALIZER)
    C = G_CHUNK
    rr = lax.broadcasted_iota(jnp.int32, (C, C), 0)
    cc = lax.broadcasted_iota(jnp.int32, (C, C), 1)
    causal = cc <= rr
    tril = jnp.where(causal, 1.0, 0.0).astype(BF16)
    er = lax.broadcasted_iota(jnp.int32, (G_DK, G_DK), 0)
    ec = lax.broadcasted_iota(jnp.int32, (G_DK, G_DK), 1)
    eye = er == ec
    gn = gn_ref[...]

    def chunk(c, carry):
        sl = pl.ds(pl.multiple_of(c * C, C), C)
        gc = g_ref[sl, :]
        gh, gl = _split(gc)
        cum = jnp.dot(tril, gh, preferred_element_type=F32) + jnp.dot(tril, gl, preferred_element_type=F32)
        last = cum[C - 1:C, :]
        kc = k_ref[0, sl, :]
        vc = v_ref[0, sl, :].astype(BF16)
        qe = (q_ref[0, sl, :] * (G_DK ** -0.5) * jnp.exp(cum)).astype(BF16)
        ke = (kc * jnp.exp(-cum)).astype(BF16)
        kd = kc * jnp.exp(last - cum)
        att = lax.dot_general(qe, ke, NT_DIMS, preferred_element_type=F32)
        att = jnp.where(causal, att, 0.0).astype(BF16)
        s = s_ref[...]
        o = jnp.dot(qe, s.astype(BF16), preferred_element_type=F32) + jnp.dot(att, vc, preferred_element_type=F32)
        decay = jnp.sum(jnp.where(eye, jnp.exp(last), 0.0), axis=1, keepdims=True)
        s_ref[...] = s * decay + jnp.dot(kd.T.astype(BF16), vc, preferred_element_type=F32)
        on = _rmsnorm(o, gn)
        gate = gg_ref[0, sl, :]
        og_ref[0, sl, :] = (on * (gate * _sigmoid(gate))).astype(BF16)
        return carry

    lax.fori_loop(0, GLA_T // C, chunk, 0)

    @pl.when(t == pl.num_programs(2) - 1)
    def _():
        st_ref[0, 0] = s_ref[...]


def _gla_prompt(proj, w_gk2, b_gk2, g_gla_norm):
    B, T, _ = proj.shape
    tt = min(GLA_T, T)
    assert tt == GLA_T
    return pl.pallas_call(
        _gla_kernel,
        grid=(B, G_HEADS, T // tt),
        in_specs=[
            pl.BlockSpec((1, tt, G_DK), lambda b, h, t: (b, t, COL_GQ // G_DK + h)),
            pl.BlockSpec((1, tt, G_DK), lambda b, h, t: (b, t, COL_GK // G_DK + h)),
            pl.BlockSpec((1, tt, G_DV), lambda b, h, t: (b, t, COL_GV // G_DV + h)),
            pl.BlockSpec((1, tt, G_DV), lambda b, h, t: (b, t, COL_GG // G_DV + h)),
            pl.BlockSpec((1, tt, LANES), lambda b, h, t: (b, t, COL_LR // LANES)),
            pl.BlockSpec((G_LOWRANK, G_DK), lambda b, h, t: (0, h)),
            pl.BlockSpec((1, G_DK), lambda b, h, t: (0, h)),
            pl.BlockSpec((1, G_DV), lambda b, h, t: (0, 0)),
        ],
        out_specs=[
            pl.BlockSpec((1, tt, G_DV), lambda b, h, t: (b, t, h)),
            pl.BlockSpec((1, 1, G_DK, G_DV), lambda b, h, t: (b, h, 0, 0)),
        ],
        out_shape=[
            jax.ShapeDtypeStruct((B, T, G_VAL_DIM), BF16),
            jax.ShapeDtypeStruct((B, G_HEADS, G_DK, G_DV), F32),
        ],
        scratch_shapes=[pltpu.VMEM((G_DK, G_DV), F32), pltpu.VMEM((tt, G_DK), F32)],
        compiler_params=_params(("arbitrary", "arbitrary", "arbitrary")),
        name="gla_prompt",
    )(proj, proj, proj, proj, proj, w_gk2, b_gk2.reshape(1, -1), g_gla_norm.reshape(1, -1))


PPB = A_BLOCK // PAGE_SIZE


def _smoba_kernel(pt_ref, qb_ref, q8_ref, kn_ref, vn_ref, ck_ref, cv_ref, o_ref,
                  kbuf, vbuf, lg_ref, p_ref, pown_ref, idx_v, idx_s, ksem, vsem, isem):
    s = pl.program_id(0)
    n_seq = pl.num_programs(0) - 1
    n_pages = kbuf.shape[1]
    nb = n_pages // PPB
    past = n_pages * PAGE_SIZE
    hd = A_HEAD_DIM

    def start_keys(seq, slot):
        for p in range(n_pages):
            pltpu.make_async_copy(ck_ref.at[pt_ref[seq, p]], kbuf.at[slot, p], ksem.at[slot]).start()

    def value_copy(seq, slot, h, r, e):
        page = pt_ref[seq, idx_s[slot, h, r] * PPB + e]
        return pltpu.make_async_copy(cv_ref.at[page, pl.ds((h // A_GROUP) * hd, hd), :],
                                     vbuf.at[slot, h, r, e], vsem.at[slot])

    @pl.when(s == 0)
    def _():
        start_keys(0, 0)

    @pl.when(s + 1 < n_seq)
    def _():
        start_keys(s + 1, (s + 1) % 2)

    @pl.when(s < n_seq)
    def _():
        slot = s % 2
        pltpu.make_async_copy(ck_ref.at[pl.ds(0, n_pages)], kbuf.at[slot], ksem.at[slot]).wait()

        def page_logits(p, carry):
            kp = kbuf[slot, p]
            rows = []
            for h in range(A_HEADS):
                g = h // A_GROUP
                rows.append(jnp.sum(qb_ref[0, h] * kp[g * hd:(g + 1) * hd, :], axis=0, keepdims=True))
            lg_ref[slot, p] = jnp.concatenate(rows, axis=0)
            return carry

        lax.fori_loop(0, n_pages, page_logits, 0)
        lg = lg_ref[slot]
        gate = jnp.sum(jnp.sum(lg.reshape(nb, PPB, A_HEADS, PAGE_SIZE), axis=1), axis=2, keepdims=True)
        bidx = lax.broadcasted_iota(jnp.int32, (nb, A_HEADS, 1), 0).astype(F32)
        picked = jnp.zeros((nb, A_HEADS, 1), F32)
        firsts = []
        for r in range(A_TOPK):
            mx = jnp.max(gate, axis=0, keepdims=True)
            first = jnp.min(jnp.where(gate == mx, bidx, float(nb)), axis=0, keepdims=True)
            hit = bidx == first
            picked = jnp.where(hit, 1.0, picked)
            gate = jnp.where(hit, -jnp.inf, gate)
            firsts.append(first.reshape(A_HEADS, 1))
        sel = jnp.broadcast_to(picked[:, None], (nb, PPB, A_HEADS, 1)).reshape(n_pages, A_HEADS, 1)
        head1 = lax.broadcasted_iota(jnp.int32, (1, A_HEADS, 1), 1) + 1
        slope = lax.bitcast_convert_type((127 - head1) << 23, F32)
        tpos = (lax.broadcasted_iota(jnp.int32, (n_pages, 1, PAGE_SIZE), 0) * PAGE_SIZE
                + lax.broadcasted_iota(jnp.int32, (n_pages, 1, PAGE_SIZE), 2))
        dist = (past - tpos).astype(F32)
        att = jnp.where(sel > 0.5, lg * (hd ** -0.5) - slope * dist, NEG_BIG)
        own = jnp.sum(q8_ref[0] * kn_ref[0], axis=1, keepdims=True) * (hd ** -0.5)
        m = jnp.maximum(jnp.max(jnp.max(att, axis=0), axis=1, keepdims=True), own)
        p = jnp.exp(att - m.reshape(1, A_HEADS, 1))
        p_own = jnp.exp(own - m)
        inv = 1.0 / (jnp.sum(jnp.sum(p, axis=0), axis=1, keepdims=True) + p_own)
        p_ref[slot] = p * inv.reshape(1, A_HEADS, 1)
        pown_ref[slot] = jnp.broadcast_to(p_own * inv, (A_HEADS, LANES))
        lane = lax.broadcasted_iota(jnp.int32, (A_HEADS, LANES), 1)
        it = jnp.zeros((A_HEADS, LANES), F32)
        for r in range(A_TOPK):
            it = jnp.where(lane == r, firsts[r], it)
        idx_v[...] = it.astype(jnp.int32)
        to_smem = pltpu.make_async_copy(idx_v, idx_s.at[slot], isem)
        to_smem.start()
        to_smem.wait()
        for h in range(A_HEADS):
            for r in range(A_TOPK):
                for e in range(PPB):
                    value_copy(s, slot, h, r, e).start()

    @pl.when(s >= 1)
    def _():
        sb = s - 1
        b = sb % 2
        for h in range(A_HEADS):
            for r in range(A_TOPK):
                for e in range(PPB):
                    value_copy(sb, b, h, r, e).wait()
        rowid = lax.broadcasted_iota(jnp.int32, (A_HEADS, hd), 0)
        out = pown_ref[b][:, :hd] * vn_ref[0]
        for h in range(A_HEADS):
            part = jnp.zeros((A_HEADS, hd), F32)
            for r in range(A_TOPK):
                for e in range(PPB):
                    pg = idx_s[b, h, r] * PPB + e
                    part = part + lax.dot_general(p_ref[b, pg].astype(BF16), vbuf[b, h, r, e].astype(BF16),
                                                  NT_DIMS, preferred_element_type=F32)
            out = out + jnp.where(rowid == h, part, 0.0)
        o_ref[0] = out


def _moba_sample(q, k_new, v_new, cache_k, cache_v, page_table):
    S = q.shape[0]
    n_pages = page_table.shape[1]
    n_pool = cache_k.shape[0]
    hd = A_HEAD_DIM
    ck = cache_k.transpose(0, 2, 3, 1).reshape(n_pool, A_KVW, PAGE_SIZE)
    cv = cache_v.transpose(0, 2, 3, 1).reshape(n_pool, A_KVW, PAGE_SIZE)
    q8 = q.reshape(S, A_HEADS, hd)
    qb = jnp.broadcast_to(q8[..., None], (S, A_HEADS, hd, LANES))
    kn8 = jnp.repeat(k_new.reshape(S, A_KV_HEADS, hd), A_GROUP, axis=1)
    vn8 = jnp.repeat(v_new.reshape(S, A_KV_HEADS, hd), A_GROUP, axis=1)
    seq = lambda s, pt: (jnp.minimum(s, S - 1), 0, 0)
    out = pl.pallas_call(
        _smoba_kernel,
        grid_spec=pltpu.PrefetchScalarGridSpec(
            num_scalar_prefetch=1,
            grid=(S + 1,),
            in_specs=[
                pl.BlockSpec((1, A_HEADS, hd, LANES), lambda s, pt: (jnp.minimum(s, S - 1), 0, 0, 0)),
                pl.BlockSpec((1, A_HEADS, hd), seq),
                pl.BlockSpec((1, A_HEADS, hd), seq),
                pl.BlockSpec((1, A_HEADS, hd), lambda s, pt: (jnp.maximum(s - 1, 0), 0, 0)),
                pl.BlockSpec(memory_space=pl.ANY),
                pl.BlockSpec(memory_space=pl.ANY),
            ],
            out_specs=pl.BlockSpec((1, A_HEADS, hd), lambda s, pt: (jnp.maximum(s - 1, 0), 0, 0)),
            scratch_shapes=[
                pltpu.VMEM((2, n_pages, A_KVW, PAGE_SIZE), F32),
                pltpu.VMEM((2, A_HEADS, A_TOPK, PPB, hd, PAGE_SIZE), F32),
                pltpu.VMEM((2, n_pages, A_HEADS, PAGE_SIZE), F32),
                pltpu.VMEM((2, n_pages, A_HEADS, PAGE_SIZE), F32),
                pltpu.VMEM((2, A_HEADS, LANES), F32),
                pltpu.VMEM((A_HEADS, LANES), jnp.int32),
                pltpu.SMEM((2, A_HEADS, LANES), jnp.int32),
                pltpu.SemaphoreType.DMA((2,)),
                pltpu.SemaphoreType.DMA((2,)),
                pltpu.SemaphoreType.DMA(()),
            ],
        ),
        out_shape=jax.ShapeDtypeStruct((S, A_HEADS, hd), F32),
        compiler_params=_params(("arbitrary",)),
        name="moba_sample",
    )(page_table, qb, q8, kn8, vn8, ck, cv)
    return out.reshape(S, A_QW)


def _sgla_kernel(p_ref, s0_ref, w2_ref, b2_ref, gn_ref, og_ref, s1_ref):
    row = p_ref[0]
    er = lax.broadcasted_iota(jnp.int32, (G_DK, G_DK), 0)
    ec = lax.broadcasted_iota(jnp.int32, (G_DK, G_DK), 1)
    eye = er == ec
    col_of = lambda r: jnp.sum(jnp.where(eye, r, 0.0), axis=1, keepdims=True)
    lr = row[:, COL_LR:COL_LR + G_LOWRANK]
    z = _dot3(lr, w2_ref[...]) + b2_ref[...]
    g_all = _log_sigmoid(z) * (1.0 / G_NORMALIZER)
    outs = []
    for h in range(G_HEADS):
        g = g_all[:, h * G_DK:(h + 1) * G_DK]
        q = row[:, COL_GQ + h * G_DK:COL_GQ + (h + 1) * G_DK] * (G_DK ** -0.5)
        k = row[:, COL_GK + h * G_DK:COL_GK + (h + 1) * G_DK]
        v = row[:, COL_GV + h * G_DV:COL_GV + (h + 1) * G_DV]
        gate = row[:, COL_GG + h * G_DV:COL_GG + (h + 1) * G_DV]
        qe = q * jnp.exp(g)
        ke = k * jnp.exp(-g)
        att = jnp.sum(qe * ke, axis=1, keepdims=True)
        s0 = s0_ref[0, h]
        o = _dot3(qe, s0) + att * v
        s1_ref[0, h] = s0 * col_of(jnp.exp(g)) + col_of(k) * v
        on = _rmsnorm(o, gn_ref[...])
        outs.append(on * (gate * _sigmoid(gate)))
    og_ref[0] = jnp.concatenate(outs, axis=1)


def _gla_sample(proj_rows, state, w_gk2, b_gk2, g_gla_norm):
    S = proj_rows.shape[0]
    return pl.pallas_call(
        _sgla_kernel,
        grid=(S,),
        in_specs=[
            pl.BlockSpec((1, 1, PROJ_W), lambda s: (s, 0, 0)),
            pl.BlockSpec((1, G_HEADS, G_DK, G_DV), lambda s: (s, 0, 0, 0)),
            pl.BlockSpec((G_LOWRANK, G_KEY_DIM), lambda s: (0, 0)),
            pl.BlockSpec((1, G_KEY_DIM), lambda s: (0, 0)),
            pl.BlockSpec((1, G_DV), lambda s: (0, 0)),
        ],
        out_specs=[
            pl.BlockSpec((1, 1, G_VAL_DIM), lambda s: (s, 0, 0)),
            pl.BlockSpec((1, G_HEADS, G_DK, G_DV), lambda s: (s, 0, 0, 0)),
        ],
        out_shape=[
            jax.ShapeDtypeStruct((S, 1, G_VAL_DIM), F32),
            jax.ShapeDtypeStruct((S, G_HEADS, G_DK, G_DV), F32),
        ],
        compiler_params=_params(("arbitrary",)),
        name="gla_sample",
    )(proj_rows, state, w_gk2, b_gk2.reshape(1, -1), g_gla_norm.reshape(1, -1))


ROUTE_E = 0
ROUTE_W = TOP_K
ROUTE_R = 2 * TOP_K


def _merge_kernel(x_ref, a_ref, og_ref, za_ref, zb_ref, ga1_ref, sc2_ref, sh2_ref, g2_ref, wa_ref, wb_ref,
                  wo_ref, wr_ref, br_ref, c0_ref, x1_ref, h2_ref, rt_ref, cnt_ref, run_ref):
    first = (pl.program_id(0) == 0) & (pl.program_id(1) == 0)

    @pl.when(first)
    def _():
        run_ref[...] = c0_ref[...]

    pa = jnp.dot(a_ref[0], wa_ref[...], preferred_element_type=F32)
    pb = jnp.dot(og_ref[0], wb_ref[...], preferred_element_type=F32)
    merged = _sigmoid(za_ref[0]) * pa + _sigmoid(zb_ref[0]) * pb
    mix = jnp.dot(merged.astype(BF16), wo_ref[...], preferred_element_type=F32)
    x1 = x_ref[0] + ga1_ref[0] * mix
    x1_ref[0] = x1
    h2 = _rmsnorm(x1, g2_ref[...]) * (1.0 + sc2_ref[0]) + sh2_ref[0]
    h2_ref[0] = h2
    logits = _dot3(h2, wr_ref[...]) + br_ref[...]
    tm = logits.shape[0]
    colf = lax.broadcasted_iota(jnp.int32, (tm, N_EXPERTS), 1).astype(F32)
    vals, cols = _top_cols(logits, colf, TOP_K)
    ex = [jnp.exp(v - vals[0]) for v in vals]
    tot = ex[0] + ex[1] + ex[2] + ex[3]
    onehot = [jnp.where(colf == c, 1.0, 0.0) for c in cols]
    osum = onehot[0] + onehot[1] + onehot[2] + onehot[3]
    tr = lax.broadcasted_iota(jnp.int32, (tm, tm), 0)
    tc = lax.broadcasted_iota(jnp.int32, (tm, tm), 1)
    lower = jnp.where(tc < tr, 1.0, 0.0).astype(BF16)
    before = jnp.dot(lower, osum.astype(BF16), preferred_element_type=F32) + run_ref[:, :N_EXPERTS]
    lane = lax.broadcasted_iota(jnp.int32, (tm, LANES), 1)
    rec = jnp.zeros((tm, LANES), F32)
    for r in range(TOP_K):
        rank = jnp.sum(onehot[r] * before, axis=1, keepdims=True)
        rec = jnp.where(lane == ROUTE_E + r, cols[r], rec)
        rec = jnp.where(lane == ROUTE_W + r, ex[r] / tot, rec)
        rec = jnp.where(lane == ROUTE_R + r, rank, rec)
    rt_ref[0] = rec
    pad = jnp.zeros((1, LANES - N_EXPERTS), F32)
    run_ref[...] = run_ref[...] + jnp.concatenate([jnp.sum(osum, axis=0, keepdims=True), pad], axis=1)
    cnt_ref[...] = run_ref[...]


def _merge_router(x, a, og, proj, ga1, sc2, sh2, g2, wa, wb, wo, wr, br, cnt0, tm):
    G, T, _ = x.shape
    R = ga1.shape[1]
    rm = 1 if R == 1 else tm
    mod_map = (lambda b, i: (b, 0, 0)) if R == 1 else (lambda b, i: (b, i, 0))
    tok = lambda w: pl.BlockSpec((1, tm, w), lambda b, i: (b, i, 0))
    full = lambda r, c: pl.BlockSpec((r, c), lambda b, i: (0, 0))
    return pl.pallas_call(
        _merge_kernel,
        grid=(G, T // tm),
        in_specs=[
            tok(D_MODEL), tok(A_QW), tok(G_VAL_DIM),
            pl.BlockSpec((1, tm, D_MODEL), lambda b, i: (b, i, COL_ZA // D_MODEL)),
            pl.BlockSpec((1, tm, D_MODEL), lambda b, i: (b, i, COL_ZB // D_MODEL)),
            pl.BlockSpec((1, rm, D_MODEL), mod_map),
            pl.BlockSpec((1, rm, D_MODEL), mod_map),
            pl.BlockSpec((1, rm, D_MODEL), mod_map),
            full(1, D_MODEL), full(A_QW, D_MODEL), full(G_VAL_DIM, D_MODEL), full(D_MODEL, D_MODEL),
            full(D_MODEL, N_EXPERTS), full(1, N_EXPERTS), full(1, LANES),
        ],
        out_specs=[tok(D_MODEL), tok(D_MODEL), tok(LANES), full(1, LANES)],
        out_shape=[
            jax.ShapeDtypeStruct((G, T, D_MODEL), F32),
            jax.ShapeDtypeStruct((G, T, D_MODEL), F32),
            jax.ShapeDtypeStruct((G, T, LANES), F32),
            jax.ShapeDtypeStruct((1, LANES), F32),
        ],
        scratch_shapes=[pltpu.VMEM((1, LANES), F32)],
        compiler_params=_params(("arbitrary", "arbitrary")),
        name="merge_router",
    )(x, a, og, proj, proj, ga1, sc2, sh2, g2.reshape(1, -1), wa, wb, wo, wr, br.reshape(1, -1), cnt0)


MOE_ROWS = 256
DISP_T = 128


def _dispatch_kernel(dest_ref, h_ref, xs_in_ref, xs_ref, sem, *, tok_off):
    del xs_in_ref
    tm = h_ref.shape[0]
    base = (tok_off + pl.program_id(0) * tm) * TOP_K

    def body(r, carry):
        for k in range(TOP_K):
            row = dest_ref[base + r * TOP_K + k]
            pltpu.make_async_copy(h_ref.at[pl.ds(r, 1), :], xs_ref.at[pl.ds(row, 1), :], sem).start()
        return carry

    lax.fori_loop(0, tm, body, 0)
    for k in range(TOP_K):
        pltpu.make_async_copy(h_ref, xs_ref.at[pl.ds(0, tm), :], sem).wait()


def _dispatch(h2, dest, xs, tok_off):
    n = h2.shape[0]
    tm = min(DISP_T, n)
    assert n % tm == 0
    return pl.pallas_call(
        functools.partial(_dispatch_kernel, tok_off=tok_off),
        grid_spec=pltpu.PrefetchScalarGridSpec(
            num_scalar_prefetch=1,
            grid=(n // tm,),
            in_specs=[
                pl.BlockSpec((tm, D_MODEL), lambda i, d: (i, 0)),
                pl.BlockSpec(memory_space=pl.ANY),
            ],
            out_specs=pl.BlockSpec(memory_space=pl.ANY),
            scratch_shapes=[pltpu.SemaphoreType.DMA(())],
        ),
        out_shape=jax.ShapeDtypeStruct(xs.shape, F32),
        input_output_aliases={2: 0},
        compiler_params=_params(("arbitrary",)),
        name="dispatch",
    )(dest, h2, xs)


def _moe_kernel(be_ref, nu_ref, x_ref, wgu_ref, bgu_ref, wdn_ref, bdn_ref, y_ref, wgu_bf, wdn_bf):
    i = pl.program_id(0)

    @pl.when(i < nu_ref[0])
    def _():
        @pl.when((i == 0) | (be_ref[i] != be_ref[jnp.maximum(i - 1, 0)]))
        def _():
            wgu_bf[...] = wgu_ref[0].astype(BF16)
            wdn_bf[...] = wdn_ref[0].astype(BF16)

        gu = jnp.dot(x_ref[...].astype(BF16), wgu_bf[...], preferred_element_type=F32) + bgu_ref[0]
        g = jnp.minimum(gu[:, :D_FF], SWIGLU_LIMIT)
        u = jnp.clip(gu[:, D_FF:], -SWIGLU_LIMIT, SWIGLU_LIMIT)
        act = ((u + 1.0) * g * _sigmoid(SWIGLU_ALPHA * g)).astype(BF16)
        y_ref[...] = jnp.dot(act, wdn_bf[...], preferred_element_type=F32) + bdn_ref[0]

    @pl.when(i >= nu_ref[0])
    def _():
        y_ref[...] = jnp.zeros_like(y_ref)


def _moe_experts(xs, blk_e, n_used, w_gu, b_gu, w_down, b_down):
    n_blocks = blk_e.shape[0]
    return pl.pallas_call(
        _moe_kernel,
        grid_spec=pltpu.PrefetchScalarGridSpec(
            num_scalar_prefetch=2,
            grid=(n_blocks,),
            in_specs=[
                pl.BlockSpec((MOE_ROWS, D_MODEL), lambda i, be, nu: (i, 0)),
                pl.BlockSpec((1, D_MODEL, 2 * D_FF), lambda i, be, nu: (be[i], 0, 0)),
                pl.BlockSpec((1, 1, 2 * D_FF), lambda i, be, nu: (be[i], 0, 0)),
                pl.BlockSpec((1, D_FF, D_MODEL), lambda i, be, nu: (be[i], 0, 0)),
                pl.BlockSpec((1, 1, D_MODEL), lambda i, be, nu: (be[i], 0, 0)),
            ],
            out_specs=pl.BlockSpec((MOE_ROWS, D_MODEL), lambda i, be, nu: (i, 0)),
            scratch_shapes=[
                pltpu.VMEM((D_MODEL, 2 * D_FF), BF16),
                pltpu.VMEM((D_FF, D_MODEL), BF16),
            ],
        ),
        out_shape=jax.ShapeDtypeStruct((n_blocks * MOE_ROWS, D_MODEL), F32),
        compiler_params=_params(("arbitrary",)),
        name="moe_experts",
    )(blk_e, n_used, xs, w_gu, b_gu.reshape(N_EXPERTS, 1, -1), w_down, b_down.reshape(N_EXPERTS, 1, -1))


COMB_T = 128


def _combine_kernel(dest_ref, yb_ref, x1_ref, rt_ref, ga2_ref, gf_ref, y_ref, gbuf, sem, *, tok_off):
    i = pl.program_id(1)
    b = pl.program_id(0)
    n_i = pl.num_programs(1)
    step = b * n_i + i
    n_steps = pl.num_programs(0) * n_i
    tm = gbuf.shape[2]

    def start_gather(stp, slot):
        base = (tok_off + stp * tm) * TOP_K

        def body(r, carry):
            for k in range(TOP_K):
                row = dest_ref[base + r * TOP_K + k]
                pltpu.make_async_copy(yb_ref.at[pl.ds(row, 1), :], gbuf.at[slot, k, pl.ds(r, 1), :],
                                      sem.at[slot]).start()
            return carry

        lax.fori_loop(0, tm, body, 0)

    @pl.when(step == 0)
    def _():
        start_gather(0, 0)

    slot = step % 2

    @pl.when(step + 1 < n_steps)
    def _():
        start_gather(step + 1, 1 - slot)

    for k in range(TOP_K):
        pltpu.make_async_copy(yb_ref.at[pl.ds(0, tm), :], gbuf.at[slot, k], sem.at[slot]).wait()
    rec = rt_ref[0]
    moe = gbuf[slot, 0] * rec[:, ROUTE_W:ROUTE_W + 1]
    for k in range(1, TOP_K):
        moe = moe + gbuf[slot, k] * rec[:, ROUTE_W + k:ROUTE_W + k + 1]
    y_ref[0] = _rmsnorm(x1_ref[0] + ga2_ref[0] * moe, gf_ref[...])


def _combine(yb, dest, x1, route, ga2, g_final, tok_off):
    G, T, _ = x1.shape
    R = ga2.shape[1]
    tm = min(COMB_T, T)
    rm = 1 if R == 1 else tm
    mod_map = (lambda b, i, d: (b, 0, 0)) if R == 1 else (lambda b, i, d: (b, i, 0))
    return pl.pallas_call(
        functools.partial(_combine_kernel, tok_off=tok_off),
        grid_spec=pltpu.PrefetchScalarGridSpec(
            num_scalar_prefetch=1,
            grid=(G, T // tm),
            in_specs=[
                pl.BlockSpec(memory_space=pl.ANY),
                pl.BlockSpec((1, tm, D_MODEL), lambda b, i, d: (b, i, 0)),
                pl.BlockSpec((1, tm, LANES), lambda b, i, d: (b, i, 0)),
                pl.BlockSpec((1, rm, D_MODEL), mod_map),
                pl.BlockSpec((1, D_MODEL), lambda b, i, d: (0, 0)),
            ],
            out_specs=pl.BlockSpec((1, tm, D_MODEL), lambda b, i, d: (b, i, 0)),
            scratch_shapes=[
                pltpu.VMEM((2, TOP_K, tm, D_MODEL), F32),
                pltpu.SemaphoreType.DMA((2,)),
            ],
        ),
        out_shape=jax.ShapeDtypeStruct((G, T, D_MODEL), F32),
        compiler_params=_params(("arbitrary", "arbitrary")),
        name="combine",
    )(dest, yb, x1, route, ga2, g_final.reshape(1, -1))


def _routing(route, counts):
    n = route.shape[0]
    na = n * TOP_K
    topi = route[:, ROUTE_E:ROUTE_E + TOP_K].astype(jnp.int32)
    rank = route[:, ROUTE_R:ROUTE_R + TOP_K].astype(jnp.int32)
    cnt = counts.astype(jnp.int32)
    padded = (cnt + MOE_ROWS - 1) // MOE_ROWS * MOE_ROWS
    pad_end = jnp.cumsum(padded)
    pad_start = pad_end - padded
    onehot = topi[..., None] == jnp.arange(N_EXPERTS, dtype=jnp.int32)
    dest = (jnp.sum(jnp.where(onehot, pad_start, 0), axis=-1) + rank).reshape(na)
    n_blocks = -(-na // MOE_ROWS) + N_EXPERTS
    blk_start = jnp.arange(n_blocks, dtype=jnp.int32) * MOE_ROWS
    blk_e = jnp.minimum(jnp.sum(blk_start[:, None] >= pad_end[None, :], axis=1), N_EXPERTS - 1).astype(jnp.int32)
    n_used = (pad_end[-1] // MOE_ROWS).astype(jnp.int32).reshape(1)
    return dest, blk_e, n_used


def _mods(mod):
    return [m[:, None, :] for m in jnp.split(mod, 6, axis=-1)]


def _layer(xp, xs, cp, cs, cache_k, cache_v, state, page_table, w_ada, b_ada, g_norm_mix, w_in, w_gk2, b_gk2,
           g_gla_norm, w_br_a, w_br_b, w_out, g_norm_ffn, w_router, b_router, w_gu, b_gu, w_down, b_down,
           g_final):
    B, T, _ = xp.shape
    S = xs.shape[0]
    c_all = jnp.concatenate([cp, cs], axis=0)
    pad = -c_all.shape[0] % 8
    mod = _ada_mod(jnp.pad(c_all, ((0, pad), (0, 0))), w_ada, b_ada)
    sh1p, sc1p, ga1p, sh2p, sc2p, ga2p = _mods(mod[:B])
    sh1s, sc1s, ga1s, sh2s, sc2s, ga2s = [m[None] for m in jnp.split(mod[B:B + S], 6, axis=-1)]

    w_perm = jnp.concatenate(
        [w_in[:, :COL_ZA], w_in[:, COL_ZA + G_LOWRANK:], w_in[:, COL_ZA:COL_ZA + G_LOWRANK],
         jnp.zeros((D_MODEL, LANES - G_LOWRANK), F32)], axis=1).astype(BF16)
    wa, wb, wo = w_br_a.astype(BF16), w_br_b.astype(BF16), w_out.astype(BF16)

    xs3 = xs.reshape(1, S, D_MODEL)
    proj_p = _inproj(xp, sc1p, sh1p, g_norm_mix, w_perm, tm=1024)
    proj_s = _inproj(xs3, sc1s, sh1s, g_norm_mix, w_perm, tm=S)

    a_p = _moba_prompt(proj_p)
    og_p, s_p = _gla_prompt(proj_p, w_gk2, b_gk2, g_gla_norm)
    ps = proj_s[0]
    k_s = ps[:, COL_AK:COL_AK + A_KVW]
    v_s = ps[:, COL_AV:COL_AV + A_KVW]
    a_s = _moba_sample(ps[:, :A_QW], k_s, v_s, cache_k, cache_v, page_table)
    og_s, s_s = _gla_sample(proj_s.reshape(S, 1, PROJ_W), state, w_gk2, b_gk2, g_gla_norm)

    cnt0 = jnp.zeros((1, LANES), F32)
    x1p, h2p, rtp, cnt1 = _merge_router(xp, a_p, og_p, proj_p, ga1p, sc2p, sh2p, g_norm_ffn, wa, wb, wo,
                                        w_router, b_router, cnt0, tm=512)
    x1s, h2s, rts, cnt2 = _merge_router(xs3, a_s.astype(BF16)[None], og_s.reshape(1, S, -1).astype(BF16), proj_s,
                                        ga1s, sc2s, sh2s, g_norm_ffn, wa, wb, wo, w_router, b_router, cnt1, tm=S)

    n_p = B * T
    route = jnp.concatenate([rtp.reshape(n_p, LANES), rts.reshape(S, LANES)], axis=0)
    dest, blk_e, n_used = _routing(route, cnt2[0, :N_EXPERTS])
    xsort = jnp.zeros((blk_e.shape[0] * MOE_ROWS, D_MODEL), F32)
    xsort = _dispatch(h2p.reshape(n_p, D_MODEL), dest, xsort, tok_off=0)
    xsort = _dispatch(h2s.reshape(S, D_MODEL), dest, xsort, tok_off=n_p)
    yb = _moe_experts(xsort, blk_e, n_used, w_gu, b_gu, w_down, b_down)
    y_p = _combine(yb, dest, x1p, rtp, ga2p, g_final, tok_off=0)
    y_s = _combine(yb, dest, x1s, rts, ga2s, g_final, tok_off=n_p)

    kp = proj_p[:, :, COL_AK:COL_AK + A_KVW].reshape(B, T, A_KV_HEADS, A_HEAD_DIM)
    vp = proj_p[:, :, COL_AV:COL_AV + A_KVW].reshape(B, T, A_KV_HEADS, A_HEAD_DIM)
    ks = k_s.reshape(S, 1, A_KV_HEADS, A_HEAD_DIM)
    vs = v_s.reshape(S, 1, A_KV_HEADS, A_HEAD_DIM)
    return y_p, y_s.reshape(S, 1, D_MODEL), kp, vp, s_p, ks, vs, s_s


def kernel(x_prompt, x_sample, c_prompt, c_sample, cache_k, cache_v, state_gla, page_table, w_ada, b_ada, g_norm_mix, w_in, w_gk2, b_gk2, g_gla_norm, w_br_a, w_br_b, w_out, g_norm_ffn, w_router, b_router, w_gu, b_gu, w_down, b_down, g_final):
    assert w_ada.shape[0] == 1, "single-layer step"
    outs = _layer(x_prompt, x_sample.reshape(x_sample.shape[0], D_MODEL), c_prompt, c_sample, cache_k[0],
                  cache_v[0], state_gla[0], page_table, w_ada[0], b_ada[0], g_norm_mix[0], w_in[0], w_gk2[0],
                  b_gk2[0], g_gla_norm[0], w_br_a[0], w_br_b[0], w_out[0], g_norm_ffn[0], w_router[0],
                  b_router[0], w_gu[0], b_gu[0], w_down[0], b_down[0], g_final)
    y_p, y_s, kp, vp, s_p, ks, vs, s_s = outs
    return (y_p, y_s, kp[None], vp[None], s_p[None], ks[None], vs[None], s_s[None])
```

```python
import functools

import jax
import jax.numpy as jnp
from jax import lax
from jax.experimental import pallas as pl
from jax.experimental.pallas import tpu as pltpu

F32 = jnp.float32
BF16 = jnp.bfloat16

D_MODEL = 1024
A_HEADS = 8
A_KV_HEADS = 4
A_GROUP = A_HEADS // A_KV_HEADS
A_HEAD_DIM = 64
A_QW = A_HEADS * A_HEAD_DIM
A_KVW = A_KV_HEADS * A_HEAD_DIM
A_BLOCK = 256
A_TOPK = 3
PAGE_SIZE = 128
G_HEADS = 4
G_KEY_DIM = D_MODEL // 2
G_VAL_DIM = D_MODEL
G_DK = G_KEY_DIM // G_HEADS
G_DV = G_VAL_DIM // G_HEADS
G_LOWRANK = 16
G_NORMALIZER = 16.0
G_CHUNK = 64
N_EXPERTS = 32
TOP_K = 4
D_FF = D_MODEL
SWIGLU_LIMIT = 7.0
SWIGLU_ALPHA = 1.702
EPS = 1e-6

LANES = 128
COL_AQ = 0
COL_AK = COL_AQ + A_QW
COL_AV = COL_AK + A_KVW
COL_GQ = COL_AV + A_KVW
COL_GK = COL_GQ + G_KEY_DIM
COL_GV = COL_GK + G_KEY_DIM
COL_GG = COL_GV + G_VAL_DIM
COL_ZA = COL_GG + G_VAL_DIM
COL_ZB = COL_ZA + D_MODEL
COL_LR = COL_ZB + D_MODEL
PROJ_TN = 1280
PROJ_W = 5 * PROJ_TN
assert PROJ_W >= COL_LR + LANES
NEG_BIG = -1e30
VMEM_LIMIT = 56 * 1024 * 1024

NT_DIMS = (((1,), (1,)), ((), ()))
NN_DIMS = (((1,), (0,)), ((), ()))


def _params(sem):
    return pltpu.CompilerParams(dimension_semantics=sem, vmem_limit_bytes=VMEM_LIMIT)


def _split(a):
    hi = a.astype(BF16)
    lo = (a - hi.astype(F32)).astype(BF16)
    return hi, lo


def _dot3(a, b, dims=NN_DIMS):
    ah, al = _split(a)
    bh, bl = _split(b)
    d = lambda x, y: lax.dot_general(x, y, dims, preferred_element_type=F32)
    return d(ah, bh) + d(al, bh) + d(ah, bl)


def _sigmoid(x):
    return 1.0 / (1.0 + jnp.exp(-x))


def _log_sigmoid(x):
    return -(jnp.maximum(-x, 0.0) + jnp.log(1.0 + jnp.exp(-jnp.abs(x))))


def _rmsnorm(x, g):
    return x * lax.rsqrt(jnp.mean(x * x, axis=-1, keepdims=True) + EPS) * g


def _top_cols(score, colf, n_pick):
    vals, cols = [], []
    big = float(score.shape[1])
    for _ in range(n_pick):
        mx = jnp.max(score, axis=1, keepdims=True)
        first = jnp.min(jnp.where(score == mx, colf, big), axis=1, keepdims=True)
        vals.append(mx)
        cols.append(first)
        score = jnp.where(colf == first, -jnp.inf, score)
    return vals, cols


def _ada_kernel(c_ref, w_ref, b_ref, o_ref):
    c = c_ref[...]
    o_ref[...] = _dot3(c * _sigmoid(c), w_ref[...]) + b_ref[...]


def _ada_mod(c, w_ada, b_ada):
    rows = c.shape[0]
    tn = 1536
    return pl.pallas_call(
        _ada_kernel,
        grid=(6 * D_MODEL // tn,),
        in_specs=[
            pl.BlockSpec((rows, D_MODEL), lambda j: (0, 0)),
            pl.BlockSpec((D_MODEL, tn), lambda j: (0, j)),
            pl.BlockSpec((1, tn), lambda j: (0, j)),
        ],
        out_specs=pl.BlockSpec((rows, tn), lambda j: (0, j)),
        out_shape=jax.ShapeDtypeStruct((rows, 6 * D_MODEL), F32),
        compiler_params=_params(("arbitrary",)),
        name="ada_mod",
    )(c, w_ada, b_ada.reshape(1, -1))


def _inproj_kernel(x_ref, sc_ref, sh_ref, g_ref, w_ref, o_ref, h_ref):
    @pl.when(pl.program_id(2) == 0)
    def _():
        y = _rmsnorm(x_ref[0], g_ref[...])
        h_ref[...] = (y * (1.0 + sc_ref[0]) + sh_ref[0]).astype(BF16)

    o_ref[0] = jnp.dot(h_ref[...], w_ref[...], preferred_element_type=F32)


def _inproj(x, sc, sh, g, w_perm, tm):
    G, T, _ = x.shape
    R = sc.shape[1]
    rm = 1 if R == 1 else tm
    mod_map = (lambda b, i, j: (b, 0, 0)) if R == 1 else (lambda b, i, j: (b, i, 0))
    return pl.pallas_call(
        _inproj_kernel,
        grid=(G, T // tm, PROJ_W // PROJ_TN),
        in_specs=[
            pl.BlockSpec((1, tm, D_MODEL), lambda b, i, j: (b, i, 0)),
            pl.BlockSpec((1, rm, D_MODEL), mod_map),
            pl.BlockSpec((1, rm, D_MODEL), mod_map),
            pl.BlockSpec((1, D_MODEL), lambda b, i, j: (0, 0)),
            pl.BlockSpec((D_MODEL, PROJ_TN), lambda b, i, j: (0, j)),
        ],
        out_specs=pl.BlockSpec((1, tm, PROJ_TN), lambda b, i, j: (b, i, j)),
        out_shape=jax.ShapeDtypeStruct((G, T, PROJ_W), F32),
        scratch_shapes=[pltpu.VMEM((tm, D_MODEL), BF16)],
        compiler_params=_params(("arbitrary", "arbitrary", "arbitrary")),
        name="inproj",
    )(x, sc, sh, g.reshape(1, -1), w_perm)


PREP_T = 2048
AUG_NB = 32
AUG_REST = LANES - A_HEAD_DIM - AUG_NB
KV_GROUP = 4


def _moba_prep_kernel(k_ref, v_ref, ka_ref, vt_ref, km_ref):
    t = pl.program_id(1)
    k = k_ref[0]
    v = v_ref[0]
    nblk = PREP_T // A_BLOCK
    km_ref[0] = jnp.mean(k.reshape(nblk, A_BLOCK, A_KVW), axis=1)
    half = LANES - A_HEAD_DIM
    row = lax.broadcasted_iota(jnp.int32, (A_BLOCK, half), 0)
    col = lax.broadcasted_iota(jnp.int32, (A_BLOCK, half), 1)
    vextra = jnp.where(col == 0, 1.0, 0.0)
    for jb in range(nblk):
        blk = t * nblk + jb
        kextra = jnp.where(col == blk, 1.0, 0.0)
        kextra = jnp.where((col == AUG_NB) | (col == AUG_NB + 1), 1.0, kextra)
        kextra = jnp.where(col == AUG_NB + 2, row.astype(F32), kextra)
        kextra = jnp.where(col == AUG_NB + 3, (blk * A_BLOCK).astype(F32), kextra)
        rows = slice(jb * A_BLOCK, (jb + 1) * A_BLOCK)
        for h in range(A_KV_HEADS):
            cols = slice(h * A_HEAD_DIM, (h + 1) * A_HEAD_DIM)
            ka_ref[0, h, jb] = jnp.concatenate([k[rows, cols], kextra], axis=1).astype(BF16)
            vt_ref[0, h, jb] = jnp.concatenate([v[rows, cols], vextra], axis=1).T.astype(BF16)


def _moba_prep(proj):
    B, T, _ = proj.shape
    nblk = PREP_T // A_BLOCK
    nb = T // A_BLOCK
    assert nb <= AUG_NB
    return pl.pallas_call(
        _moba_prep_kernel,
        grid=(B, T // PREP_T),
        in_specs=[
            pl.BlockSpec((1, PREP_T, A_KVW), lambda b, t: (b, t, COL_AK // A_KVW)),
            pl.BlockSpec((1, PREP_T, A_KVW), lambda b, t: (b, t, COL_AV // A_KVW)),
        ],
        out_specs=[
            pl.BlockSpec((1, A_KV_HEADS, nblk, A_BLOCK, LANES), lambda b, t: (b, 0, t, 0, 0)),
            pl.BlockSpec((1, A_KV_HEADS, nblk, LANES, A_BLOCK), lambda b, t: (b, 0, t, 0, 0)),
            pl.BlockSpec((1, nblk, A_KVW), lambda b, t: (b, t, 0)),
        ],
        out_shape=[
            jax.ShapeDtypeStruct((B, A_KV_HEADS, nb, A_BLOCK, LANES), BF16),
            jax.ShapeDtypeStruct((B, A_KV_HEADS, nb, LANES, A_BLOCK), BF16),
            jax.ShapeDtypeStruct((B, nb, A_KVW), F32),
        ],
        compiler_params=_params(("arbitrary", "arbitrary")),
        name="moba_prep",
    )(proj, proj)


def _slope_of(head_plus_one, shape):
    bits = (127 - head_plus_one) << 23
    return lax.bitcast_convert_type(jnp.full(shape, bits, jnp.int32), F32)


def _top_rows(score, rowf, n_pick, n_valid):
    picked = jnp.zeros(score.shape, jnp.bool_)
    big = float(score.shape[0])
    for r in range(n_pick):
        mx = jnp.max(score, axis=0, keepdims=True)
        first = jnp.min(jnp.where(score == mx, rowf, big), axis=0, keepdims=True)
        hit = rowf == first
        picked = picked | (hit & ((jnp.zeros_like(rowf) + n_valid) > r))
        score = jnp.where(hit, -jnp.inf, score)
    return picked


def _moba_attn_kernel(q_ref, km_ref, ka_ref, vt_ref, o_ref, qa_ref, qo_ref, m_ref, acc_ref, sa_ref, sb_ref):
    kvh = pl.program_id(1)
    i = pl.program_id(2)
    nb = km_ref.shape[2]
    qt = q_ref[0].T
    km = km_ref[0, 0]
    blk_f = i.astype(F32)
    rowf = lax.broadcasted_iota(jnp.int32, (nb, A_BLOCK), 0).astype(F32)
    erow = lax.broadcasted_iota(jnp.int32, (AUG_REST, A_BLOCK), 0)
    qq = lax.broadcasted_iota(jnp.int32, (AUG_REST, A_BLOCK), 1).astype(F32)
    for hh in range(A_GROUP):
        qh = qt[hh * A_HEAD_DIM:(hh + 1) * A_HEAD_DIM, :]
        gate = _dot3(km, qh)
        gate = jnp.where(rowf < blk_f, gate, -jnp.inf)
        picked = _top_rows(gate, rowf, A_TOPK, blk_f)
        selbias = jnp.where(picked, 0.0, NEG_BIG)
        if nb < AUG_NB:
            selbias = jnp.concatenate([selbias, jnp.zeros((AUG_NB - nb, A_BLOCK), F32)], axis=0)
        slope = _slope_of(kvh * A_GROUP + hh + 1, (AUG_REST, A_BLOCK))
        alibi = jnp.where(erow == 0, -slope * qq, 0.0)
        alibi = jnp.where(erow == 1, -slope * (blk_f * A_BLOCK), alibi)
        alibi = jnp.where((erow == 2) | (erow == 3), slope, alibi)
        qs = qh * (A_HEAD_DIM ** -0.5)
        qa_ref[hh] = jnp.concatenate([qs, selbias, alibi], axis=0).astype(BF16)
        qo_ref[hh] = jnp.concatenate([qs, jnp.zeros((AUG_NB, A_BLOCK), F32), alibi], axis=0).astype(BF16)

    kb = ka_ref[0, 0, i]
    vb = vt_ref[0, 0, i]
    key = lax.broadcasted_iota(jnp.int32, (A_BLOCK, A_BLOCK), 0)
    qry = lax.broadcasted_iota(jnp.int32, (A_BLOCK, A_BLOCK), 1)
    for hh in range(A_GROUP):
        s = jnp.dot(kb, qo_ref[hh], preferred_element_type=F32)
        s = jnp.where(key <= qry, s, NEG_BIG)
        m = jnp.max(s, axis=0, keepdims=True)
        p = jnp.exp(s - m)
        m_ref[hh] = m
        acc_ref[hh] = jnp.dot(vb, p.astype(BF16), preferred_element_type=F32)

    n_groups = (i + KV_GROUP - 1) // KV_GROUP

    def scores(g, s_ref):
        for j in range(KV_GROUP):
            kj = ka_ref[0, 0, g * KV_GROUP + j]
            for hh in range(A_GROUP):
                s_ref[hh, j] = jnp.dot(kj, qa_ref[hh], preferred_element_type=F32)

    def accumulate(g, s_ref):
        for hh in range(A_GROUP):
            m_old = m_ref[hh]
            m_new = m_old
            for j in range(KV_GROUP):
                m_new = jnp.maximum(m_new, jnp.max(s_ref[hh, j], axis=0, keepdims=True))
            acc = jnp.exp(m_old - m_new) * acc_ref[hh]
            for j in range(KV_GROUP):
                p = jnp.exp(s_ref[hh, j] - m_new).astype(BF16)
                acc = acc + jnp.dot(vt_ref[0, 0, g * KV_GROUP + j], p, preferred_element_type=F32)
            acc_ref[hh] = acc
            m_ref[hh] = m_new

    def on_parity(g, fn):
        @pl.when(g % 2 == 0)
        def _():
            fn(sa_ref, sb_ref)

        @pl.when(g % 2 == 1)
        def _():
            fn(sb_ref, sa_ref)

    @pl.when(n_groups > 0)
    def _():
        scores(0, sa_ref)

        def body(g, carry):
            def step(cur, nxt):
                scores(g + 1, nxt)
                accumulate(g, cur)

            on_parity(g, step)
            return carry

        lax.fori_loop(0, n_groups - 1, body, 0)
        on_parity(n_groups - 1, lambda cur, nxt: accumulate(n_groups - 1, cur))

    outs = []
    for hh in range(A_GROUP):
        acc = acc_ref[hh]
        outs.append((acc[:A_HEAD_DIM] / acc[A_HEAD_DIM:A_HEAD_DIM + 1]).T)
    o_ref[0] = jnp.concatenate(outs, axis=1).astype(BF16)


def _moba_prompt(proj):
    B, T, _ = proj.shape
    nb = T // A_BLOCK
    assert nb % KV_GROUP == 0
    kaug, vaug_t, km = _moba_prep(proj)
    km = km.reshape(B, nb, A_KV_HEADS, A_HEAD_DIM).transpose(0, 2, 1, 3)
    qw = A_GROUP * A_HEAD_DIM
    return pl.pallas_call(
        _moba_attn_kernel,
        grid=(B, A_KV_HEADS, nb),
        in_specs=[
            pl.BlockSpec((1, A_BLOCK, qw), lambda b, h, i: (b, i, h)),
            pl.BlockSpec((1, 1, nb, A_HEAD_DIM), lambda b, h, i: (b, h, 0, 0)),
            pl.BlockSpec((1, 1, nb, A_BLOCK, LANES), lambda b, h, i: (b, h, 0, 0, 0)),
            pl.BlockSpec((1, 1, nb, LANES, A_BLOCK), lambda b, h, i: (b, h, 0, 0, 0)),
        ],
        out_specs=pl.BlockSpec((1, A_BLOCK, qw), lambda b, h, i: (b, i, h)),
        out_shape=jax.ShapeDtypeStruct((B, T, A_QW), BF16),
        scratch_shapes=[
            pltpu.VMEM((A_GROUP, LANES, A_BLOCK), BF16),
            pltpu.VMEM((A_GROUP, LANES, A_BLOCK), BF16),
            pltpu.VMEM((A_GROUP, 1, A_BLOCK), F32),
            pltpu.VMEM((A_GROUP, LANES, A_BLOCK), F32),
            pltpu.VMEM((A_GROUP, KV_GROUP, A_BLOCK, A_BLOCK), F32),
            pltpu.VMEM((A_GROUP, KV_GROUP, A_BLOCK, A_BLOCK), F32),
        ],
        compiler_params=_params(("arbitrary", "arbitrary", "arbitrary")),
        name="moba_attn",
    )(proj, km, kaug, vaug_t)


GLA_T = 1024
GLA_UNROLL = 8


def _gla_kernel(q_ref, k_ref, v_ref, gg_ref, lr_ref, w2_ref, b2_ref, gn_ref, og_ref, st_ref, s_ref, g_ref):
    t = pl.program_id(2)

    @pl.when(t == 0)
    def _():
        s_ref[...] = jnp.zeros_like(s_ref)

    z = _dot3(lr_ref[0][:, :G_LOWRANK], w2_ref[...]) + b2_ref[...]
    g_ref[...] = _log_sigmoid(z) * (1.0 / G_NORMALIZER)
    C = G_CHUNK
    rr = lax.broadcasted_iota(jnp.int32, (C, C), 0)
    cc = lax.broadcasted_iota(jnp.int32, (C, C), 1)
    causal = cc <= rr
    tril = jnp.where(causal, 1.0, 0.0).astype(BF16)
    er = lax.broadcasted_iota(jnp.int32, (G_DK, G_DK), 0)
    ec = lax.broadcasted_iota(jnp.int32, (G_DK, G_DK), 1)
    eye = er == ec
    gn = gn_ref[...]

    def group(gi, s):
        sls = [pl.ds(pl.multiple_of((gi * GLA_UNROLL + c) * C, C), C) for c in range(GLA_UNROLL)]
        cums = []
        for sl in sls:
            gh, gl = _split(g_ref[sl, :])
            cums.append(jnp.dot(tril, gh, preferred_element_type=F32) + jnp.dot(tril, gl, preferred_element_type=F32))
        parts = []
        for sl, cum in zip(sls, cums):
            last = cum[C - 1:C, :]
            kc = k_ref[0, sl, :]
            vc = v_ref[0, sl, :].astype(BF16)
            qe = (q_ref[0, sl, :] * (G_DK ** -0.5) * jnp.exp(cum)).astype(BF16)
            ke = (kc * jnp.exp(-cum)).astype(BF16)
            kd = kc * jnp.exp(last - cum)
            att = lax.dot_general(qe, ke, NT_DIMS, preferred_element_type=F32)
            att = jnp.where(causal, att, 0.0).astype(BF16)
            intra = jnp.dot(att, vc, preferred_element_type=F32)
            inc = jnp.dot(kd.T.astype(BF16), vc, preferred_element_type=F32)
            decay = jnp.sum(jnp.where(eye, jnp.exp(last), 0.0), axis=1, keepdims=True)
            parts.append((qe, intra, inc, decay))
        for sl, (qe, intra, inc, decay) in zip(sls, parts):
            o = jnp.dot(qe, s.astype(BF16), preferred_element_type=F32) + intra
            s = s * decay + inc
            on = _rmsnorm(o, gn)
            gate = gg_ref[0, sl, :]
            og_ref[0, sl, :] = (on * (gate * _sigmoid(gate))).astype(BF16)
        return s

    s_ref[...] = lax.fori_loop(0, GLA_T // (C * GLA_UNROLL), group, s_ref[...])

    @pl.when(t == pl.num_programs(2) - 1)
    def _():
        st_ref[0, 0] = s_ref[...]


def _gla_prompt(proj, w_gk2, b_gk2, g_gla_norm):
    B, T, _ = proj.shape
    tt = min(GLA_T, T)
    assert tt == GLA_T
    return pl.pallas_call(
        _gla_kernel,
        grid=(B, G_HEADS, T // tt),
        in_specs=[
            pl.BlockSpec((1, tt, G_DK), lambda b, h, t: (b, t, COL_GQ // G_DK + h)),
            pl.BlockSpec((1, tt, G_DK), lambda b, h, t: (b, t, COL_GK // G_DK + h)),
            pl.BlockSpec((1, tt, G_DV), lambda b, h, t: (b, t, COL_GV // G_DV + h)),
            pl.BlockSpec((1, tt, G_DV), lambda b, h, t: (b, t, COL_GG // G_DV + h)),
            pl.BlockSpec((1, tt, LANES), lambda b, h, t: (b, t, COL_LR // LANES)),
            pl.BlockSpec((G_LOWRANK, G_DK), lambda b, h, t: (0, h)),
            pl.BlockSpec((1, G_DK), lambda b, h, t: (0, h)),
            pl.BlockSpec((1, G_DV), lambda b, h, t: (0, 0)),
        ],
        out_specs=[
            pl.BlockSpec((1, tt, G_DV), lambda b, h, t: (b, t, h)),
            pl.BlockSpec((1, 1, G_DK, G_DV), lambda b, h, t: (b, h, 0, 0)),
        ],
        out_shape=[
            jax.ShapeDtypeStruct((B, T, G_VAL_DIM), BF16),
            jax.ShapeDtypeStruct((B, G_HEADS, G_DK, G_DV), F32),
        ],
        scratch_shapes=[pltpu.VMEM((G_DK, G_DV), F32), pltpu.VMEM((tt, G_DK), F32)],
        compiler_params=_params(("arbitrary", "arbitrary", "arbitrary")),
        name="gla_prompt",
    )(proj, proj, proj, proj, proj, w_gk2, b_gk2.reshape(1, -1), g_gla_norm.reshape(1, -1))


PPB = A_BLOCK // PAGE_SIZE


def _smoba_kernel(pt_ref, qb_ref, q8_ref, kn_ref, vn_ref, ck_ref, cv_ref, o_ref,
                  kbuf, vbuf, lg_ref, p_ref, pown_ref, idx_v, idx_s, ksem, vsem, isem):
    s = pl.program_id(0)
    n_seq = pl.num_programs(0) - 1
    n_pages = kbuf.shape[1]
    nb = n_pages // PPB
    past = n_pages * PAGE_SIZE
    hd = A_HEAD_DIM

    def start_keys(seq, slot):
        for p in range(n_pages):
            pltpu.make_async_copy(ck_ref.at[pt_ref[seq, p]], kbuf.at[slot, p], ksem.at[slot]).start()

    def value_copy(seq, slot, h, r, e):
        page = pt_ref[seq, idx_s[slot, h, r] * PPB + e]
        return pltpu.make_async_copy(cv_ref.at[page, pl.ds((h // A_GROUP) * hd, hd), :],
                                     vbuf.at[slot, h, r, e], vsem.at[slot])

    @pl.when(s == 0)
    def _():
        start_keys(0, 0)

    @pl.when(s + 1 < n_seq)
    def _():
        start_keys(s + 1, (s + 1) % 2)

    @pl.when(s < n_seq)
    def _():
        slot = s % 2
        pltpu.make_async_copy(ck_ref.at[pl.ds(0, n_pages)], kbuf.at[slot], ksem.at[slot]).wait()

        def page_logits(p, carry):
            kp = kbuf[slot, p]
            rows = []
            for h in range(A_HEADS):
                g = h // A_GROUP
                rows.append(jnp.sum(qb_ref[0, h] * kp[g * hd:(g + 1) * hd, :], axis=0, keepdims=True))
            lg_ref[slot, p] = jnp.concatenate(rows, axis=0)
            return carry

        lax.fori_loop(0, n_pages, page_logits, 0)
        lg = lg_ref[slot]
        gate = jnp.sum(jnp.sum(lg.reshape(nb, PPB, A_HEADS, PAGE_SIZE), axis=1), axis=2, keepdims=True)
        bidx = lax.broadcasted_iota(jnp.int32, (nb, A_HEADS, 1), 0).astype(F32)
        picked = jnp.zeros((nb, A_HEADS, 1), F32)
        firsts = []
        for r in range(A_TOPK):
            mx = jnp.max(gate, axis=0, keepdims=True)
            first = jnp.min(jnp.where(gate == mx, bidx, float(nb)), axis=0, keepdims=True)
            hit = bidx == first
            picked = jnp.where(hit, 1.0, picked)
            gate = jnp.where(hit, -jnp.inf, gate)
            firsts.append(first.reshape(A_HEADS, 1))
        sel = jnp.broadcast_to(picked[:, None], (nb, PPB, A_HEADS, 1)).reshape(n_pages, A_HEADS, 1)
        head1 = lax.broadcasted_iota(jnp.int32, (1, A_HEADS, 1), 1) + 1
        slope = lax.bitcast_convert_type((127 - head1) << 23, F32)
        tpos = (lax.broadcasted_iota(jnp.int32, (n_pages, 1, PAGE_SIZE), 0) * PAGE_SIZE
                + lax.broadcasted_iota(jnp.int32, (n_pages, 1, PAGE_SIZE), 2))
        dist = (past - tpos).astype(F32)
        att = jnp.where(sel > 0.5, lg * (hd ** -0.5) - slope * dist, NEG_BIG)
        own = jnp.sum(q8_ref[0] * kn_ref[0], axis=1, keepdims=True) * (hd ** -0.5)
        m = jnp.maximum(jnp.max(jnp.max(att, axis=0), axis=1, keepdims=True), own)
        p = jnp.exp(att - m.reshape(1, A_HEADS, 1))
        p_own = jnp.exp(own - m)
        inv = 1.0 / (jnp.sum(jnp.sum(p, axis=0), axis=1, keepdims=True) + p_own)
        p_ref[slot] = p * inv.reshape(1, A_HEADS, 1)
        pown_ref[slot] = jnp.broadcast_to(p_own * inv, (A_HEADS, LANES))
        lane = lax.broadcasted_iota(jnp.int32, (A_HEADS, LANES), 1)
        it = jnp.zeros((A_HEADS, LANES), F32)
        for r in range(A_TOPK):
            it = jnp.where(lane == r, firsts[r], it)
        idx_v[...] = it.astype(jnp.int32)
        to_smem = pltpu.make_async_copy(idx_v, idx_s.at[slot], isem)
        to_smem.start()
        to_smem.wait()
        for h in range(A_HEADS):
            for r in range(A_TOPK):
                for e in range(PPB):
                    value_copy(s, slot, h, r, e).start()

    @pl.when(s >= 1)
    def _():
        sb = s - 1
        b = sb % 2
        for h in range(A_HEADS):
            for r in range(A_TOPK):
                for e in range(PPB):
                    value_copy(sb, b, h, r, e).wait()
        rowid = lax.broadcasted_iota(jnp.int32, (A_HEADS, hd), 0)
        out = pown_ref[b][:, :hd] * vn_ref[0]
        for h in range(A_HEADS):
            part = jnp.zeros((A_HEADS, hd), F32)
            for r in range(A_TOPK):
                for e in range(PPB):
                    pg = idx_s[b, h, r] * PPB + e
                    part = part + lax.dot_general(p_ref[b, pg].astype(BF16), vbuf[b, h, r, e].astype(BF16),
                                                  NT_DIMS, preferred_element_type=F32)
            out = out + jnp.where(rowid == h, part, 0.0)
        o_ref[0] = out


def _moba_sample(q, k_new, v_new, cache_k, cache_v, page_table):
    S = q.shape[0]
    n_pages = page_table.shape[1]
    n_pool = cache_k.shape[0]
    hd = A_HEAD_DIM
    ck = cache_k.transpose(0, 2, 3, 1).reshape(n_pool, A_KVW, PAGE_SIZE)
    cv = cache_v.transpose(0, 2, 3, 1).reshape(n_pool, A_KVW, PAGE_SIZE)
    q8 = q.reshape(S, A_HEADS, hd)
    qb = jnp.broadcast_to(q8[..., None], (S, A_HEADS, hd, LANES))
    kn8 = jnp.repeat(k_new.reshape(S, A_KV_HEADS, hd), A_GROUP, axis=1)
    vn8 = jnp.repeat(v_new.reshape(S, A_KV_HEADS, hd), A_GROUP, axis=1)
    seq = lambda s, pt: (jnp.minimum(s, S - 1), 0, 0)
    out = pl.pallas_call(
        _smoba_kernel,
        grid_spec=pltpu.PrefetchScalarGridSpec(
            num_scalar_prefetch=1,
            grid=(S + 1,),
            in_specs=[
                pl.BlockSpec((1, A_HEADS, hd, LANES), lambda s, pt: (jnp.minimum(s, S - 1), 0, 0, 0)),
                pl.BlockSpec((1, A_HEADS, hd), seq),
                pl.BlockSpec((1, A_HEADS, hd), seq),
                pl.BlockSpec((1, A_HEADS, hd), lambda s, pt: (jnp.maximum(s - 1, 0), 0, 0)),
                pl.BlockSpec(memory_space=pl.ANY),
                pl.BlockSpec(memory_space=pl.ANY),
            ],
            out_specs=pl.BlockSpec((1, A_HEADS, hd), lambda s, pt: (jnp.maximum(s - 1, 0), 0, 0)),
            scratch_shapes=[
                pltpu.VMEM((2, n_pages, A_KVW, PAGE_SIZE), F32),
                pltpu.VMEM((2, A_HEADS, A_TOPK, PPB, hd, PAGE_SIZE), F32),
                pltpu.VMEM((2, n_pages, A_HEADS, PAGE_SIZE), F32),
                pltpu.VMEM((2, n_pages, A_HEADS, PAGE_SIZE), F32),
                pltpu.VMEM((2, A_HEADS, LANES), F32),
                pltpu.VMEM((A_HEADS, LANES), jnp.int32),
                pltpu.SMEM((2, A_HEADS, LANES), jnp.int32),
                pltpu.SemaphoreType.DMA((2,)),
                pltpu.SemaphoreType.DMA((2,)),
                pltpu.SemaphoreType.DMA(()),
            ],
        ),
        out_shape=jax.ShapeDtypeStruct((S, A_HEADS, hd), F32),
        compiler_params=_params(("arbitrary",)),
        name="moba_sample",
    )(page_table, qb, q8, kn8, vn8, ck, cv)
    return out.reshape(S, A_QW)


def _sgla_kernel(p_ref, s0_ref, w2_ref, b2_ref, gn_ref, og_ref, s1_ref):
    row = p_ref[0]
    er = lax.broadcasted_iota(jnp.int32, (G_DK, G_DK), 0)
    ec = lax.broadcasted_iota(jnp.int32, (G_DK, G_DK), 1)
    eye = er == ec
    col_of = lambda r: jnp.sum(jnp.where(eye, r, 0.0), axis=1, keepdims=True)
    lr = row[:, COL_LR:COL_LR + G_LOWRANK]
    z = _dot3(lr, w2_ref[...]) + b2_ref[...]
    g_all = _log_sigmoid(z) * (1.0 / G_NORMALIZER)
    outs = []
    for h in range(G_HEADS):
        g = g_all[:, h * G_DK:(h + 1) * G_DK]
        q = row[:, COL_GQ + h * G_DK:COL_GQ + (h + 1) * G_DK] * (G_DK ** -0.5)
        k = row[:, COL_GK + h * G_DK:COL_GK + (h + 1) * G_DK]
        v = row[:, COL_GV + h * G_DV:COL_GV + (h + 1) * G_DV]
        gate = row[:, COL_GG + h * G_DV:COL_GG + (h + 1) * G_DV]
        qe = q * jnp.exp(g)
        ke = k * jnp.exp(-g)
        att = jnp.sum(qe * ke, axis=1, keepdims=True)
        s0 = s0_ref[0, h]
        o = _dot3(qe, s0) + att * v
        s1_ref[0, h] = s0 * col_of(jnp.exp(g)) + col_of(k) * v
        on = _rmsnorm(o, gn_ref[...])
        outs.append(on * (gate * _sigmoid(gate)))
    og_ref[0] = jnp.concatenate(outs, axis=1)


def _gla_sample(proj_rows, state, w_gk2, b_gk2, g_gla_norm):
    S = proj_rows.shape[0]
    return pl.pallas_call(
        _sgla_kernel,
        grid=(S,),
        in_specs=[
            pl.BlockSpec((1, 1, PROJ_W), lambda s: (s, 0, 0)),
            pl.BlockSpec((1, G_HEADS, G_DK, G_DV), lambda s: (s, 0, 0, 0)),
            pl.BlockSpec((G_LOWRANK, G_KEY_DIM), lambda s: (0, 0)),
            pl.BlockSpec((1, G_KEY_DIM), lambda s: (0, 0)),
            pl.BlockSpec((1, G_DV), lambda s: (0, 0)),
        ],
        out_specs=[
            pl.BlockSpec((1, 1, G_VAL_DIM), lambda s: (s, 0, 0)),
            pl.BlockSpec((1, G_HEADS, G_DK, G_DV), lambda s: (s, 0, 0, 0)),
        ],
        out_shape=[
            jax.ShapeDtypeStruct((S, 1, G_VAL_DIM), F32),
            jax.ShapeDtypeStruct((S, G_HEADS, G_DK, G_DV), F32),
        ],
        compiler_params=_params(("arbitrary",)),
        name="gla_sample",
    )(proj_rows, state, w_gk2, b_gk2.reshape(1, -1), g_gla_norm.reshape(1, -1))


ROUTE_E = 0
ROUTE_W = TOP_K
ROUTE_R = 2 * TOP_K


def _merge_kernel(x_ref, a_ref, og_ref, za_ref, zb_ref, ga1_ref, sc2_ref, sh2_ref, g2_ref, wa_ref, wb_ref,
                  wo_ref, wr_ref, br_ref, c0_ref, x1_ref, h2_ref, rt_ref, cnt_ref, run_ref):
    first = (pl.program_id(0) == 0) & (pl.program_id(1) == 0)

    @pl.when(first)
    def _():
        run_ref[...] = c0_ref[...]

    pa = jnp.dot(a_ref[0], wa_ref[...], preferred_element_type=F32)
    pb = jnp.dot(og_ref[0], wb_ref[...], preferred_element_type=F32)
    merged = _sigmoid(za_ref[0]) * pa + _sigmoid(zb_ref[0]) * pb
    mix = jnp.dot(merged.astype(BF16), wo_ref[...], preferred_element_type=F32)
    x1 = x_ref[0] + ga1_ref[0] * mix
    x1_ref[0] = x1
    h2 = _rmsnorm(x1, g2_ref[...]) * (1.0 + sc2_ref[0]) + sh2_ref[0]
    h2_ref[0] = h2
    logits = _dot3(h2, wr_ref[...]) + br_ref[...]
    tm = logits.shape[0]
    colf = lax.broadcasted_iota(jnp.int32, (tm, N_EXPERTS), 1).astype(F32)
    vals, cols = _top_cols(logits, colf, TOP_K)
    ex = [jnp.exp(v - vals[0]) for v in vals]
    tot = ex[0] + ex[1] + ex[2] + ex[3]
    onehot = [jnp.where(colf == c, 1.0, 0.0) for c in cols]
    osum = onehot[0] + onehot[1] + onehot[2] + onehot[3]
    tr = lax.broadcasted_iota(jnp.int32, (tm, tm), 0)
    tc = lax.broadcasted_iota(jnp.int32, (tm, tm), 1)
    lower = jnp.where(tc < tr, 1.0, 0.0).astype(BF16)
    before = jnp.dot(lower, osum.astype(BF16), preferred_element_type=F32) + run_ref[:, :N_EXPERTS]
    lane = lax.broadcasted_iota(jnp.int32, (tm, LANES), 1)
    rec = jnp.zeros((tm, LANES), F32)
    for r in range(TOP_K):
        rank = jnp.sum(onehot[r] * before, axis=1, keepdims=True)
        rec = jnp.where(lane == ROUTE_E + r, cols[r], rec)
        rec = jnp.where(lane == ROUTE_W + r, ex[r] / tot, rec)
        rec = jnp.where(lane == ROUTE_R + r, rank, rec)
    rt_ref[0] = rec
    pad = jnp.zeros((1, LANES - N_EXPERTS), F32)
    run_ref[...] = run_ref[...] + jnp.concatenate([jnp.sum(osum, axis=0, keepdims=True), pad], axis=1)
    cnt_ref[...] = run_ref[...]


def _merge_router(x, a, og, proj, ga1, sc2, sh2, g2, wa, wb, wo, wr, br, cnt0, tm):
    G, T, _ = x.shape
    R = ga1.shape[1]
    rm = 1 if R == 1 else tm
    mod_map = (lambda b, i: (b, 0, 0)) if R == 1 else (lambda b, i: (b, i, 0))
    tok = lambda w: pl.BlockSpec((1, tm, w), lambda b, i: (b, i, 0))
    full = lambda r, c: pl.BlockSpec((r, c), lambda b, i: (0, 0))
    return pl.pallas_call(
        _merge_kernel,
        grid=(G, T // tm),
        in_specs=[
            tok(D_MODEL), tok(A_QW), tok(G_VAL_DIM),
            pl.BlockSpec((1, tm, D_MODEL), lambda b, i: (b, i, COL_ZA // D_MODEL)),
            pl.BlockSpec((1, tm, D_MODEL), lambda b, i: (b, i, COL_ZB // D_MODEL)),
            pl.BlockSpec((1, rm, D_MODEL), mod_map),
            pl.BlockSpec((1, rm, D_MODEL), mod_map),
            pl.BlockSpec((1, rm, D_MODEL), mod_map),
            full(1, D_MODEL), full(A_QW, D_MODEL), full(G_VAL_DIM, D_MODEL), full(D_MODEL, D_MODEL),
            full(D_MODEL, N_EXPERTS), full(1, N_EXPERTS), full(1, LANES),
        ],
        out_specs=[tok(D_MODEL), tok(D_MODEL), tok(LANES), full(1, LANES)],
        out_shape=[
            jax.ShapeDtypeStruct((G, T, D_MODEL), F32),
            jax.ShapeDtypeStruct((G, T, D_MODEL), F32),
            jax.ShapeDtypeStruct((G, T, LANES), F32),
            jax.ShapeDtypeStruct((1, LANES), F32),
        ],
        scratch_shapes=[pltpu.VMEM((1, LANES), F32)],
        compiler_params=_params(("arbitrary", "arbitrary")),
        name="merge_router",
    )(x, a, og, proj, proj, ga1, sc2, sh2, g2.reshape(1, -1), wa, wb, wo, wr, br.reshape(1, -1), cnt0)


MOE_ROWS = 256
DISP_T = 128


def _zero_expert_tails(pe_ref, xs_ref, zbuf, zsem):
    zbuf[...] = jnp.zeros_like(zbuf)

    def tail(e):
        start = pl.multiple_of(pe_ref[e + 1] - MOE_ROWS, MOE_ROWS)
        return pltpu.make_async_copy(zbuf, xs_ref.at[pl.ds(start, MOE_ROWS), :], zsem)

    for e in range(N_EXPERTS):
        @pl.when(pe_ref[e + 1] > pe_ref[e])
        def _():
            tail(e).start()

    for e in range(N_EXPERTS):
        @pl.when(pe_ref[e + 1] > pe_ref[e])
        def _():
            tail(e).wait()


def _dispatch_first_kernel(dest_ref, pe_ref, h_ref, xs_ref, sem, zbuf, zsem, *, tok_off):
    @pl.when(pl.program_id(0) == 0)
    def _():
        _zero_expert_tails(pe_ref, xs_ref, zbuf, zsem)

    _scatter_rows(dest_ref, h_ref, xs_ref, sem, tok_off)


def _dispatch_more_kernel(dest_ref, pe_ref, h_ref, xs_in_ref, xs_ref, sem, *, tok_off):
    del pe_ref, xs_in_ref
    _scatter_rows(dest_ref, h_ref, xs_ref, sem, tok_off)


def _scatter_rows(dest_ref, h_ref, xs_ref, sem, tok_off):
    tm = h_ref.shape[0]
    base = (tok_off + pl.program_id(0) * tm) * TOP_K

    def body(r, carry):
        for k in range(TOP_K):
            row = dest_ref[base + r * TOP_K + k]
            pltpu.make_async_copy(h_ref.at[pl.ds(r, 1), :], xs_ref.at[pl.ds(row, 1), :], sem).start()
        return carry

    lax.fori_loop(0, tm, body, 0)
    for k in range(TOP_K):
        pltpu.make_async_copy(h_ref, xs_ref.at[pl.ds(0, tm), :], sem).wait()


def _dispatch(h2, dest, pad_edges, tok_off, rows=None, xs=None):
    n = h2.shape[0]
    tm = min(DISP_T, n)
    assert n % tm == 0
    tile = pl.BlockSpec((tm, D_MODEL), lambda i, d, pe: (i, 0))
    hbm = pl.BlockSpec(memory_space=pl.ANY)
    if xs is None:
        return pl.pallas_call(
            functools.partial(_dispatch_first_kernel, tok_off=tok_off),
            grid_spec=pltpu.PrefetchScalarGridSpec(
                num_scalar_prefetch=2,
                grid=(n // tm,),
                in_specs=[tile],
                out_specs=hbm,
                scratch_shapes=[pltpu.SemaphoreType.DMA(()), pltpu.VMEM((MOE_ROWS, D_MODEL), F32),
                                pltpu.SemaphoreType.DMA(())],
            ),
            out_shape=jax.ShapeDtypeStruct((rows, D_MODEL), F32),
            compiler_params=_params(("arbitrary",)),
            name="dispatch_first",
        )(dest, pad_edges, h2)
    return pl.pallas_call(
        functools.partial(_dispatch_more_kernel, tok_off=tok_off),
        grid_spec=pltpu.PrefetchScalarGridSpec(
            num_scalar_prefetch=2,
            grid=(n // tm,),
            in_specs=[tile, hbm],
            out_specs=hbm,
            scratch_shapes=[pltpu.SemaphoreType.DMA(())],
        ),
        out_shape=jax.ShapeDtypeStruct(xs.shape, F32),
        input_output_aliases={3: 0},
        compiler_params=_params(("arbitrary",)),
        name="dispatch_more",
    )(dest, pad_edges, h2, xs)


def _moe_kernel(be_ref, nu_ref, x_ref, wgu_ref, bgu_ref, wdn_ref, bdn_ref, y_ref, wgu_bf, wdn_bf):
    i = pl.program_id(0)

    @pl.when(i < nu_ref[0])
    def _():
        @pl.when((i == 0) | (be_ref[i] != be_ref[jnp.maximum(i - 1, 0)]))
        def _():
            wgu_bf[...] = wgu_ref[0].astype(BF16)
            wdn_bf[...] = wdn_ref[0].astype(BF16)

        gu = jnp.dot(x_ref[...].astype(BF16), wgu_bf[...], preferred_element_type=F32) + bgu_ref[0]
        g = jnp.minimum(gu[:, :D_FF], SWIGLU_LIMIT)
        u = jnp.clip(gu[:, D_FF:], -SWIGLU_LIMIT, SWIGLU_LIMIT)
        act = ((u + 1.0) * g * _sigmoid(SWIGLU_ALPHA * g)).astype(BF16)
        y_ref[...] = jnp.dot(act, wdn_bf[...], preferred_element_type=F32) + bdn_ref[0]

    @pl.when(i >= nu_ref[0])
    def _():
        y_ref[...] = jnp.zeros_like(y_ref)


def _moe_experts(xs, blk_e, n_used, w_gu, b_gu, w_down, b_down):
    n_blocks = blk_e.shape[0]
    return pl.pallas_call(
        _moe_kernel,
        grid_spec=pltpu.PrefetchScalarGridSpec(
            num_scalar_prefetch=2,
            grid=(n_blocks,),
            in_specs=[
                pl.BlockSpec((MOE_ROWS, D_MODEL), lambda i, be, nu: (i, 0)),
                pl.BlockSpec((1, D_MODEL, 2 * D_FF), lambda i, be, nu: (be[i], 0, 0)),
                pl.BlockSpec((1, 1, 2 * D_FF), lambda i, be, nu: (be[i], 0, 0)),
                pl.BlockSpec((1, D_FF, D_MODEL), lambda i, be, nu: (be[i], 0, 0)),
                pl.BlockSpec((1, 1, D_MODEL), lambda i, be, nu: (be[i], 0, 0)),
            ],
            out_specs=pl.BlockSpec((MOE_ROWS, D_MODEL), lambda i, be, nu: (i, 0)),
            scratch_shapes=[
                pltpu.VMEM((D_MODEL, 2 * D_FF), BF16),
                pltpu.VMEM((D_FF, D_MODEL), BF16),
            ],
        ),
        out_shape=jax.ShapeDtypeStruct((n_blocks * MOE_ROWS, D_MODEL), F32),
        compiler_params=_params(("arbitrary",)),
        name="moe_experts",
    )(blk_e, n_used, xs, w_gu, b_gu.reshape(N_EXPERTS, 1, -1), w_down, b_down.reshape(N_EXPERTS, 1, -1))


COMB_T = 128


def _combine_kernel(dest_ref, yb_ref, x1_ref, rt_ref, ga2_ref, gf_ref, y_ref, gbuf, sem, *, tok_off):
    i = pl.program_id(1)
    b = pl.program_id(0)
    n_i = pl.num_programs(1)
    step = b * n_i + i
    n_steps = pl.num_programs(0) * n_i
    tm = gbuf.shape[2]

    def start_gather(stp, slot):
        base = (tok_off + stp * tm) * TOP_K

        def body(r, carry):
            for k in range(TOP_K):
                row = dest_ref[base + r * TOP_K + k]
                pltpu.make_async_copy(yb_ref.at[pl.ds(row, 1), :], gbuf.at[slot, k, pl.ds(r, 1), :],
                                      sem.at[slot]).start()
            return carry

        lax.fori_loop(0, tm, body, 0)

    @pl.when(step == 0)
    def _():
        start_gather(0, 0)

    slot = step % 2

    @pl.when(step + 1 < n_steps)
    def _():
        start_gather(step + 1, 1 - slot)

    for k in range(TOP_K):
        pltpu.make_async_copy(yb_ref.at[pl.ds(0, tm), :], gbuf.at[slot, k], sem.at[slot]).wait()
    rec = rt_ref[0]
    moe = gbuf[slot, 0] * rec[:, ROUTE_W:ROUTE_W + 1]
    for k in range(1, TOP_K):
        moe = moe + gbuf[slot, k] * rec[:, ROUTE_W + k:ROUTE_W + k + 1]
    y_ref[0] = _rmsnorm(x1_ref[0] + ga2_ref[0] * moe, gf_ref[...])


def _combine(yb, dest, x1, route, ga2, g_final, tok_off):
    G, T, _ = x1.shape
    R = ga2.shape[1]
    tm = min(COMB_T, T)
    rm = 1 if R == 1 else tm
    mod_map = (lambda b, i, d: (b, 0, 0)) if R == 1 else (lambda b, i, d: (b, i, 0))
    return pl.pallas_call(
        functools.partial(_combine_kernel, tok_off=tok_off),
        grid_spec=pltpu.PrefetchScalarGridSpec(
            num_scalar_prefetch=1,
            grid=(G, T // tm),
            in_specs=[
                pl.BlockSpec(memory_space=pl.ANY),
                pl.BlockSpec((1, tm, D_MODEL), lambda b, i, d: (b, i, 0)),
                pl.BlockSpec((1, tm, LANES), lambda b, i, d: (b, i, 0)),
                pl.BlockSpec((1, rm, D_MODEL), mod_map),
                pl.BlockSpec((1, D_MODEL), lambda b, i, d: (0, 0)),
            ],
            out_specs=pl.BlockSpec((1, tm, D_MODEL), lambda b, i, d: (b, i, 0)),
            scratch_shapes=[
                pltpu.VMEM((2, TOP_K, tm, D_MODEL), F32),
                pltpu.SemaphoreType.DMA((2,)),
            ],
        ),
        out_shape=jax.ShapeDtypeStruct((G, T, D_MODEL), F32),
        compiler_params=_params(("arbitrary", "arbitrary")),
        name="combine",
    )(dest, yb, x1, route, ga2, g_final.reshape(1, -1))


def _routing(route, counts):
    n = route.shape[0]
    na = n * TOP_K
    topi = route[:, ROUTE_E:ROUTE_E + TOP_K].astype(jnp.int32)
    rank = route[:, ROUTE_R:ROUTE_R + TOP_K].astype(jnp.int32)
    cnt = counts.astype(jnp.int32)
    padded = (cnt + MOE_ROWS - 1) // MOE_ROWS * MOE_ROWS
    pad_end = jnp.cumsum(padded)
    pad_start = pad_end - padded
    onehot = topi[..., None] == jnp.arange(N_EXPERTS, dtype=jnp.int32)
    dest = (jnp.sum(jnp.where(onehot, pad_start, 0), axis=-1) + rank).reshape(na)
    n_blocks = -(-na // MOE_ROWS) + N_EXPERTS
    blk_start = jnp.arange(n_blocks, dtype=jnp.int32) * MOE_ROWS
    blk_e = jnp.minimum(jnp.sum(blk_start[:, None] >= pad_end[None, :], axis=1), N_EXPERTS - 1).astype(jnp.int32)
    n_used = (pad_end[-1] // MOE_ROWS).astype(jnp.int32).reshape(1)
    pad_edges = jnp.concatenate([jnp.zeros((1,), jnp.int32), pad_end.astype(jnp.int32)])
    return dest, blk_e, n_used, pad_edges


def _mods(mod):
    return [m[:, None, :] for m in jnp.split(mod, 6, axis=-1)]


def _layer(xp, xs, cp, cs, cache_k, cache_v, state, page_table, w_ada, b_ada, g_norm_mix, w_in, w_gk2, b_gk2,
           g_gla_norm, w_br_a, w_br_b, w_out, g_norm_ffn, w_router, b_router, w_gu, b_gu, w_down, b_down,
           g_final):
    B, T, _ = xp.shape
    S = xs.shape[0]
    c_all = jnp.concatenate([cp, cs], axis=0)
    pad = -c_all.shape[0] % 8
    mod = _ada_mod(jnp.pad(c_all, ((0, pad), (0, 0))), w_ada, b_ada)
    sh1p, sc1p, ga1p, sh2p, sc2p, ga2p = _mods(mod[:B])
    sh1s, sc1s, ga1s, sh2s, sc2s, ga2s = [m[None] for m in jnp.split(mod[B:B + S], 6, axis=-1)]

    w_perm = jnp.concatenate(
        [w_in[:, :COL_ZA], w_in[:, COL_ZA + G_LOWRANK:], w_in[:, COL_ZA:COL_ZA + G_LOWRANK],
         jnp.zeros((D_MODEL, PROJ_W - COL_LR - G_LOWRANK), F32)], axis=1).astype(BF16)
    wa, wb, wo = w_br_a.astype(BF16), w_br_b.astype(BF16), w_out.astype(BF16)

    xs3 = xs.reshape(1, S, D_MODEL)
    proj_p = _inproj(xp, sc1p, sh1p, g_norm_mix, w_perm, tm=1024)
    proj_s = _inproj(xs3, sc1s, sh1s, g_norm_mix, w_perm, tm=S)

    a_p = _moba_prompt(proj_p)
    og_p, s_p = _gla_prompt(proj_p, w_gk2, b_gk2, g_gla_norm)
    ps = proj_s[0]
    k_s = ps[:, COL_AK:COL_AK + A_KVW]
    v_s = ps[:, COL_AV:COL_AV + A_KVW]
    a_s = _moba_sample(ps[:, :A_QW], k_s, v_s, cache_k, cache_v, page_table)
    og_s, s_s = _gla_sample(proj_s.reshape(S, 1, PROJ_W), state, w_gk2, b_gk2, g_gla_norm)

    cnt0 = jnp.zeros((1, LANES), F32)
    x1p, h2p, rtp, cnt1 = _merge_router(xp, a_p, og_p, proj_p, ga1p, sc2p, sh2p, g_norm_ffn, wa, wb, wo,
                                        w_router, b_router, cnt0, tm=512)
    x1s, h2s, rts, cnt2 = _merge_router(xs3, a_s.astype(BF16)[None], og_s.reshape(1, S, -1).astype(BF16), proj_s,
                                        ga1s, sc2s, sh2s, g_norm_ffn, wa, wb, wo, w_router, b_router, cnt1, tm=S)

    n_p = B * T
    route = jnp.concatenate([rtp.reshape(n_p, LANES), rts.reshape(S, LANES)], axis=0)
    dest, blk_e, n_used, pad_edges = _routing(route, cnt2[0, :N_EXPERTS])
    xsort = _dispatch(h2p.reshape(n_p, D_MODEL), dest, pad_edges, 0, rows=blk_e.shape[0] * MOE_ROWS)
    xsort = _dispatch(h2s.reshape(S, D_MODEL), dest, pad_edges, n_p, xs=xsort)
    yb = _moe_experts(xsort, blk_e, n_used, w_gu, b_gu, w_down, b_down)
    y_p = _combine(yb, dest, x1p, rtp, ga2p, g_final, tok_off=0)
    y_s = _combine(yb, dest, x1s, rts, ga2s, g_final, tok_off=n_p)

    kp = proj_p[:, :, COL_AK:COL_AK + A_KVW].reshape(B, T, A_KV_HEADS, A_HEAD_DIM)
    vp = proj_p[:, :, COL_AV:COL_AV + A_KVW].reshape(B, T, A_KV_HEADS, A_HEAD_DIM)
    ks = k_s.reshape(S, 1, A_KV_HEADS, A_HEAD_DIM)
    vs = v_s.reshape(S, 1, A_KV_HEADS, A_HEAD_DIM)
    return y_p, y_s.reshape(S, 1, D_MODEL), kp, vp, s_p, ks, vs, s_s


def kernel(x_prompt, x_sample, c_prompt, c_sample, cache_k, cache_v, state_gla, page_table, w_ada, b_ada, g_norm_mix, w_in, w_gk2, b_gk2, g_gla_norm, w_br_a, w_br_b, w_out, g_norm_ffn, w_router, b_router, w_gu, b_gu, w_down, b_down, g_final):
    assert w_ada.shape[0] == 1, "single-layer step"
    outs = _layer(x_prompt, x_sample.reshape(x_sample.shape[0], D_MODEL), c_prompt, c_sample, cache_k[0],
                  cache_v[0], state_gla[0], page_table, w_ada[0], b_ada[0], g_norm_mix[0], w_in[0], w_gk2[0],
                  b_gk2[0], g_gla_norm[0], w_br_a[0], w_br_b[0], w_out[0], g_norm_ffn[0], w_router[0],
                  b_router[0], w_gu[0], b_gu[0], w_down[0], b_down[0], g_final)
    y_p, y_s, kp, vp, s_p, ks, vs, s_s = outs
    return (y_p, y_s, kp[None], vp[None], s_p[None], ks[None], vs[None], s_s[None])
```

```python
import functools

import jax
import jax.numpy as jnp
from jax import lax
from jax.experimental import pallas as pl
from jax.experimental.pallas import tpu as pltpu

F32 = jnp.float32
BF16 = jnp.bfloat16

D_MODEL = 1024
A_HEADS = 8
A_KV_HEADS = 4
A_GROUP = A_HEADS // A_KV_HEADS
A_HEAD_DIM = 64
A_QW = A_HEADS * A_HEAD_DIM
A_KVW = A_KV_HEADS * A_HEAD_DIM
A_BLOCK = 256
A_TOPK = 3
PAGE_SIZE = 128
G_HEADS = 4
G_KEY_DIM = D_MODEL // 2
G_VAL_DIM = D_MODEL
G_DK = G_KEY_DIM // G_HEADS
G_DV = G_VAL_DIM // G_HEADS
G_LOWRANK = 16
G_NORMALIZER = 16.0
G_CHUNK = 64
N_EXPERTS = 32
TOP_K = 4
D_FF = D_MODEL
SWIGLU_LIMIT = 7.0
SWIGLU_ALPHA = 1.702
EPS = 1e-6

LANES = 128
COL_AQ = 0
COL_AK = COL_AQ + A_QW
COL_AV = COL_AK + A_KVW
COL_GQ = COL_AV + A_KVW
COL_GK = COL_GQ + G_KEY_DIM
COL_GV = COL_GK + G_KEY_DIM
COL_GG = COL_GV + G_VAL_DIM
COL_ZA = COL_GG + G_VAL_DIM
COL_ZB = COL_ZA + D_MODEL
COL_LR = COL_ZB + D_MODEL
PROJ_TN = 1280
PROJ_W = 5 * PROJ_TN
assert PROJ_W >= COL_LR + LANES
NEG_BIG = -1e30
VMEM_LIMIT = 56 * 1024 * 1024

NT_DIMS = (((1,), (1,)), ((), ()))
NN_DIMS = (((1,), (0,)), ((), ()))


def _params(sem):
    return pltpu.CompilerParams(dimension_semantics=sem, vmem_limit_bytes=VMEM_LIMIT)


def _split(a):
    hi = a.astype(BF16)
    lo = (a - hi.astype(F32)).astype(BF16)
    return hi, lo


def _dot3(a, b, dims=NN_DIMS):
    ah, al = _split(a)
    bh, bl = _split(b)
    d = lambda x, y: lax.dot_general(x, y, dims, preferred_element_type=F32)
    return d(ah, bh) + d(al, bh) + d(ah, bl)


def _sigmoid(x):
    return 1.0 / (1.0 + jnp.exp(-x))


def _log_sigmoid(x):
    return -(jnp.maximum(-x, 0.0) + jnp.log(1.0 + jnp.exp(-jnp.abs(x))))


def _rmsnorm(x, g):
    return x * lax.rsqrt(jnp.mean(x * x, axis=-1, keepdims=True) + EPS) * g


def _top_cols(score, colf, n_pick):
    vals, cols = [], []
    big = float(score.shape[1])
    for _ in range(n_pick):
        mx = jnp.max(score, axis=1, keepdims=True)
        first = jnp.min(jnp.where(score == mx, colf, big), axis=1, keepdims=True)
        vals.append(mx)
        cols.append(first)
        score = jnp.where(colf == first, -jnp.inf, score)
    return vals, cols


def _ada_kernel(c_ref, w_ref, b_ref, o_ref):
    c = c_ref[...]
    o_ref[...] = _dot3(c * _sigmoid(c), w_ref[...]) + b_ref[...]


def _ada_mod(c, w_ada, b_ada):
    rows = c.shape[0]
    tn = 1536
    return pl.pallas_call(
        _ada_kernel,
        grid=(6 * D_MODEL // tn,),
        in_specs=[
            pl.BlockSpec((rows, D_MODEL), lambda j: (0, 0)),
            pl.BlockSpec((D_MODEL, tn), lambda j: (0, j)),
            pl.BlockSpec((1, tn), lambda j: (0, j)),
        ],
        out_specs=pl.BlockSpec((rows, tn), lambda j: (0, j)),
        out_shape=jax.ShapeDtypeStruct((rows, 6 * D_MODEL), F32),
        compiler_params=_params(("arbitrary",)),
        name="ada_mod",
    )(c, w_ada, b_ada.reshape(1, -1))


def _inproj_kernel(x_ref, sc_ref, sh_ref, g_ref, w_ref, o_ref, h_ref):
    @pl.when(pl.program_id(2) == 0)
    def _():
        y = _rmsnorm(x_ref[0], g_ref[...])
        h_ref[...] = (y * (1.0 + sc_ref[0]) + sh_ref[0]).astype(BF16)

    o_ref[0] = jnp.dot(h_ref[...], w_ref[...], preferred_element_type=F32)


def _inproj(x, sc, sh, g, w_perm, tm):
    G, T, _ = x.shape
    R = sc.shape[1]
    rm = 1 if R == 1 else tm
    mod_map = (lambda b, i, j: (b, 0, 0)) if R == 1 else (lambda b, i, j: (b, i, 0))
    return pl.pallas_call(
        _inproj_kernel,
        grid=(G, T // tm, PROJ_W // PROJ_TN),
        in_specs=[
            pl.BlockSpec((1, tm, D_MODEL), lambda b, i, j: (b, i, 0)),
            pl.BlockSpec((1, rm, D_MODEL), mod_map),
            pl.BlockSpec((1, rm, D_MODEL), mod_map),
            pl.BlockSpec((1, D_MODEL), lambda b, i, j: (0, 0)),
            pl.BlockSpec((D_MODEL, PROJ_TN), lambda b, i, j: (0, j)),
        ],
        out_specs=pl.BlockSpec((1, tm, PROJ_TN), lambda b, i, j: (b, i, j)),
        out_shape=jax.ShapeDtypeStruct((G, T, PROJ_W), F32),
        scratch_shapes=[pltpu.VMEM((tm, D_MODEL), BF16)],
        compiler_params=_params(("arbitrary", "arbitrary", "arbitrary")),
        name="inproj",
    )(x, sc, sh, g.reshape(1, -1), w_perm)


PREP_T = 2048
AUG_NB = 32
AUG_REST = LANES - A_HEAD_DIM - AUG_NB
KV_GROUP = 4


def _moba_prep_kernel(k_ref, v_ref, ka_ref, vt_ref, km_ref):
    t = pl.program_id(1)
    k = k_ref[0]
    v = v_ref[0]
    nblk = PREP_T // A_BLOCK
    km_ref[0] = jnp.mean(k.reshape(nblk, A_BLOCK, A_KVW), axis=1)
    half = LANES - A_HEAD_DIM
    row = lax.broadcasted_iota(jnp.int32, (A_BLOCK, half), 0)
    col = lax.broadcasted_iota(jnp.int32, (A_BLOCK, half), 1)
    vextra = jnp.where(col == 0, 1.0, 0.0)
    for jb in range(nblk):
        blk = t * nblk + jb
        kextra = jnp.where(col == blk, 1.0, 0.0)
        kextra = jnp.where((col == AUG_NB) | (col == AUG_NB + 1), 1.0, kextra)
        kextra = jnp.where(col == AUG_NB + 2, row.astype(F32), kextra)
        kextra = jnp.where(col == AUG_NB + 3, (blk * A_BLOCK).astype(F32), kextra)
        rows = slice(jb * A_BLOCK, (jb + 1) * A_BLOCK)
        for h in range(A_KV_HEADS):
            cols = slice(h * A_HEAD_DIM, (h + 1) * A_HEAD_DIM)
            ka_ref[0, h, jb] = jnp.concatenate([k[rows, cols], kextra], axis=1).astype(BF16)
            vt_ref[0, h, jb] = jnp.concatenate([v[rows, cols], vextra], axis=1).T.astype(BF16)


def _moba_prep(proj):
    B, T, _ = proj.shape
    nblk = PREP_T // A_BLOCK
    nb = T // A_BLOCK
    assert nb <= AUG_NB
    return pl.pallas_call(
        _moba_prep_kernel,
        grid=(B, T // PREP_T),
        in_specs=[
            pl.BlockSpec((1, PREP_T, A_KVW), lambda b, t: (b, t, COL_AK // A_KVW)),
            pl.BlockSpec((1, PREP_T, A_KVW), lambda b, t: (b, t, COL_AV // A_KVW)),
        ],
        out_specs=[
            pl.BlockSpec((1, A_KV_HEADS, nblk, A_BLOCK, LANES), lambda b, t: (b, 0, t, 0, 0)),
            pl.BlockSpec((1, A_KV_HEADS, nblk, LANES, A_BLOCK), lambda b, t: (b, 0, t, 0, 0)),
            pl.BlockSpec((1, nblk, A_KVW), lambda b, t: (b, t, 0)),
        ],
        out_shape=[
            jax.ShapeDtypeStruct((B, A_KV_HEADS, nb, A_BLOCK, LANES), BF16),
            jax.ShapeDtypeStruct((B, A_KV_HEADS, nb, LANES, A_BLOCK), BF16),
            jax.ShapeDtypeStruct((B, nb, A_KVW), F32),
        ],
        compiler_params=_params(("arbitrary", "arbitrary")),
        name="moba_prep",
    )(proj, proj)


def _slope_of(head_plus_one, shape):
    bits = (127 - head_plus_one) << 23
    return lax.bitcast_convert_type(jnp.full(shape, bits, jnp.int32), F32)


def _top_rows(score, rowf, n_pick, n_valid):
    picked = jnp.zeros(score.shape, jnp.bool_)
    big = float(score.shape[0])
    for r in range(n_pick):
        mx = jnp.max(score, axis=0, keepdims=True)
        first = jnp.min(jnp.where(score == mx, rowf, big), axis=0, keepdims=True)
        hit = rowf == first
        picked = picked | (hit & ((jnp.zeros_like(rowf) + n_valid) > r))
        score = jnp.where(hit, -jnp.inf, score)
    return picked


def _moba_attn_kernel(q_ref, km_ref, ka_ref, vt_ref, o_ref, qa_ref, qo_ref, m_ref, acc_ref, sa_ref, sb_ref):
    kvh = pl.program_id(1)
    i = pl.program_id(2)
    nb = km_ref.shape[2]
    qt = q_ref[0].T
    km = km_ref[0, 0]
    blk_f = i.astype(F32)
    rowf = lax.broadcasted_iota(jnp.int32, (nb, A_BLOCK), 0).astype(F32)
    erow = lax.broadcasted_iota(jnp.int32, (AUG_REST, A_BLOCK), 0)
    qq = lax.broadcasted_iota(jnp.int32, (AUG_REST, A_BLOCK), 1).astype(F32)
    for hh in range(A_GROUP):
        qh = qt[hh * A_HEAD_DIM:(hh + 1) * A_HEAD_DIM, :]
        gate = _dot3(km, qh)
        gate = jnp.where(rowf < blk_f, gate, -jnp.inf)
        picked = _top_rows(gate, rowf, A_TOPK, blk_f)
        selbias = jnp.where(picked, 0.0, NEG_BIG)
        if nb < AUG_NB:
            selbias = jnp.concatenate([selbias, jnp.zeros((AUG_NB - nb, A_BLOCK), F32)], axis=0)
        slope = _slope_of(kvh * A_GROUP + hh + 1, (AUG_REST, A_BLOCK))
        alibi = jnp.where(erow == 0, -slope * qq, 0.0)
        alibi = jnp.where(erow == 1, -slope * (blk_f * A_BLOCK), alibi)
        alibi = jnp.where((erow == 2) | (erow == 3), slope, alibi)
        qs = qh * (A_HEAD_DIM ** -0.5)
        qa_ref[hh] = jnp.concatenate([qs, selbias, alibi], axis=0).astype(BF16)
        qo_ref[hh] = jnp.concatenate([qs, jnp.zeros((AUG_NB, A_BLOCK), F32), alibi], axis=0).astype(BF16)

    for j in range(KV_GROUP):
        for hh in range(A_GROUP):
            sa_ref[hh, j] = jnp.dot(ka_ref[0, 0, j], qa_ref[hh], preferred_element_type=F32)

    kb = ka_ref[0, 0, i]
    vb = vt_ref[0, 0, i]
    key = lax.broadcasted_iota(jnp.int32, (A_BLOCK, A_BLOCK), 0)
    qry = lax.broadcasted_iota(jnp.int32, (A_BLOCK, A_BLOCK), 1)
    for hh in range(A_GROUP):
        s = jnp.dot(kb, qo_ref[hh], preferred_element_type=F32)
        s = jnp.where(key <= qry, s, NEG_BIG)
        m = jnp.max(s, axis=0, keepdims=True)
        p = jnp.exp(s - m)
        m_ref[hh] = m
        acc_ref[hh] = jnp.dot(vb, p.astype(BF16), preferred_element_type=F32)

    n_groups = (i + KV_GROUP - 1) // KV_GROUP

    def scores(g, s_ref):
        for j in range(KV_GROUP):
            kj = ka_ref[0, 0, g * KV_GROUP + j]
            for hh in range(A_GROUP):
                s_ref[hh, j] = jnp.dot(kj, qa_ref[hh], preferred_element_type=F32)

    def accumulate(g, s_ref):
        for hh in range(A_GROUP):
            m_old = m_ref[hh]
            m_new = m_old
            for j in range(KV_GROUP):
                m_new = jnp.maximum(m_new, jnp.max(s_ref[hh, j], axis=0, keepdims=True))
            acc = jnp.exp(m_old - m_new) * acc_ref[hh]
            for j in range(KV_GROUP):
                p = jnp.exp(s_ref[hh, j] - m_new).astype(BF16)
                acc = acc + jnp.dot(vt_ref[0, 0, g * KV_GROUP + j], p, preferred_element_type=F32)
            acc_ref[hh] = acc
            m_ref[hh] = m_new

    def on_parity(g, fn):
        @pl.when(g % 2 == 0)
        def _():
            fn(sa_ref, sb_ref)

        @pl.when(g % 2 == 1)
        def _():
            fn(sb_ref, sa_ref)

    @pl.when(n_groups > 0)
    def _():
        def body(g, carry):
            def step(cur, nxt):
                scores(g + 1, nxt)
                accumulate(g, cur)

            on_parity(g, step)
            return carry

        lax.fori_loop(0, n_groups - 1, body, 0)
        on_parity(n_groups - 1, lambda cur, nxt: accumulate(n_groups - 1, cur))

    outs = []
    for hh in range(A_GROUP):
        acc = acc_ref[hh]
        outs.append((acc[:A_HEAD_DIM] / acc[A_HEAD_DIM:A_HEAD_DIM + 1]).T)
    o_ref[0] = jnp.concatenate(outs, axis=1).astype(BF16)


def _moba_prompt(proj):
    B, T, _ = proj.shape
    nb = T // A_BLOCK
    assert nb % KV_GROUP == 0
    kaug, vaug_t, km = _moba_prep(proj)
    km = km.reshape(B, nb, A_KV_HEADS, A_HEAD_DIM).transpose(0, 2, 1, 3)
    qw = A_GROUP * A_HEAD_DIM
    return pl.pallas_call(
        _moba_attn_kernel,
        grid=(B, A_KV_HEADS, nb),
        in_specs=[
            pl.BlockSpec((1, A_BLOCK, qw), lambda b, h, i: (b, i, h)),
            pl.BlockSpec((1, 1, nb, A_HEAD_DIM), lambda b, h, i: (b, h, 0, 0)),
            pl.BlockSpec((1, 1, nb, A_BLOCK, LANES), lambda b, h, i: (b, h, 0, 0, 0)),
            pl.BlockSpec((1, 1, nb, LANES, A_BLOCK), lambda b, h, i: (b, h, 0, 0, 0)),
        ],
        out_specs=pl.BlockSpec((1, A_BLOCK, qw), lambda b, h, i: (b, i, h)),
        out_shape=jax.ShapeDtypeStruct((B, T, A_QW), BF16),
        scratch_shapes=[
            pltpu.VMEM((A_GROUP, LANES, A_BLOCK), BF16),
            pltpu.VMEM((A_GROUP, LANES, A_BLOCK), BF16),
            pltpu.VMEM((A_GROUP, 1, A_BLOCK), F32),
            pltpu.VMEM((A_GROUP, LANES, A_BLOCK), F32),
            pltpu.VMEM((A_GROUP, KV_GROUP, A_BLOCK, A_BLOCK), F32),
            pltpu.VMEM((A_GROUP, KV_GROUP, A_BLOCK, A_BLOCK), F32),
        ],
        compiler_params=_params(("arbitrary", "arbitrary", "arbitrary")),
        name="moba_attn",
    )(proj, km, kaug, vaug_t)


GLA_T = 1024
GLA_UNROLL = 8


def _gla_kernel(q_ref, k_ref, v_ref, gg_ref, lr_ref, w2_ref, b2_ref, gn_ref, og_ref, st_ref, s_ref, g_ref):
    t = pl.program_id(2)

    @pl.when(t == 0)
    def _():
        s_ref[...] = jnp.zeros_like(s_ref)

    z = _dot3(lr_ref[0][:, :G_LOWRANK], w2_ref[...]) + b2_ref[...]
    g_ref[...] = _log_sigmoid(z) * (1.0 / G_NORMALIZER)
    C = G_CHUNK
    rr = lax.broadcasted_iota(jnp.int32, (C, C), 0)
    cc = lax.broadcasted_iota(jnp.int32, (C, C), 1)
    causal = cc <= rr
    tril = jnp.where(causal, 1.0, 0.0).astype(BF16)
    er = lax.broadcasted_iota(jnp.int32, (G_DK, G_DK), 0)
    ec = lax.broadcasted_iota(jnp.int32, (G_DK, G_DK), 1)
    eye = er == ec
    gn = gn_ref[...]

    def group(gi, s):
        sls = [pl.ds(pl.multiple_of((gi * GLA_UNROLL + c) * C, C), C) for c in range(GLA_UNROLL)]
        cums = []
        for sl in sls:
            gh, gl = _split(g_ref[sl, :])
            cums.append(jnp.dot(tril, gh, preferred_element_type=F32) + jnp.dot(tril, gl, preferred_element_type=F32))
        parts = []
        for sl, cum in zip(sls, cums):
            last = cum[C - 1:C, :]
            kc = k_ref[0, sl, :]
            vc = v_ref[0, sl, :].astype(BF16)
            qe = (q_ref[0, sl, :] * (G_DK ** -0.5) * jnp.exp(cum)).astype(BF16)
            ke = (kc * jnp.exp(-cum)).astype(BF16)
            kd = kc * jnp.exp(last - cum)
            att = lax.dot_general(qe, ke, NT_DIMS, preferred_element_type=F32)
            att = jnp.where(causal, att, 0.0).astype(BF16)
            intra = jnp.dot(att, vc, preferred_element_type=F32)
            inc = jnp.dot(kd.T.astype(BF16), vc, preferred_element_type=F32)
            decay = jnp.sum(jnp.where(eye, jnp.exp(last), 0.0), axis=1, keepdims=True)
            parts.append((qe, intra, inc, decay))
        for sl, (qe, intra, inc, decay) in zip(sls, parts):
            o = jnp.dot(qe, s.astype(BF16), preferred_element_type=F32) + intra
            s = s * decay + inc
            on = _rmsnorm(o, gn)
            gate = gg_ref[0, sl, :]
            og_ref[0, sl, :] = (on * (gate * _sigmoid(gate))).astype(BF16)
        return s

    s_ref[...] = lax.fori_loop(0, GLA_T // (C * GLA_UNROLL), group, s_ref[...])

    @pl.when(t == pl.num_programs(2) - 1)
    def _():
        st_ref[0, 0] = s_ref[...]


def _gla_prompt(proj, w_gk2, b_gk2, g_gla_norm):
    B, T, _ = proj.shape
    tt = min(GLA_T, T)
    assert tt == GLA_T
    return pl.pallas_call(
        _gla_kernel,
        grid=(B, G_HEADS, T // tt),
        in_specs=[
            pl.BlockSpec((1, tt, G_DK), lambda b, h, t: (b, t, COL_GQ // G_DK + h)),
            pl.BlockSpec((1, tt, G_DK), lambda b, h, t: (b, t, COL_GK // G_DK + h)),
            pl.BlockSpec((1, tt, G_DV), lambda b, h, t: (b, t, COL_GV // G_DV + h)),
            pl.BlockSpec((1, tt, G_DV), lambda b, h, t: (b, t, COL_GG // G_DV + h)),
            pl.BlockSpec((1, tt, LANES), lambda b, h, t: (b, t, COL_LR // LANES)),
            pl.BlockSpec((G_LOWRANK, G_DK), lambda b, h, t: (0, h)),
            pl.BlockSpec((1, G_DK), lambda b, h, t: (0, h)),
            pl.BlockSpec((1, G_DV), lambda b, h, t: (0, 0)),
        ],
        out_specs=[
            pl.BlockSpec((1, tt, G_DV), lambda b, h, t: (b, t, h)),
            pl.BlockSpec((1, 1, G_DK, G_DV), lambda b, h, t: (b, h, 0, 0)),
        ],
        out_shape=[
            jax.ShapeDtypeStruct((B, T, G_VAL_DIM), BF16),
            jax.ShapeDtypeStruct((B, G_HEADS, G_DK, G_DV), F32),
        ],
        scratch_shapes=[pltpu.VMEM((G_DK, G_DV), F32), pltpu.VMEM((tt, G_DK), F32)],
        compiler_params=_params(("arbitrary", "arbitrary", "arbitrary")),
        name="gla_prompt",
    )(proj, proj, proj, proj, proj, w_gk2, b_gk2.reshape(1, -1), g_gla_norm.reshape(1, -1))


PPB = A_BLOCK // PAGE_SIZE


def _smoba_kernel(pt_ref, qb_ref, q8_ref, kn_ref, vn_ref, ck_ref, cv_ref, o_ref,
                  kbuf, vbuf, lg_ref, p_ref, pown_ref, idx_v, idx_s, ksem, vsem, isem):
    s = pl.program_id(0)
    n_seq = pl.num_programs(0) - 1
    n_pages = kbuf.shape[1]
    nb = n_pages // PPB
    past = n_pages * PAGE_SIZE
    hd = A_HEAD_DIM

    def start_keys(seq, slot):
        for p in range(n_pages):
            pltpu.make_async_copy(ck_ref.at[pt_ref[seq, p]], kbuf.at[slot, p], ksem.at[slot]).start()

    def value_copy(seq, slot, h, r, e):
        page = pt_ref[seq, idx_s[slot, h, r] * PPB + e]
        return pltpu.make_async_copy(cv_ref.at[page, pl.ds((h // A_GROUP) * hd, hd), :],
                                     vbuf.at[slot, h, r, e], vsem.at[slot])

    @pl.when(s == 0)
    def _():
        start_keys(0, 0)

    @pl.when(s + 1 < n_seq)
    def _():
        start_keys(s + 1, (s + 1) % 2)

    @pl.when(s < n_seq)
    def _():
        slot = s % 2
        pltpu.make_async_copy(ck_ref.at[pl.ds(0, n_pages)], kbuf.at[slot], ksem.at[slot]).wait()

        def page_logits(p, carry):
            kp = kbuf[slot, p]
            rows = []
            for h in range(A_HEADS):
                g = h // A_GROUP
                rows.append(jnp.sum(qb_ref[0, h] * kp[g * hd:(g + 1) * hd, :], axis=0, keepdims=True))
            lg_ref[slot, p] = jnp.concatenate(rows, axis=0)
            return carry

        lax.fori_loop(0, n_pages, page_logits, 0)
        lg = lg_ref[slot]
        gate = jnp.sum(jnp.sum(lg.reshape(nb, PPB, A_HEADS, PAGE_SIZE), axis=1), axis=2, keepdims=True)
        bidx = lax.broadcasted_iota(jnp.int32, (nb, A_HEADS, 1), 0).astype(F32)
        picked = jnp.zeros((nb, A_HEADS, 1), F32)
        firsts = []
        for r in range(A_TOPK):
            mx = jnp.max(gate, axis=0, keepdims=True)
            first = jnp.min(jnp.where(gate == mx, bidx, float(nb)), axis=0, keepdims=True)
            hit = bidx == first
            picked = jnp.where(hit, 1.0, picked)
            gate = jnp.where(hit, -jnp.inf, gate)
            firsts.append(first.reshape(A_HEADS, 1))
        sel = jnp.broadcast_to(picked[:, None], (nb, PPB, A_HEADS, 1)).reshape(n_pages, A_HEADS, 1)
        head1 = lax.broadcasted_iota(jnp.int32, (1, A_HEADS, 1), 1) + 1
        slope = lax.bitcast_convert_type((127 - head1) << 23, F32)
        tpos = (lax.broadcasted_iota(jnp.int32, (n_pages, 1, PAGE_SIZE), 0) * PAGE_SIZE
                + lax.broadcasted_iota(jnp.int32, (n_pages, 1, PAGE_SIZE), 2))
        dist = (past - tpos).astype(F32)
        att = jnp.where(sel > 0.5, lg * (hd ** -0.5) - slope * dist, NEG_BIG)
        own = jnp.sum(q8_ref[0] * kn_ref[0], axis=1, keepdims=True) * (hd ** -0.5)
        m = jnp.maximum(jnp.max(jnp.max(att, axis=0), axis=1, keepdims=True), own)
        p = jnp.exp(att - m.reshape(1, A_HEADS, 1))
        p_own = jnp.exp(own - m)
        inv = 1.0 / (jnp.sum(jnp.sum(p, axis=0), axis=1, keepdims=True) + p_own)
        p_ref[slot] = p * inv.reshape(1, A_HEADS, 1)
        pown_ref[slot] = jnp.broadcast_to(p_own * inv, (A_HEADS, LANES))
        lane = lax.broadcasted_iota(jnp.int32, (A_HEADS, LANES), 1)
        it = jnp.zeros((A_HEADS, LANES), F32)
        for r in range(A_TOPK):
            it = jnp.where(lane == r, firsts[r], it)
        idx_v[...] = it.astype(jnp.int32)
        to_smem = pltpu.make_async_copy(idx_v, idx_s.at[slot], isem)
        to_smem.start()
        to_smem.wait()
        for h in range(A_HEADS):
            for r in range(A_TOPK):
                for e in range(PPB):
                    value_copy(s, slot, h, r, e).start()

    @pl.when(s >= 1)
    def _():
        sb = s - 1
        b = sb % 2
        for h in range(A_HEADS):
            for r in range(A_TOPK):
                for e in range(PPB):
                    value_copy(sb, b, h, r, e).wait()
        rowid = lax.broadcasted_iota(jnp.int32, (A_HEADS, hd), 0)
        out = pown_ref[b][:, :hd] * vn_ref[0]
        for h in range(A_HEADS):
            part = jnp.zeros((A_HEADS, hd), F32)
            for r in range(A_TOPK):
                for e in range(PPB):
                    pg = idx_s[b, h, r] * PPB + e
                    part = part + lax.dot_general(p_ref[b, pg].astype(BF16), vbuf[b, h, r, e].astype(BF16),
                                                  NT_DIMS, preferred_element_type=F32)
            out = out + jnp.where(rowid == h, part, 0.0)
        o_ref[0] = out


def _moba_sample(q, k_new, v_new, cache_k, cache_v, page_table):
    S = q.shape[0]
    n_pages = page_table.shape[1]
    n_pool = cache_k.shape[0]
    hd = A_HEAD_DIM
    ck = cache_k.transpose(0, 2, 3, 1).reshape(n_pool, A_KVW, PAGE_SIZE)
    cv = cache_v.transpose(0, 2, 3, 1).reshape(n_pool, A_KVW, PAGE_SIZE)
    q8 = q.reshape(S, A_HEADS, hd)
    qb = jnp.broadcast_to(q8[..., None], (S, A_HEADS, hd, LANES))
    kn8 = jnp.repeat(k_new.reshape(S, A_KV_HEADS, hd), A_GROUP, axis=1)
    vn8 = jnp.repeat(v_new.reshape(S, A_KV_HEADS, hd), A_GROUP, axis=1)
    seq = lambda s, pt: (jnp.minimum(s, S - 1), 0, 0)
    out = pl.pallas_call(
        _smoba_kernel,
        grid_spec=pltpu.PrefetchScalarGridSpec(
            num_scalar_prefetch=1,
            grid=(S + 1,),
            in_specs=[
                pl.BlockSpec((1, A_HEADS, hd, LANES), lambda s, pt: (jnp.minimum(s, S - 1), 0, 0, 0)),
                pl.BlockSpec((1, A_HEADS, hd), seq),
                pl.BlockSpec((1, A_HEADS, hd), seq),
                pl.BlockSpec((1, A_HEADS, hd), lambda s, pt: (jnp.maximum(s - 1, 0), 0, 0)),
                pl.BlockSpec(memory_space=pl.ANY),
                pl.BlockSpec(memory_space=pl.ANY),
            ],
            out_specs=pl.BlockSpec((1, A_HEADS, hd), lambda s, pt: (jnp.maximum(s - 1, 0), 0, 0)),
            scratch_shapes=[
                pltpu.VMEM((2, n_pages, A_KVW, PAGE_SIZE), F32),
                pltpu.VMEM((2, A_HEADS, A_TOPK, PPB, hd, PAGE_SIZE), F32),
                pltpu.VMEM((2, n_pages, A_HEADS, PAGE_SIZE), F32),
                pltpu.VMEM((2, n_pages, A_HEADS, PAGE_SIZE), F32),
                pltpu.VMEM((2, A_HEADS, LANES), F32),
                pltpu.VMEM((A_HEADS, LANES), jnp.int32),
                pltpu.SMEM((2, A_HEADS, LANES), jnp.int32),
                pltpu.SemaphoreType.DMA((2,)),
                pltpu.SemaphoreType.DMA((2,)),
                pltpu.SemaphoreType.DMA(()),
            ],
        ),
        out_shape=jax.ShapeDtypeStruct((S, A_HEADS, hd), F32),
        compiler_params=_params(("arbitrary",)),
        name="moba_sample",
    )(page_table, qb, q8, kn8, vn8, ck, cv)
    return out.reshape(S, A_QW)


def _sgla_kernel(p_ref, s0_ref, w2_ref, b2_ref, gn_ref, og_ref, s1_ref):
    row = p_ref[0]
    er = lax.broadcasted_iota(jnp.int32, (G_DK, G_DK), 0)
    ec = lax.broadcasted_iota(jnp.int32, (G_DK, G_DK), 1)
    eye = er == ec
    col_of = lambda r: jnp.sum(jnp.where(eye, r, 0.0), axis=1, keepdims=True)
    lr = row[:, COL_LR:COL_LR + G_LOWRANK]
    z = _dot3(lr, w2_ref[...]) + b2_ref[...]
    g_all = _log_sigmoid(z) * (1.0 / G_NORMALIZER)
    outs = []
    for h in range(G_HEADS):
        g = g_all[:, h * G_DK:(h + 1) * G_DK]
        q = row[:, COL_GQ + h * G_DK:COL_GQ + (h + 1) * G_DK] * (G_DK ** -0.5)
        k = row[:, COL_GK + h * G_DK:COL_GK + (h + 1) * G_DK]
        v = row[:, COL_GV + h * G_DV:COL_GV + (h + 1) * G_DV]
        gate = row[:, COL_GG + h * G_DV:COL_GG + (h + 1) * G_DV]
        qe = q * jnp.exp(g)
        ke = k * jnp.exp(-g)
        att = jnp.sum(qe * ke, axis=1, keepdims=True)
        s0 = s0_ref[0, h]
        o = _dot3(qe, s0) + att * v
        s1_ref[0, h] = s0 * col_of(jnp.exp(g)) + col_of(k) * v
        on = _rmsnorm(o, gn_ref[...])
        outs.append(on * (gate * _sigmoid(gate)))
    og_ref[0] = jnp.concatenate(outs, axis=1)


def _gla_sample(proj_rows, state, w_gk2, b_gk2, g_gla_norm):
    S = proj_rows.shape[0]
    return pl.pallas_call(
        _sgla_kernel,
        grid=(S,),
        in_specs=[
            pl.BlockSpec((1, 1, PROJ_W), lambda s: (s, 0, 0)),
            pl.BlockSpec((1, G_HEADS, G_DK, G_DV), lambda s: (s, 0, 0, 0)),
            pl.BlockSpec((G_LOWRANK, G_KEY_DIM), lambda s: (0, 0)),
            pl.BlockSpec((1, G_KEY_DIM), lambda s: (0, 0)),
            pl.BlockSpec((1, G_DV), lambda s: (0, 0)),
        ],
        out_specs=[
            pl.BlockSpec((1, 1, G_VAL_DIM), lambda s: (s, 0, 0)),
            pl.BlockSpec((1, G_HEADS, G_DK, G_DV), lambda s: (s, 0, 0, 0)),
        ],
        out_shape=[
            jax.ShapeDtypeStruct((S, 1, G_VAL_DIM), F32),
            jax.ShapeDtypeStruct((S, G_HEADS, G_DK, G_DV), F32),
        ],
        compiler_params=_params(("arbitrary",)),
        name="gla_sample",
    )(proj_rows, state, w_gk2, b_gk2.reshape(1, -1), g_gla_norm.reshape(1, -1))


ROUTE_E = 0
ROUTE_W = TOP_K
ROUTE_R = 2 * TOP_K


def _merge_kernel(x_ref, a_ref, og_ref, za_ref, zb_ref, ga1_ref, sc2_ref, sh2_ref, g2_ref, wa_ref, wb_ref,
                  wo_ref, wr_ref, br_ref, c0_ref, x1_ref, h2_ref, rt_ref, cnt_ref, run_ref):
    first = (pl.program_id(0) == 0) & (pl.program_id(1) == 0)

    @pl.when(first)
    def _():
        run_ref[...] = c0_ref[...]

    pa = jnp.dot(a_ref[0], wa_ref[...], preferred_element_type=F32)
    pb = jnp.dot(og_ref[0], wb_ref[...], preferred_element_type=F32)
    merged = _sigmoid(za_ref[0]) * pa + _sigmoid(zb_ref[0]) * pb
    mix = jnp.dot(merged.astype(BF16), wo_ref[...], preferred_element_type=F32)
    x1 = x_ref[0] + ga1_ref[0] * mix
    x1_ref[0] = x1
    h2 = _rmsnorm(x1, g2_ref[...]) * (1.0 + sc2_ref[0]) + sh2_ref[0]
    h2_ref[0] = h2
    logits = _dot3(h2, wr_ref[...]) + br_ref[...]
    tm = logits.shape[0]
    colf = lax.broadcasted_iota(jnp.int32, (tm, N_EXPERTS), 1).astype(F32)
    vals, cols = _top_cols(logits, colf, TOP_K)
    ex = [jnp.exp(v - vals[0]) for v in vals]
    tot = ex[0] + ex[1] + ex[2] + ex[3]
    onehot = [jnp.where(colf == c, 1.0, 0.0) for c in cols]
    osum = onehot[0] + onehot[1] + onehot[2] + onehot[3]
    tr = lax.broadcasted_iota(jnp.int32, (tm, tm), 0)
    tc = lax.broadcasted_iota(jnp.int32, (tm, tm), 1)
    lower = jnp.where(tc < tr, 1.0, 0.0).astype(BF16)
    before = jnp.dot(lower, osum.astype(BF16), preferred_element_type=F32) + run_ref[:, :N_EXPERTS]
    lane = lax.broadcasted_iota(jnp.int32, (tm, LANES), 1)
    rec = jnp.zeros((tm, LANES), F32)
    for r in range(TOP_K):
        rank = jnp.sum(onehot[r] * before, axis=1, keepdims=True)
        rec = jnp.where(lane == ROUTE_E + r, cols[r], rec)
        rec = jnp.where(lane == ROUTE_W + r, ex[r] / tot, rec)
        rec = jnp.where(lane == ROUTE_R + r, rank, rec)
    rt_ref[0] = rec
    pad = jnp.zeros((1, LANES - N_EXPERTS), F32)
    run_ref[...] = run_ref[...] + jnp.concatenate([jnp.sum(osum, axis=0, keepdims=True), pad], axis=1)
    cnt_ref[...] = run_ref[...]


def _merge_router(x, a, og, proj, ga1, sc2, sh2, g2, wa, wb, wo, wr, br, cnt0, tm):
    G, T, _ = x.shape
    R = ga1.shape[1]
    rm = 1 if R == 1 else tm
    mod_map = (lambda b, i: (b, 0, 0)) if R == 1 else (lambda b, i: (b, i, 0))
    tok = lambda w: pl.BlockSpec((1, tm, w), lambda b, i: (b, i, 0))
    full = lambda r, c: pl.BlockSpec((r, c), lambda b, i: (0, 0))
    return pl.pallas_call(
        _merge_kernel,
        grid=(G, T // tm),
        in_specs=[
            tok(D_MODEL), tok(A_QW), tok(G_VAL_DIM),
            pl.BlockSpec((1, tm, D_MODEL), lambda b, i: (b, i, COL_ZA // D_MODEL)),
            pl.BlockSpec((1, tm, D_MODEL), lambda b, i: (b, i, COL_ZB // D_MODEL)),
            pl.BlockSpec((1, rm, D_MODEL), mod_map),
            pl.BlockSpec((1, rm, D_MODEL), mod_map),
            pl.BlockSpec((1, rm, D_MODEL), mod_map),
            full(1, D_MODEL), full(A_QW, D_MODEL), full(G_VAL_DIM, D_MODEL), full(D_MODEL, D_MODEL),
            full(D_MODEL, N_EXPERTS), full(1, N_EXPERTS), full(1, LANES),
        ],
        out_specs=[tok(D_MODEL), tok(D_MODEL), tok(LANES), full(1, LANES)],
        out_shape=[
            jax.ShapeDtypeStruct((G, T, D_MODEL), F32),
            jax.ShapeDtypeStruct((G, T, D_MODEL), F32),
            jax.ShapeDtypeStruct((G, T, LANES), F32),
            jax.ShapeDtypeStruct((1, LANES), F32),
        ],
        scratch_shapes=[pltpu.VMEM((1, LANES), F32)],
        compiler_params=_params(("arbitrary", "arbitrary")),
        name="merge_router",
    )(x, a, og, proj, proj, ga1, sc2, sh2, g2.reshape(1, -1), wa, wb, wo, wr, br.reshape(1, -1), cnt0)


MOE_ROWS = 256
DISP_T = 128


def _zero_unwritten(pe_ref, nu_ref, xs_ref, zbuf, zsem):
    zbuf[...] = jnp.zeros_like(zbuf)
    n_blocks = xs_ref.shape[0] // MOE_ROWS

    def block_copy(start):
        return pltpu.make_async_copy(zbuf, xs_ref.at[pl.ds(pl.multiple_of(start, MOE_ROWS), MOE_ROWS), :], zsem)

    todo = [(pe_ref[e + 1] > pe_ref[e], pe_ref[e + 1] - MOE_ROWS) for e in range(N_EXPERTS)]
    todo += [(n_blocks - 1 - u >= nu_ref[0], (n_blocks - 1 - u) * MOE_ROWS) for u in range(N_EXPERTS)]
    for cond, start in todo:
        @pl.when(cond)
        def _():
            block_copy(start).start()

    for cond, start in todo:
        @pl.when(cond)
        def _():
            block_copy(start).wait()


def _scatter_rows(dest_ref, h_ref, xs_ref, sem, first_tok):
    tm = h_ref.shape[0]
    base = first_tok * TOP_K

    def body(r, carry):
        for k in range(TOP_K):
            row = dest_ref[base + r * TOP_K + k]
            pltpu.make_async_copy(h_ref.at[pl.ds(r, 1), :], xs_ref.at[pl.ds(row, 1), :], sem).start(
                priority=k % 2)
        return carry

    lax.fori_loop(0, tm, body, 0)
    for k in range(TOP_K):
        pltpu.make_async_copy(h_ref, xs_ref.at[pl.ds(0, tm), :], sem).wait()


def _dispatch_kernel(dest_ref, pe_ref, nu_ref, hp_ref, hs_ref, xs_ref, sem, zbuf, zsem, *, steps_p):
    i = pl.program_id(0)

    @pl.when(i == 0)
    def _():
        _zero_unwritten(pe_ref, nu_ref, xs_ref, zbuf, zsem)

    @pl.when(i < steps_p)
    def _():
        _scatter_rows(dest_ref, hp_ref, xs_ref, sem, i * hp_ref.shape[0])

    @pl.when(i >= steps_p)
    def _():
        _scatter_rows(dest_ref, hs_ref, xs_ref, sem, steps_p * hp_ref.shape[0] + (i - steps_p) * hs_ref.shape[0])


def _dispatch(h2p, h2s, dest, pad_edges, n_used, rows):
    n_p, n_s = h2p.shape[0], h2s.shape[0]
    tp, ts = min(DISP_T, n_p), min(DISP_T, n_s)
    assert n_p % tp == 0 and n_s % ts == 0 and rows // MOE_ROWS >= N_EXPERTS
    steps_p, steps_s = n_p // tp, n_s // ts
    return pl.pallas_call(
        functools.partial(_dispatch_kernel, steps_p=steps_p),
        grid_spec=pltpu.PrefetchScalarGridSpec(
            num_scalar_prefetch=3,
            grid=(steps_p + steps_s,),
            in_specs=[
                pl.BlockSpec((tp, D_MODEL), lambda i, d, pe, nu: (jnp.minimum(i, steps_p - 1), 0)),
                pl.BlockSpec((ts, D_MODEL), lambda i, d, pe, nu: (jnp.clip(i - steps_p, 0, steps_s - 1), 0)),
            ],
            out_specs=pl.BlockSpec(memory_space=pl.ANY),
            scratch_shapes=[pltpu.SemaphoreType.DMA(()), pltpu.VMEM((MOE_ROWS, D_MODEL), F32),
                            pltpu.SemaphoreType.DMA(())],
        ),
        out_shape=jax.ShapeDtypeStruct((rows, D_MODEL), F32),
        compiler_params=_params(("arbitrary",)),
        name="dispatch",
    )(dest, pad_edges, n_used, h2p, h2s)


def _moe_kernel(be_ref, nu_ref, x_ref, wgu_ref, bgu_ref, wdn_ref, bdn_ref, y_ref, wgu_bf, wdn_bf):
    i = pl.program_id(0)

    @pl.when(i < nu_ref[0])
    def _():
        @pl.when((i == 0) | (be_ref[i] != be_ref[jnp.maximum(i - 1, 0)]))
        def _():
            wgu_bf[...] = wgu_ref[0].astype(BF16)
            wdn_bf[...] = wdn_ref[0].astype(BF16)

        gu = jnp.dot(x_ref[...].astype(BF16), wgu_bf[...], preferred_element_type=F32) + bgu_ref[0]
        g = jnp.minimum(gu[:, :D_FF], SWIGLU_LIMIT)
        u = jnp.clip(gu[:, D_FF:], -SWIGLU_LIMIT, SWIGLU_LIMIT)
        act = ((u + 1.0) * g * _sigmoid(SWIGLU_ALPHA * g)).astype(BF16)
        y_ref[...] = jnp.dot(act, wdn_bf[...], preferred_element_type=F32) + bdn_ref[0]

    @pl.when(i >= nu_ref[0])
    def _():
        y_ref[...] = jnp.zeros_like(y_ref)


def _moe_experts(xs, blk_e, n_used, w_gu, b_gu, w_down, b_down):
    n_blocks = blk_e.shape[0]
    return pl.pallas_call(
        _moe_kernel,
        grid_spec=pltpu.PrefetchScalarGridSpec(
            num_scalar_prefetch=2,
            grid=(n_blocks,),
            in_specs=[
                pl.BlockSpec((MOE_ROWS, D_MODEL), lambda i, be, nu: (i, 0)),
                pl.BlockSpec((1, D_MODEL, 2 * D_FF), lambda i, be, nu: (be[i], 0, 0)),
                pl.BlockSpec((1, 1, 2 * D_FF), lambda i, be, nu: (be[i], 0, 0)),
                pl.BlockSpec((1, D_FF, D_MODEL), lambda i, be, nu: (be[i], 0, 0)),
                pl.BlockSpec((1, 1, D_MODEL), lambda i, be, nu: (be[i], 0, 0)),
            ],
            out_specs=pl.BlockSpec((MOE_ROWS, D_MODEL), lambda i, be, nu: (i, 0)),
            scratch_shapes=[
                pltpu.VMEM((D_MODEL, 2 * D_FF), BF16),
                pltpu.VMEM((D_FF, D_MODEL), BF16),
            ],
        ),
        out_shape=jax.ShapeDtypeStruct((n_blocks * MOE_ROWS, D_MODEL), F32),
        compiler_params=_params(("arbitrary",)),
        name="moe_experts",
    )(blk_e, n_used, xs, w_gu, b_gu.reshape(N_EXPERTS, 1, -1), w_down, b_down.reshape(N_EXPERTS, 1, -1))


COMB_T = 128


def _combine_kernel(dest_ref, yb_ref, x1_ref, rt_ref, ga2_ref, gf_ref, y_ref, gbuf, sem, *, tok_off):
    i = pl.program_id(1)
    b = pl.program_id(0)
    n_i = pl.num_programs(1)
    step = b * n_i + i
    n_steps = pl.num_programs(0) * n_i
    tm = gbuf.shape[2]

    def start_gather(stp, slot):
        base = (tok_off + stp * tm) * TOP_K

        def body(r, carry):
            for k in range(TOP_K):
                row = dest_ref[base + r * TOP_K + k]
                pltpu.make_async_copy(yb_ref.at[pl.ds(row, 1), :], gbuf.at[slot, k, pl.ds(r, 1), :],
                                      sem.at[slot]).start(priority=k % 2)
            return carry

        lax.fori_loop(0, tm, body, 0)

    @pl.when(step == 0)
    def _():
        start_gather(0, 0)

    slot = step % 2

    @pl.when(step + 1 < n_steps)
    def _():
        start_gather(step + 1, 1 - slot)

    for k in range(TOP_K):
        pltpu.make_async_copy(yb_ref.at[pl.ds(0, tm), :], gbuf.at[slot, k], sem.at[slot]).wait()
    rec = rt_ref[0]
    moe = gbuf[slot, 0] * rec[:, ROUTE_W:ROUTE_W + 1]
    for k in range(1, TOP_K):
        moe = moe + gbuf[slot, k] * rec[:, ROUTE_W + k:ROUTE_W + k + 1]
    y_ref[0] = _rmsnorm(x1_ref[0] + ga2_ref[0] * moe, gf_ref[...])


def _combine(yb, dest, x1, route, ga2, g_final, tok_off):
    G, T, _ = x1.shape
    R = ga2.shape[1]
    tm = min(COMB_T, T)
    rm = 1 if R == 1 else tm
    mod_map = (lambda b, i, d: (b, 0, 0)) if R == 1 else (lambda b, i, d: (b, i, 0))
    return pl.pallas_call(
        functools.partial(_combine_kernel, tok_off=tok_off),
        grid_spec=pltpu.PrefetchScalarGridSpec(
            num_scalar_prefetch=1,
            grid=(G, T // tm),
            in_specs=[
                pl.BlockSpec(memory_space=pl.ANY),
                pl.BlockSpec((1, tm, D_MODEL), lambda b, i, d: (b, i, 0)),
                pl.BlockSpec((1, tm, LANES), lambda b, i, d: (b, i, 0)),
                pl.BlockSpec((1, rm, D_MODEL), mod_map),
                pl.BlockSpec((1, D_MODEL), lambda b, i, d: (0, 0)),
            ],
            out_specs=pl.BlockSpec((1, tm, D_MODEL), lambda b, i, d: (b, i, 0)),
            scratch_shapes=[
                pltpu.VMEM((2, TOP_K, tm, D_MODEL), F32),
                pltpu.SemaphoreType.DMA((2,)),
            ],
        ),
        out_shape=jax.ShapeDtypeStruct((G, T, D_MODEL), F32),
        compiler_params=_params(("arbitrary", "arbitrary")),
        name="combine",
    )(dest, yb, x1, route, ga2, g_final.reshape(1, -1))


def _routing(route, counts):
    n = route.shape[0]
    na = n * TOP_K
    topi = route[:, ROUTE_E:ROUTE_E + TOP_K].astype(jnp.int32)
    rank = route[:, ROUTE_R:ROUTE_R + TOP_K].astype(jnp.int32)
    cnt = counts.astype(jnp.int32)
    padded = (cnt + MOE_ROWS - 1) // MOE_ROWS * MOE_ROWS
    pad_end = jnp.cumsum(padded)
    pad_start = pad_end - padded
    onehot = topi[..., None] == jnp.arange(N_EXPERTS, dtype=jnp.int32)
    dest = (jnp.sum(jnp.where(onehot, pad_start, 0), axis=-1) + rank).reshape(na)
    n_blocks = -(-na // MOE_ROWS) + N_EXPERTS
    blk_start = jnp.arange(n_blocks, dtype=jnp.int32) * MOE_ROWS
    blk_e = jnp.minimum(jnp.sum(blk_start[:, None] >= pad_end[None, :], axis=1), N_EXPERTS - 1).astype(jnp.int32)
    n_used = (pad_end[-1] // MOE_ROWS).astype(jnp.int32).reshape(1)
    pad_edges = jnp.concatenate([jnp.zeros((1,), jnp.int32), pad_end.astype(jnp.int32)])
    return dest, blk_e, n_used, pad_edges


def _mods(mod):
    return [m[:, None, :] for m in jnp.split(mod, 6, axis=-1)]


def _layer(xp, xs, cp, cs, cache_k, cache_v, state, page_table, w_ada, b_ada, g_norm_mix, w_in, w_gk2, b_gk2,
           g_gla_norm, w_br_a, w_br_b, w_out, g_norm_ffn, w_router, b_router, w_gu, b_gu, w_down, b_down,
           g_final):
    B, T, _ = xp.shape
    S = xs.shape[0]
    c_all = jnp.concatenate([cp, cs], axis=0)
    pad = -c_all.shape[0] % 8
    mod = _ada_mod(jnp.pad(c_all, ((0, pad), (0, 0))), w_ada, b_ada)
    sh1p, sc1p, ga1p, sh2p, sc2p, ga2p = _mods(mod[:B])
    sh1s, sc1s, ga1s, sh2s, sc2s, ga2s = [m[None] for m in jnp.split(mod[B:B + S], 6, axis=-1)]

    w_perm = jnp.concatenate(
        [w_in[:, :COL_ZA], w_in[:, COL_ZA + G_LOWRANK:], w_in[:, COL_ZA:COL_ZA + G_LOWRANK],
         jnp.zeros((D_MODEL, PROJ_W - COL_LR - G_LOWRANK), F32)], axis=1).astype(BF16)
    wa, wb, wo = w_br_a.astype(BF16), w_br_b.astype(BF16), w_out.astype(BF16)

    xs3 = xs.reshape(1, S, D_MODEL)
    proj_p = _inproj(xp, sc1p, sh1p, g_norm_mix, w_perm, tm=1024)
    proj_s = _inproj(xs3, sc1s, sh1s, g_norm_mix, w_perm, tm=S)

    a_p = _moba_prompt(proj_p)
    og_p, s_p = _gla_prompt(proj_p, w_gk2, b_gk2, g_gla_norm)
    ps = proj_s[0]
    k_s = ps[:, COL_AK:COL_AK + A_KVW]
    v_s = ps[:, COL_AV:COL_AV + A_KVW]
    a_s = _moba_sample(ps[:, :A_QW], k_s, v_s, cache_k, cache_v, page_table)
    og_s, s_s = _gla_sample(proj_s.reshape(S, 1, PROJ_W), state, w_gk2, b_gk2, g_gla_norm)

    cnt0 = jnp.zeros((1, LANES), F32)
    x1p, h2p, rtp, cnt1 = _merge_router(xp, a_p, og_p, proj_p, ga1p, sc2p, sh2p, g_norm_ffn, wa, wb, wo,
                                        w_router, b_router, cnt0, tm=512)
    x1s, h2s, rts, cnt2 = _merge_router(xs3, a_s.astype(BF16)[None], og_s.reshape(1, S, -1).astype(BF16), proj_s,
                                        ga1s, sc2s, sh2s, g_norm_ffn, wa, wb, wo, w_router, b_router, cnt1, tm=S)

    n_p = B * T
    route = jnp.concatenate([rtp.reshape(n_p, LANES), rts.reshape(S, LANES)], axis=0)
    dest, blk_e, n_used, pad_edges = _routing(route, cnt2[0, :N_EXPERTS])
    xsort = _dispatch(h2p.reshape(n_p, D_MODEL), h2s.reshape(S, D_MODEL), dest, pad_edges, n_used,
                      rows=blk_e.shape[0] * MOE_ROWS)
    yb = _moe_experts(xsort, blk_e, n_used, w_gu, b_gu, w_down, b_down)
    y_p = _combine(yb, dest, x1p, rtp, ga2p, g_final, tok_off=0)
    y_s = _combine(yb, dest, x1s, rts, ga2s, g_final, tok_off=n_p)

    kp = proj_p[:, :, COL_AK:COL_AK + A_KVW].reshape(B, T, A_KV_HEADS, A_HEAD_DIM)
    vp = proj_p[:, :, COL_AV:COL_AV + A_KVW].reshape(B, T, A_KV_HEADS, A_HEAD_DIM)
    ks = k_s.reshape(S, 1, A_KV_HEADS, A_HEAD_DIM)
    vs = v_s.reshape(S, 1, A_KV_HEADS, A_HEAD_DIM)
    return y_p, y_s.reshape(S, 1, D_MODEL), kp, vp, s_p, ks, vs, s_s


def kernel(x_prompt, x_sample, c_prompt, c_sample, cache_k, cache_v, state_gla, page_table, w_ada, b_ada, g_norm_mix, w_in, w_gk2, b_gk2, g_gla_norm, w_br_a, w_br_b, w_out, g_norm_ffn, w_router, b_router, w_gu, b_gu, w_down, b_down, g_final):
    assert w_ada.shape[0] == 1, "single-layer step"
    outs = _layer(x_prompt, x_sample.reshape(x_sample.shape[0], D_MODEL), c_prompt, c_sample, cache_k[0],
                  cache_v[0], state_gla[0], page_table, w_ada[0], b_ada[0], g_norm_mix[0], w_in[0], w_gk2[0],
                  b_gk2[0], g_gla_norm[0], w_br_a[0], w_br_b[0], w_out[0], g_norm_ffn[0], w_router[0],
                  b_router[0], w_gu[0], b_gu[0], w_down[0], b_down[0], g_final)
    y_p, y_s, kp, vp, s_p, ks, vs, s_s = outs
    return (y_p, y_s, kp[None], vp[None], s_p[None], ks[None], vs[None], s_s[None])
```

```python
import functools

import jax
import jax.numpy as jnp
from jax import lax
from jax.experimental import pallas as pl
from jax.experimental.pallas import tpu as pltpu

F32 = jnp.float32
BF16 = jnp.bfloat16

D_MODEL = 1024
A_HEADS = 8
A_KV_HEADS = 4
A_GROUP = A_HEADS // A_KV_HEADS
A_HEAD_DIM = 64
A_QW = A_HEADS * A_HEAD_DIM
A_KVW = A_KV_HEADS * A_HEAD_DIM
A_BLOCK = 256
A_TOPK = 3
PAGE_SIZE = 128
G_HEADS = 4
G_KEY_DIM = D_MODEL // 2
G_VAL_DIM = D_MODEL
G_DK = G_KEY_DIM // G_HEADS
G_DV = G_VAL_DIM // G_HEADS
G_LOWRANK = 16
G_NORMALIZER = 16.0
G_CHUNK = 64
N_EXPERTS = 32
TOP_K = 4
D_FF = D_MODEL
SWIGLU_LIMIT = 7.0
SWIGLU_ALPHA = 1.702
EPS = 1e-6

LANES = 128
COL_AQ = 0
COL_AK = COL_AQ + A_QW
COL_AV = COL_AK + A_KVW
COL_GQ = COL_AV + A_KVW
COL_GK = COL_GQ + G_KEY_DIM
COL_GV = COL_GK + G_KEY_DIM
COL_GG = COL_GV + G_VAL_DIM
COL_ZA = COL_GG + G_VAL_DIM
COL_ZB = COL_ZA + D_MODEL
COL_LR = COL_ZB + D_MODEL
PROJ_TN = 1280
PROJ_W = 5 * PROJ_TN
assert PROJ_W >= COL_LR + LANES
NEG_BIG = -1e30
VMEM_LIMIT = 56 * 1024 * 1024

NT_DIMS = (((1,), (1,)), ((), ()))
NN_DIMS = (((1,), (0,)), ((), ()))


def _params(sem):
    return pltpu.CompilerParams(dimension_semantics=sem, vmem_limit_bytes=VMEM_LIMIT)


def _split(a):
    hi = a.astype(BF16)
    lo = (a - hi.astype(F32)).astype(BF16)
    return hi, lo


def _dot3(a, b, dims=NN_DIMS):
    ah, al = _split(a)
    bh, bl = _split(b)
    d = lambda x, y: lax.dot_general(x, y, dims, preferred_element_type=F32)
    return d(ah, bh) + d(al, bh) + d(ah, bl)


def _sigmoid(x):
    return 1.0 / (1.0 + jnp.exp(-x))


def _log_sigmoid(x):
    return -(jnp.maximum(-x, 0.0) + jnp.log(1.0 + jnp.exp(-jnp.abs(x))))


def _rmsnorm(x, g):
    return x * lax.rsqrt(jnp.mean(x * x, axis=-1, keepdims=True) + EPS) * g


def _top_cols(score, colf, n_pick):
    vals, cols = [], []
    big = float(score.shape[1])
    for _ in range(n_pick):
        mx = jnp.max(score, axis=1, keepdims=True)
        first = jnp.min(jnp.where(score == mx, colf, big), axis=1, keepdims=True)
        vals.append(mx)
        cols.append(first)
        score = jnp.where(colf == first, -jnp.inf, score)
    return vals, cols


def _ada_kernel(c_ref, w_ref, b_ref, o_ref):
    c = c_ref[...]
    o_ref[...] = _dot3(c * _sigmoid(c), w_ref[...]) + b_ref[...]


def _ada_mod(c, w_ada, b_ada):
    rows = c.shape[0]
    tn = 1536
    return pl.pallas_call(
        _ada_kernel,
        grid=(6 * D_MODEL // tn,),
        in_specs=[
            pl.BlockSpec((rows, D_MODEL), lambda j: (0, 0)),
            pl.BlockSpec((D_MODEL, tn), lambda j: (0, j)),
            pl.BlockSpec((1, tn), lambda j: (0, j)),
        ],
        out_specs=pl.BlockSpec((rows, tn), lambda j: (0, j)),
        out_shape=jax.ShapeDtypeStruct((rows, 6 * D_MODEL), F32),
        compiler_params=_params(("arbitrary",)),
        name="ada_mod",
    )(c, w_ada, b_ada.reshape(1, -1))


def _inproj_kernel(x_ref, sc_ref, sh_ref, g_ref, w_ref, o_ref, h_ref):
    @pl.when(pl.program_id(2) == 0)
    def _():
        y = _rmsnorm(x_ref[0], g_ref[...])
        h_ref[...] = (y * (1.0 + sc_ref[0]) + sh_ref[0]).astype(BF16)

    o_ref[0] = jnp.dot(h_ref[...], w_ref[...], preferred_element_type=F32)


def _inproj(x, sc, sh, g, w_perm, tm):
    G, T, _ = x.shape
    R = sc.shape[1]
    rm = 1 if R == 1 else tm
    mod_map = (lambda b, i, j: (b, 0, 0)) if R == 1 else (lambda b, i, j: (b, i, 0))
    return pl.pallas_call(
        _inproj_kernel,
        grid=(G, T // tm, PROJ_W // PROJ_TN),
        in_specs=[
            pl.BlockSpec((1, tm, D_MODEL), lambda b, i, j: (b, i, 0)),
            pl.BlockSpec((1, rm, D_MODEL), mod_map),
            pl.BlockSpec((1, rm, D_MODEL), mod_map),
            pl.BlockSpec((1, D_MODEL), lambda b, i, j: (0, 0)),
            pl.BlockSpec((D_MODEL, PROJ_TN), lambda b, i, j: (0, j)),
        ],
        out_specs=pl.BlockSpec((1, tm, PROJ_TN), lambda b, i, j: (b, i, j)),
        out_shape=jax.ShapeDtypeStruct((G, T, PROJ_W), F32),
        scratch_shapes=[pltpu.VMEM((tm, D_MODEL), BF16)],
        compiler_params=_params(("arbitrary", "arbitrary", "arbitrary")),
        name="inproj",
    )(x, sc, sh, g.reshape(1, -1), w_perm)


PREP_T = 2048
AUG_NB = 32
AUG_REST = LANES - A_HEAD_DIM - AUG_NB
KV_GROUP = 4


def _moba_prep_kernel(k_ref, v_ref, ka_ref, vt_ref, km_ref):
    t = pl.program_id(1)
    k = k_ref[0]
    v = v_ref[0]
    nblk = PREP_T // A_BLOCK
    km_ref[0] = jnp.mean(k.reshape(nblk, A_BLOCK, A_KVW), axis=1)
    half = LANES - A_HEAD_DIM
    row = lax.broadcasted_iota(jnp.int32, (A_BLOCK, half), 0)
    col = lax.broadcasted_iota(jnp.int32, (A_BLOCK, half), 1)
    vextra = jnp.where(col == 0, 1.0, 0.0)
    for jb in range(nblk):
        blk = t * nblk + jb
        kextra = jnp.where(col == blk, 1.0, 0.0)
        kextra = jnp.where((col == AUG_NB) | (col == AUG_NB + 1), 1.0, kextra)
        kextra = jnp.where(col == AUG_NB + 2, row.astype(F32), kextra)
        kextra = jnp.where(col == AUG_NB + 3, (blk * A_BLOCK).astype(F32), kextra)
        rows = slice(jb * A_BLOCK, (jb + 1) * A_BLOCK)
        for h in range(A_KV_HEADS):
            cols = slice(h * A_HEAD_DIM, (h + 1) * A_HEAD_DIM)
            ka_ref[0, h, jb] = jnp.concatenate([k[rows, cols], kextra], axis=1).astype(BF16)
            vt_ref[0, h, jb] = jnp.concatenate([v[rows, cols], vextra], axis=1).T.astype(BF16)


def _moba_prep(proj):
    B, T, _ = proj.shape
    nblk = PREP_T // A_BLOCK
    nb = T // A_BLOCK
    assert nb <= AUG_NB
    return pl.pallas_call(
        _moba_prep_kernel,
        grid=(B, T // PREP_T),
        in_specs=[
            pl.BlockSpec((1, PREP_T, A_KVW), lambda b, t: (b, t, COL_AK // A_KVW)),
            pl.BlockSpec((1, PREP_T, A_KVW), lambda b, t: (b, t, COL_AV // A_KVW)),
        ],
        out_specs=[
            pl.BlockSpec((1, A_KV_HEADS, nblk, A_BLOCK, LANES), lambda b, t: (b, 0, t, 0, 0)),
            pl.BlockSpec((1, A_KV_HEADS, nblk, LANES, A_BLOCK), lambda b, t: (b, 0, t, 0, 0)),
            pl.BlockSpec((1, nblk, A_KVW), lambda b, t: (b, t, 0)),
        ],
        out_shape=[
            jax.ShapeDtypeStruct((B, A_KV_HEADS, nb, A_BLOCK, LANES), BF16),
            jax.ShapeDtypeStruct((B, A_KV_HEADS, nb, LANES, A_BLOCK), BF16),
            jax.ShapeDtypeStruct((B, nb, A_KVW), F32),
        ],
        compiler_params=_params(("arbitrary", "arbitrary")),
        name="moba_prep",
    )(proj, proj)


def _slope_of(head_plus_one, shape):
    bits = (127 - head_plus_one) << 23
    if shape is not None:
        bits = jnp.full(shape, bits, jnp.int32)
    return lax.bitcast_convert_type(bits, F32)


def _top_rows(score, rowf, n_pick, n_valid):
    picked = jnp.zeros(score.shape, jnp.bool_)
    big = float(score.shape[0])
    for r in range(n_pick):
        mx = jnp.max(score, axis=0, keepdims=True)
        first = jnp.min(jnp.where(score == mx, rowf, big), axis=0, keepdims=True)
        hit = rowf == first
        picked = picked | (hit & ((jnp.zeros_like(rowf) + n_valid) > r))
        score = jnp.where(hit, -jnp.inf, score)
    return picked


def _moba_attn_kernel(q_ref, km_ref, ka_ref, vt_ref, o_ref, qa_ref, qo_ref, m_ref, acc_ref, sa_ref, sb_ref):
    kvh = pl.program_id(1)
    i = pl.program_id(2)
    nb = km_ref.shape[2]
    qt = q_ref[0].T
    km = km_ref[0, 0]
    blk_f = i.astype(F32)
    wide = A_GROUP * A_BLOCK
    q2 = jnp.concatenate([qt[hh * A_HEAD_DIM:(hh + 1) * A_HEAD_DIM, :] for hh in range(A_GROUP)], axis=1)
    rowf = lax.broadcasted_iota(jnp.int32, (nb, wide), 0).astype(F32)
    gate = jnp.where(rowf < blk_f, _dot3(km, q2), -jnp.inf)
    selbias = jnp.where(_top_rows(gate, rowf, A_TOPK, blk_f), 0.0, NEG_BIG)
    if nb < AUG_NB:
        selbias = jnp.concatenate([selbias, jnp.zeros((AUG_NB - nb, wide), F32)], axis=0)
    erow = lax.broadcasted_iota(jnp.int32, (AUG_REST, wide), 0)
    lane = lax.broadcasted_iota(jnp.int32, (AUG_REST, wide), 1)
    qq = (lane % A_BLOCK).astype(F32)
    slope = _slope_of(kvh * A_GROUP + lane // A_BLOCK + 1, None)
    alibi = jnp.where(erow == 0, -slope * qq, 0.0)
    alibi = jnp.where(erow == 1, -slope * (blk_f * A_BLOCK), alibi)
    alibi = jnp.where((erow == 2) | (erow == 3), slope, alibi)
    qs = q2 * (A_HEAD_DIM ** -0.5)
    qa2 = jnp.concatenate([qs, selbias, alibi], axis=0).astype(BF16)
    qo2 = jnp.concatenate([qs, jnp.zeros((AUG_NB, wide), F32), alibi], axis=0).astype(BF16)
    for hh in range(A_GROUP):
        qa_ref[hh] = qa2[:, hh * A_BLOCK:(hh + 1) * A_BLOCK]
        qo_ref[hh] = qo2[:, hh * A_BLOCK:(hh + 1) * A_BLOCK]

    for j in range(KV_GROUP):
        for hh in range(A_GROUP):
            sa_ref[hh, j] = jnp.dot(ka_ref[0, 0, j], qa_ref[hh], preferred_element_type=F32)

    kb = ka_ref[0, 0, i]
    vb = vt_ref[0, 0, i]
    key = lax.broadcasted_iota(jnp.int32, (A_BLOCK, A_BLOCK), 0)
    qry = lax.broadcasted_iota(jnp.int32, (A_BLOCK, A_BLOCK), 1)
    for hh in range(A_GROUP):
        s = jnp.dot(kb, qo_ref[hh], preferred_element_type=F32)
        s = jnp.where(key <= qry, s, NEG_BIG)
        m = jnp.max(s, axis=0, keepdims=True)
        p = jnp.exp(s - m)
        m_ref[hh] = m
        acc_ref[hh] = jnp.dot(vb, p.astype(BF16), preferred_element_type=F32)

    n_groups = (i + KV_GROUP - 1) // KV_GROUP

    def scores(g, s_ref):
        for j in range(KV_GROUP):
            kj = ka_ref[0, 0, g * KV_GROUP + j]
            for hh in range(A_GROUP):
                s_ref[hh, j] = jnp.dot(kj, qa_ref[hh], preferred_element_type=F32)

    def accumulate(g, s_ref):
        for hh in range(A_GROUP):
            m_old = m_ref[hh]
            m_new = m_old
            for j in range(KV_GROUP):
                m_new = jnp.maximum(m_new, jnp.max(s_ref[hh, j], axis=0, keepdims=True))
            acc = jnp.exp(m_old - m_new) * acc_ref[hh]
            for j in range(KV_GROUP):
                p = jnp.exp(s_ref[hh, j] - m_new).astype(BF16)
                acc = acc + jnp.dot(vt_ref[0, 0, g * KV_GROUP + j], p, preferred_element_type=F32)
            acc_ref[hh] = acc
            m_ref[hh] = m_new

    def on_parity(g, fn):
        @pl.when(g % 2 == 0)
        def _():
            fn(sa_ref, sb_ref)

        @pl.when(g % 2 == 1)
        def _():
            fn(sb_ref, sa_ref)

    @pl.when(n_groups > 0)
    def _():
        def body(g, carry):
            def step(cur, nxt):
                scores(g + 1, nxt)
                accumulate(g, cur)

            on_parity(g, step)
            return carry

        lax.fori_loop(0, n_groups - 1, body, 0)
        on_parity(n_groups - 1, lambda cur, nxt: accumulate(n_groups - 1, cur))

    outs = []
    for hh in range(A_GROUP):
        acc = acc_ref[hh]
        outs.append((acc[:A_HEAD_DIM] / acc[A_HEAD_DIM:A_HEAD_DIM + 1]).T)
    o_ref[0] = jnp.concatenate(outs, axis=1).astype(BF16)


def _moba_prompt(proj):
    B, T, _ = proj.shape
    nb = T // A_BLOCK
    assert nb % KV_GROUP == 0
    kaug, vaug_t, km = _moba_prep(proj)
    km = km.reshape(B, nb, A_KV_HEADS, A_HEAD_DIM).transpose(0, 2, 1, 3)
    qw = A_GROUP * A_HEAD_DIM
    return pl.pallas_call(
        _moba_attn_kernel,
        grid=(B, A_KV_HEADS, nb),
        in_specs=[
            pl.BlockSpec((1, A_BLOCK, qw), lambda b, h, i: (b, i, h)),
            pl.BlockSpec((1, 1, nb, A_HEAD_DIM), lambda b, h, i: (b, h, 0, 0)),
            pl.BlockSpec((1, 1, nb, A_BLOCK, LANES), lambda b, h, i: (b, h, 0, 0, 0)),
            pl.BlockSpec((1, 1, nb, LANES, A_BLOCK), lambda b, h, i: (b, h, 0, 0, 0)),
        ],
        out_specs=pl.BlockSpec((1, A_BLOCK, qw), lambda b, h, i: (b, i, h)),
        out_shape=jax.ShapeDtypeStruct((B, T, A_QW), BF16),
        scratch_shapes=[
            pltpu.VMEM((A_GROUP, LANES, A_BLOCK), BF16),
            pltpu.VMEM((A_GROUP, LANES, A_BLOCK), BF16),
            pltpu.VMEM((A_GROUP, 1, A_BLOCK), F32),
            pltpu.VMEM((A_GROUP, LANES, A_BLOCK), F32),
            pltpu.VMEM((A_GROUP, KV_GROUP, A_BLOCK, A_BLOCK), F32),
            pltpu.VMEM((A_GROUP, KV_GROUP, A_BLOCK, A_BLOCK), F32),
        ],
        compiler_params=_params(("arbitrary", "arbitrary", "arbitrary")),
        name="moba_attn",
    )(proj, km, kaug, vaug_t)


GLA_T = 1024
GLA_UNROLL = 8


def _gla_kernel(q_ref, k_ref, v_ref, gg_ref, lr_ref, w2_ref, b2_ref, gn_ref, og_ref, st_ref, s_ref, g_ref):
    t = pl.program_id(2)

    @pl.when(t == 0)
    def _():
        s_ref[...] = jnp.zeros_like(s_ref)

    z = _dot3(lr_ref[0][:, :G_LOWRANK], w2_ref[...]) + b2_ref[...]
    g_ref[...] = _log_sigmoid(z) * (1.0 / G_NORMALIZER)
    C = G_CHUNK
    rr = lax.broadcasted_iota(jnp.int32, (C, C), 0)
    cc = lax.broadcasted_iota(jnp.int32, (C, C), 1)
    causal = cc <= rr
    tril = jnp.where(causal, 1.0, 0.0).astype(BF16)
    er = lax.broadcasted_iota(jnp.int32, (G_DK, G_DK), 0)
    ec = lax.broadcasted_iota(jnp.int32, (G_DK, G_DK), 1)
    eye = er == ec
    gn = gn_ref[...]

    def group(gi, s):
        sls = [pl.ds(pl.multiple_of((gi * GLA_UNROLL + c) * C, C), C) for c in range(GLA_UNROLL)]
        cums = []
        for sl in sls:
            gh, gl = _split(g_ref[sl, :])
            cums.append(jnp.dot(tril, gh, preferred_element_type=F32) + jnp.dot(tril, gl, preferred_element_type=F32))
        parts = []
        for sl, cum in zip(sls, cums):
            last = cum[C - 1:C, :]
            kc = k_ref[0, sl, :]
            vc = v_ref[0, sl, :].astype(BF16)
            qe = (q_ref[0, sl, :] * (G_DK ** -0.5) * jnp.exp(cum)).astype(BF16)
            ke = (kc * jnp.exp(-cum)).astype(BF16)
            kd = kc * jnp.exp(last - cum)
            att = lax.dot_general(qe, ke, NT_DIMS, preferred_element_type=F32)
            att = jnp.where(causal, att, 0.0).astype(BF16)
            intra = jnp.dot(att, vc, preferred_element_type=F32)
            inc = jnp.dot(kd.T.astype(BF16), vc, preferred_element_type=F32)
            decay = jnp.sum(jnp.where(eye, jnp.exp(last), 0.0), axis=1, keepdims=True)
            parts.append((qe, intra, inc, decay))
        for sl, (qe, intra, inc, decay) in zip(sls, parts):
            o = jnp.dot(qe, s.astype(BF16), preferred_element_type=F32) + intra
            s = s * decay + inc
            on = _rmsnorm(o, gn)
            gate = gg_ref[0, sl, :]
            og_ref[0, sl, :] = (on * (gate * _sigmoid(gate))).astype(BF16)
        return s

    s_ref[...] = lax.fori_loop(0, GLA_T // (C * GLA_UNROLL), group, s_ref[...])

    @pl.when(t == pl.num_programs(2) - 1)
    def _():
        st_ref[0, 0] = s_ref[...]


def _gla_prompt(proj, w_gk2, b_gk2, g_gla_norm):
    B, T, _ = proj.shape
    tt = min(GLA_T, T)
    assert tt == GLA_T
    return pl.pallas_call(
        _gla_kernel,
        grid=(B, G_HEADS, T // tt),
        in_specs=[
            pl.BlockSpec((1, tt, G_DK), lambda b, h, t: (b, t, COL_GQ // G_DK + h)),
            pl.BlockSpec((1, tt, G_DK), lambda b, h, t: (b, t, COL_GK // G_DK + h)),
            pl.BlockSpec((1, tt, G_DV), lambda b, h, t: (b, t, COL_GV // G_DV + h)),
            pl.BlockSpec((1, tt, G_DV), lambda b, h, t: (b, t, COL_GG // G_DV + h)),
            pl.BlockSpec((1, tt, LANES), lambda b, h, t: (b, t, COL_LR // LANES)),
            pl.BlockSpec((G_LOWRANK, G_DK), lambda b, h, t: (0, h)),
            pl.BlockSpec((1, G_DK), lambda b, h, t: (0, h)),
            pl.BlockSpec((1, G_DV), lambda b, h, t: (0, 0)),
        ],
        out_specs=[
            pl.BlockSpec((1, tt, G_DV), lambda b, h, t: (b, t, h)),
            pl.BlockSpec((1, 1, G_DK, G_DV), lambda b, h, t: (b, h, 0, 0)),
        ],
        out_shape=[
            jax.ShapeDtypeStruct((B, T, G_VAL_DIM), BF16),
            jax.ShapeDtypeStruct((B, G_HEADS, G_DK, G_DV), F32),
        ],
        scratch_shapes=[pltpu.VMEM((G_DK, G_DV), F32), pltpu.VMEM((tt, G_DK), F32)],
        compiler_params=_params(("arbitrary", "arbitrary", "arbitrary")),
        name="gla_prompt",
    )(proj, proj, proj, proj, proj, w_gk2, b_gk2.reshape(1, -1), g_gla_norm.reshape(1, -1))


PPB = A_BLOCK // PAGE_SIZE


def _smoba_kernel(pt_ref, qb_ref, q8_ref, kn_ref, vn_ref, ck_ref, cv_ref, o_ref,
                  kbuf, vbuf, lg_ref, p_ref, pown_ref, idx_v, idx_s, ksem, vsem, isem):
    s = pl.program_id(0)
    n_seq = pl.num_programs(0) - 1
    n_pages = kbuf.shape[1]
    nb = n_pages // PPB
    past = n_pages * PAGE_SIZE
    hd = A_HEAD_DIM

    def start_keys(seq, slot):
        for p in range(n_pages):
            pltpu.make_async_copy(ck_ref.at[pt_ref[seq, p]], kbuf.at[slot, p], ksem.at[slot]).start()

    def value_copy(seq, slot, h, r, e):
        page = pt_ref[seq, idx_s[slot, h, r] * PPB + e]
        return pltpu.make_async_copy(cv_ref.at[page, pl.ds((h // A_GROUP) * hd, hd), :],
                                     vbuf.at[slot, h, r, e], vsem.at[slot])

    @pl.when(s == 0)
    def _():
        start_keys(0, 0)

    @pl.when(s + 1 < n_seq)
    def _():
        start_keys(s + 1, (s + 1) % 2)

    @pl.when(s < n_seq)
    def _():
        slot = s % 2
        pltpu.make_async_copy(ck_ref.at[pl.ds(0, n_pages)], kbuf.at[slot], ksem.at[slot]).wait()

        def page_logits(p, carry):
            kp = kbuf[slot, p]
            rows = []
            for h in range(A_HEADS):
                g = h // A_GROUP
                rows.append(jnp.sum(qb_ref[0, h] * kp[g * hd:(g + 1) * hd, :], axis=0, keepdims=True))
            lg_ref[slot, p] = jnp.concatenate(rows, axis=0)
            return carry

        lax.fori_loop(0, n_pages, page_logits, 0)
        lg = lg_ref[slot]
        gate = jnp.sum(jnp.sum(lg.reshape(nb, PPB, A_HEADS, PAGE_SIZE), axis=1), axis=2, keepdims=True)
        bidx = lax.broadcasted_iota(jnp.int32, (nb, A_HEADS, 1), 0).astype(F32)
        picked = jnp.zeros((nb, A_HEADS, 1), F32)
        firsts = []
        for r in range(A_TOPK):
            mx = jnp.max(gate, axis=0, keepdims=True)
            first = jnp.min(jnp.where(gate == mx, bidx, float(nb)), axis=0, keepdims=True)
            hit = bidx == first
            picked = jnp.where(hit, 1.0, picked)
            gate = jnp.where(hit, -jnp.inf, gate)
            firsts.append(first.reshape(A_HEADS, 1))
        sel = jnp.broadcast_to(picked[:, None], (nb, PPB, A_HEADS, 1)).reshape(n_pages, A_HEADS, 1)
        head1 = lax.broadcasted_iota(jnp.int32, (1, A_HEADS, 1), 1) + 1
        slope = lax.bitcast_convert_type((127 - head1) << 23, F32)
        tpos = (lax.broadcasted_iota(jnp.int32, (n_pages, 1, PAGE_SIZE), 0) * PAGE_SIZE
                + lax.broadcasted_iota(jnp.int32, (n_pages, 1, PAGE_SIZE), 2))
        dist = (past - tpos).astype(F32)
        att = jnp.where(sel > 0.5, lg * (hd ** -0.5) - slope * dist, NEG_BIG)
        own = jnp.sum(q8_ref[0] * kn_ref[0], axis=1, keepdims=True) * (hd ** -0.5)
        m = jnp.maximum(jnp.max(jnp.max(att, axis=0), axis=1, keepdims=True), own)
        p = jnp.exp(att - m.reshape(1, A_HEADS, 1))
        p_own = jnp.exp(own - m)
        inv = 1.0 / (jnp.sum(jnp.sum(p, axis=0), axis=1, keepdims=True) + p_own)
        p_ref[slot] = p * inv.reshape(1, A_HEADS, 1)
        pown_ref[slot] = jnp.broadcast_to(p_own * inv, (A_HEADS, LANES))
        lane = lax.broadcasted_iota(jnp.int32, (A_HEADS, LANES), 1)
        it = jnp.zeros((A_HEADS, LANES), F32)
        for r in range(A_TOPK):
            it = jnp.where(lane == r, firsts[r], it)
        idx_v[...] = it.astype(jnp.int32)
        to_smem = pltpu.make_async_copy(idx_v, idx_s.at[slot], isem)
        to_smem.start()
        to_smem.wait()
        for h in range(A_HEADS):
            for r in range(A_TOPK):
                for e in range(PPB):
                    value_copy(s, slot, h, r, e).start()

    @pl.when(s >= 1)
    def _():
        sb = s - 1
        b = sb % 2
        for h in range(A_HEADS):
            for r in range(A_TOPK):
                for e in range(PPB):
                    value_copy(sb, b, h, r, e).wait()
        lane = lax.broadcasted_iota(jnp.int32, (hd, LANES), 1)
        cols = jnp.zeros((hd, LANES), F32)
        for h in range(A_HEADS):
            acc = jnp.zeros((hd, PAGE_SIZE), F32)
            for r in range(A_TOPK):
                for e in range(PPB):
                    pg = idx_s[b, h, r] * PPB + e
                    acc = acc + vbuf[b, h, r, e] * p_ref[b, pg, h:h + 1, :]
            cols = jnp.where(lane == h, jnp.sum(acc, axis=1, keepdims=True), cols)
        o_ref[0] = cols.T[:A_HEADS, :] + pown_ref[b][:, :hd] * vn_ref[0]


def _moba_sample(q, k_new, v_new, cache_k, cache_v, page_table):
    S = q.shape[0]
    n_pages = page_table.shape[1]
    n_pool = cache_k.shape[0]
    hd = A_HEAD_DIM
    ck = cache_k.transpose(0, 2, 3, 1).reshape(n_pool, A_KVW, PAGE_SIZE)
    cv = cache_v.transpose(0, 2, 3, 1).reshape(n_pool, A_KVW, PAGE_SIZE)
    q8 = q.reshape(S, A_HEADS, hd)
    qb = jnp.broadcast_to(q8[..., None], (S, A_HEADS, hd, LANES))
    kn8 = jnp.repeat(k_new.reshape(S, A_KV_HEADS, hd), A_GROUP, axis=1)
    vn8 = jnp.repeat(v_new.reshape(S, A_KV_HEADS, hd), A_GROUP, axis=1)
    seq = lambda s, pt: (jnp.minimum(s, S - 1), 0, 0)
    out = pl.pallas_call(
        _smoba_kernel,
        grid_spec=pltpu.PrefetchScalarGridSpec(
            num_scalar_prefetch=1,
            grid=(S + 1,),
            in_specs=[
                pl.BlockSpec((1, A_HEADS, hd, LANES), lambda s, pt: (jnp.minimum(s, S - 1), 0, 0, 0)),
                pl.BlockSpec((1, A_HEADS, hd), seq),
                pl.BlockSpec((1, A_HEADS, hd), seq),
                pl.BlockSpec((1, A_HEADS, hd), lambda s, pt: (jnp.maximum(s - 1, 0), 0, 0)),
                pl.BlockSpec(memory_space=pl.ANY),
                pl.BlockSpec(memory_space=pl.ANY),
            ],
            out_specs=pl.BlockSpec((1, A_HEADS, hd), lambda s, pt: (jnp.maximum(s - 1, 0), 0, 0)),
            scratch_shapes=[
                pltpu.VMEM((2, n_pages, A_KVW, PAGE_SIZE), F32),
                pltpu.VMEM((2, A_HEADS, A_TOPK, PPB, hd, PAGE_SIZE), F32),
                pltpu.VMEM((2, n_pages, A_HEADS, PAGE_SIZE), F32),
                pltpu.VMEM((2, n_pages, A_HEADS, PAGE_SIZE), F32),
                pltpu.VMEM((2, A_HEADS, LANES), F32),
                pltpu.VMEM((A_HEADS, LANES), jnp.int32),
                pltpu.SMEM((2, A_HEADS, LANES), jnp.int32),
                pltpu.SemaphoreType.DMA((2,)),
                pltpu.SemaphoreType.DMA((2,)),
                pltpu.SemaphoreType.DMA(()),
            ],
        ),
        out_shape=jax.ShapeDtypeStruct((S, A_HEADS, hd), F32),
        compiler_params=_params(("arbitrary",)),
        name="moba_sample",
    )(page_table, qb, q8, kn8, vn8, ck, cv)
    return out.reshape(S, A_QW)


def _sgla_kernel(p_ref, s0_ref, w2_ref, b2_ref, gn_ref, og_ref, s1_ref):
    row = p_ref[0]
    er = lax.broadcasted_iota(jnp.int32, (G_DK, G_DK), 0)
    ec = lax.broadcasted_iota(jnp.int32, (G_DK, G_DK), 1)
    eye = er == ec
    col_of = lambda r: jnp.sum(jnp.where(eye, r, 0.0), axis=1, keepdims=True)
    lr = row[:, COL_LR:COL_LR + G_LOWRANK]
    z = _dot3(lr, w2_ref[...]) + b2_ref[...]
    g_all = _log_sigmoid(z) * (1.0 / G_NORMALIZER)
    outs = []
    for h in range(G_HEADS):
        g = g_all[:, h * G_DK:(h + 1) * G_DK]
        q = row[:, COL_GQ + h * G_DK:COL_GQ + (h + 1) * G_DK] * (G_DK ** -0.5)
        k = row[:, COL_GK + h * G_DK:COL_GK + (h + 1) * G_DK]
        v = row[:, COL_GV + h * G_DV:COL_GV + (h + 1) * G_DV]
        gate = row[:, COL_GG + h * G_DV:COL_GG + (h + 1) * G_DV]
        qe = q * jnp.exp(g)
        ke = k * jnp.exp(-g)
        att = jnp.sum(qe * ke, axis=1, keepdims=True)
        s0 = s0_ref[0, h]
        o = _dot3(qe, s0) + att * v
        s1_ref[0, h] = s0 * col_of(jnp.exp(g)) + col_of(k) * v
        on = _rmsnorm(o, gn_ref[...])
        outs.append(on * (gate * _sigmoid(gate)))
    og_ref[0] = jnp.concatenate(outs, axis=1)


def _gla_sample(proj_rows, state, w_gk2, b_gk2, g_gla_norm):
    S = proj_rows.shape[0]
    return pl.pallas_call(
        _sgla_kernel,
        grid=(S,),
        in_specs=[
            pl.BlockSpec((1, 1, PROJ_W), lambda s: (s, 0, 0)),
            pl.BlockSpec((1, G_HEADS, G_DK, G_DV), lambda s: (s, 0, 0, 0)),
            pl.BlockSpec((G_LOWRANK, G_KEY_DIM), lambda s: (0, 0)),
            pl.BlockSpec((1, G_KEY_DIM), lambda s: (0, 0)),
            pl.BlockSpec((1, G_DV), lambda s: (0, 0)),
        ],
        out_specs=[
            pl.BlockSpec((1, 1, G_VAL_DIM), lambda s: (s, 0, 0)),
            pl.BlockSpec((1, G_HEADS, G_DK, G_DV), lambda s: (s, 0, 0, 0)),
        ],
        out_shape=[
            jax.ShapeDtypeStruct((S, 1, G_VAL_DIM), F32),
            jax.ShapeDtypeStruct((S, G_HEADS, G_DK, G_DV), F32),
        ],
        compiler_params=_params(("arbitrary",)),
        name="gla_sample",
    )(proj_rows, state, w_gk2, b_gk2.reshape(1, -1), g_gla_norm.reshape(1, -1))


ROUTE_E = 0
ROUTE_W = TOP_K
ROUTE_R = 2 * TOP_K


def _merge_kernel(x_ref, a_ref, og_ref, za_ref, zb_ref, ga1_ref, sc2_ref, sh2_ref, g2_ref, wa_ref, wb_ref,
                  wo_ref, wr_ref, br_ref, c0_ref, x1_ref, h2_ref, rt_ref, cnt_ref, run_ref):
    first = (pl.program_id(0) == 0) & (pl.program_id(1) == 0)

    @pl.when(first)
    def _():
        run_ref[...] = c0_ref[...]

    pa = jnp.dot(a_ref[0], wa_ref[...], preferred_element_type=F32)
    pb = jnp.dot(og_ref[0], wb_ref[...], preferred_element_type=F32)
    merged = _sigmoid(za_ref[0]) * pa + _sigmoid(zb_ref[0]) * pb
    mix = jnp.dot(merged.astype(BF16), wo_ref[...], preferred_element_type=F32)
    x1 = x_ref[0] + ga1_ref[0] * mix
    x1_ref[0] = x1
    h2 = _rmsnorm(x1, g2_ref[...]) * (1.0 + sc2_ref[0]) + sh2_ref[0]
    h2_ref[0] = h2
    logits = _dot3(h2, wr_ref[...]) + br_ref[...]
    tm = logits.shape[0]
    colf = lax.broadcasted_iota(jnp.int32, (tm, N_EXPERTS), 1).astype(F32)
    vals, cols = _top_cols(logits, colf, TOP_K)
    ex = [jnp.exp(v - vals[0]) for v in vals]
    tot = ex[0] + ex[1] + ex[2] + ex[3]
    onehot = [jnp.where(colf == c, 1.0, 0.0) for c in cols]
    osum = onehot[0] + onehot[1] + onehot[2] + onehot[3]
    tr = lax.broadcasted_iota(jnp.int32, (tm, tm), 0)
    tc = lax.broadcasted_iota(jnp.int32, (tm, tm), 1)
    lower = jnp.where(tc < tr, 1.0, 0.0).astype(BF16)
    before = jnp.dot(lower, osum.astype(BF16), preferred_element_type=F32) + run_ref[:, :N_EXPERTS]
    lane = lax.broadcasted_iota(jnp.int32, (tm, LANES), 1)
    rec = jnp.zeros((tm, LANES), F32)
    for r in range(TOP_K):
        rank = jnp.sum(onehot[r] * before, axis=1, keepdims=True)
        rec = jnp.where(lane == ROUTE_E + r, cols[r], rec)
        rec = jnp.where(lane == ROUTE_W + r, ex[r] / tot, rec)
        rec = jnp.where(lane == ROUTE_R + r, rank, rec)
    rt_ref[0] = rec
    pad = jnp.zeros((1, LANES - N_EXPERTS), F32)
    run_ref[...] = run_ref[...] + jnp.concatenate([jnp.sum(osum, axis=0, keepdims=True), pad], axis=1)
    cnt_ref[...] = run_ref[...]


def _merge_router(x, a, og, proj, ga1, sc2, sh2, g2, wa, wb, wo, wr, br, cnt0, tm):
    G, T, _ = x.shape
    R = ga1.shape[1]
    rm = 1 if R == 1 else tm
    mod_map = (lambda b, i: (b, 0, 0)) if R == 1 else (lambda b, i: (b, i, 0))
    tok = lambda w: pl.BlockSpec((1, tm, w), lambda b, i: (b, i, 0))
    full = lambda r, c: pl.BlockSpec((r, c), lambda b, i: (0, 0))
    return pl.pallas_call(
        _merge_kernel,
        grid=(G, T // tm),
        in_specs=[
            tok(D_MODEL), tok(A_QW), tok(G_VAL_DIM),
            pl.BlockSpec((1, tm, D_MODEL), lambda b, i: (b, i, COL_ZA // D_MODEL)),
            pl.BlockSpec((1, tm, D_MODEL), lambda b, i: (b, i, COL_ZB // D_MODEL)),
            pl.BlockSpec((1, rm, D_MODEL), mod_map),
            pl.BlockSpec((1, rm, D_MODEL), mod_map),
            pl.BlockSpec((1, rm, D_MODEL), mod_map),
            full(1, D_MODEL), full(A_QW, D_MODEL), full(G_VAL_DIM, D_MODEL), full(D_MODEL, D_MODEL),
            full(D_MODEL, N_EXPERTS), full(1, N_EXPERTS), full(1, LANES),
        ],
        out_specs=[tok(D_MODEL), tok(D_MODEL), tok(LANES), full(1, LANES)],
        out_shape=[
            jax.ShapeDtypeStruct((G, T, D_MODEL), F32),
            jax.ShapeDtypeStruct((G, T, D_MODEL), F32),
            jax.ShapeDtypeStruct((G, T, LANES), F32),
            jax.ShapeDtypeStruct((1, LANES), F32),
        ],
        scratch_shapes=[pltpu.VMEM((1, LANES), F32)],
        compiler_params=_params(("arbitrary", "arbitrary")),
        name="merge_router",
    )(x, a, og, proj, proj, ga1, sc2, sh2, g2.reshape(1, -1), wa, wb, wo, wr, br.reshape(1, -1), cnt0)


MOE_ROWS = 256
LOG2_TOP_K = TOP_K.bit_length() - 1
assert 1 << LOG2_TOP_K == TOP_K
ROW_TOK_BITS = 15
ROW_OUT_BITS = 17


def _moe_kernel(be_ref, nu_ref, rp_ref, h_ref, wgu_ref, bgu_ref, wdn_ref, bdn_ref,
                yt_ref, xa, xb, ya, yb, wgu_bf, wdn_bf, gsem, ssem, *, plane):
    i = pl.program_id(0)
    last = pl.num_programs(0) - 1
    n_used = nu_ref[0]

    def row_ids(blk, r, live):
        v = rp_ref[blk * MOE_ROWS + r]
        tok = v & ((1 << ROW_TOK_BITS) - 1)
        out = (v >> ROW_TOK_BITS) & ((1 << ROW_OUT_BITS) - 1)
        return tok, jnp.where(live, out, TOP_K * plane + r)

    def gather_copy(tok, r, x_ref, sem):
        return pltpu.make_async_copy(h_ref.at[pl.ds(tok, 1), :], x_ref.at[pl.ds(r, 1), :], sem)

    def scatter_copy(out, r, y_ref, sem):
        return pltpu.make_async_copy(y_ref.at[pl.ds(r, 1), :], yt_ref.at[pl.ds(out, 1), :], sem)

    def wait_gather(x_ref, sem):
        pltpu.make_async_copy(h_ref.at[pl.ds(0, MOE_ROWS), :], x_ref, sem).wait()

    def wait_scatter(y_ref, sem):
        pltpu.make_async_copy(y_ref, yt_ref.at[pl.ds(0, MOE_ROWS), :], sem).wait()

    def ffn_up(x):
        gu = jnp.dot(x.astype(BF16), wgu_bf[...], preferred_element_type=F32) + bgu_ref[0]
        g = jnp.minimum(gu[:, :D_FF], SWIGLU_LIMIT)
        u = jnp.clip(gu[:, D_FF:], -SWIGLU_LIMIT, SWIGLU_LIMIT)
        return ((u + 1.0) * g * _sigmoid(SWIGLU_ALPHA * g)).astype(BF16)

    def ffn_down(act):
        return jnp.dot(act, wdn_bf[...], preferred_element_type=F32) + bdn_ref[0]

    @pl.when(i == 0)
    def _():
        ya[...] = jnp.zeros_like(ya)
        yb[...] = jnp.zeros_like(yb)

        def body(r, carry):
            tok, _ = row_ids(0, r, True)
            gather_copy(tok, r, xa, gsem.at[0]).start()
            return carry

        lax.fori_loop(0, MOE_ROWS, body, 0)

    @pl.when((i < n_used) & ((i == 0) | (be_ref[i] != be_ref[jnp.maximum(i - 1, 0)])))
    def _():
        wgu_bf[...] = wgu_ref[0].astype(BF16)
        wdn_bf[...] = wdn_ref[0].astype(BF16)

    def step(x_cur, y_cur, x_oth, y_oth, p):
        nxt = jnp.minimum(i + 1, last)
        prv = jnp.maximum(i - 1, 0)
        wait_gather(x_cur, gsem.at[p])
        for r in range(MOE_ROWS):
            tok, _ = row_ids(nxt, r, True)
            gather_copy(tok, r, x_oth, gsem.at[1 - p]).start()
        act = ffn_up(x_cur[...])
        for r in range(MOE_ROWS):
            _, out = row_ids(prv, r, i > 0)
            scatter_copy(out, r, y_oth, ssem.at[1 - p]).start()
        y_cur[...] = ffn_down(act)
        wait_scatter(y_oth, ssem.at[1 - p])

    def finish(x_cur, y_cur, x_oth, y_oth, p):
        def body(r, carry):
            _, out = row_ids(i, r, True)
            scatter_copy(out, r, y_cur, ssem.at[p]).start()
            return carry

        lax.fori_loop(0, MOE_ROWS, body, 0)
        wait_scatter(y_cur, ssem.at[p])
        wait_gather(x_oth, gsem.at[1 - p])

    for p, bufs in enumerate(((xa, ya, xb, yb), (xb, yb, xa, ya))):
        @pl.when((i < n_used) & (i % 2 == p))
        def _():
            step(*bufs, p)

        @pl.when((i == n_used - 1) & (i % 2 == p))
        def _():
            finish(*bufs, p)


def _row_plan(order, base_idx, valid, plane):
    na = order.shape[0]
    rows = base_idx.shape[0] * MOE_ROWS
    assert plane <= 1 << ROW_TOK_BITS and TOP_K * plane + MOE_ROWS <= 1 << ROW_OUT_BITS
    r = jnp.arange(rows, dtype=jnp.int32) % MOE_ROWS
    blk = jnp.arange(rows, dtype=jnp.int32) // MOE_ROWS
    real = r < valid[blk]
    a = order[jnp.clip(base_idx[blk] + r, 0, na - 1)]
    tok = a >> LOG2_TOP_K
    out = jnp.where(real, (a & (TOP_K - 1)) * plane + tok, TOP_K * plane + r)
    packed = (out.astype(jnp.uint32) << ROW_TOK_BITS) | jnp.where(real, tok, 0).astype(jnp.uint32)
    return lax.bitcast_convert_type(packed, jnp.int32)


def _moe_experts(h2, order, blk_e, base_idx, valid, n_used, w_gu, b_gu, w_down, b_down):
    n = h2.shape[0]
    n_blocks = blk_e.shape[0]
    plane = n
    wspec = lambda shape: pl.BlockSpec(shape, lambda i, be, nu, rp: (be[i], 0, 0))
    return pl.pallas_call(
        functools.partial(_moe_kernel, plane=plane),
        grid_spec=pltpu.PrefetchScalarGridSpec(
            num_scalar_prefetch=3,
            grid=(n_blocks,),
            in_specs=[
                pl.BlockSpec(memory_space=pl.ANY),
                wspec((1, D_MODEL, 2 * D_FF)), wspec((1, 1, 2 * D_FF)),
                wspec((1, D_FF, D_MODEL)), wspec((1, 1, D_MODEL)),
            ],
            out_specs=pl.BlockSpec(memory_space=pl.ANY),
            scratch_shapes=[
                pltpu.VMEM((MOE_ROWS, D_MODEL), F32), pltpu.VMEM((MOE_ROWS, D_MODEL), F32),
                pltpu.VMEM((MOE_ROWS, D_MODEL), F32), pltpu.VMEM((MOE_ROWS, D_MODEL), F32),
                pltpu.VMEM((D_MODEL, 2 * D_FF), BF16), pltpu.VMEM((D_FF, D_MODEL), BF16),
                pltpu.SemaphoreType.DMA((2,)), pltpu.SemaphoreType.DMA((2,)),
            ],
        ),
        out_shape=jax.ShapeDtypeStruct((TOP_K * plane + MOE_ROWS, D_MODEL), F32),
        compiler_params=_params(("arbitrary",)),
        name="moe_experts",
    )(blk_e, n_used, _row_plan(order, base_idx, valid, plane), h2, w_gu, b_gu.reshape(N_EXPERTS, 1, -1),
      w_down, b_down.reshape(N_EXPERTS, 1, -1))


COMB_T = 128


def _combine_kernel(y0_ref, y1_ref, y2_ref, y3_ref, x1_ref, rt_ref, ga2_ref, gf_ref, y_ref):
    rec = rt_ref[0]
    moe = y0_ref[...] * rec[:, ROUTE_W:ROUTE_W + 1]
    for k, yk_ref in enumerate((y1_ref, y2_ref, y3_ref), start=1):
        moe = moe + yk_ref[...] * rec[:, ROUTE_W + k:ROUTE_W + k + 1]
    y_ref[0] = _rmsnorm(x1_ref[0] + ga2_ref[0] * moe, gf_ref[...])


def _combine(yt, plane, x1, route, ga2, g_final, tok_off):
    G, T, _ = x1.shape
    R = ga2.shape[1]
    tm = min(COMB_T, T)
    assert plane % tm == 0 and tok_off % tm == 0 and T % tm == 0
    rm = 1 if R == 1 else tm
    mod_map = (lambda b, i: (b, 0, 0)) if R == 1 else (lambda b, i: (b, i, 0))
    steps = T // tm
    yspec = lambda k: pl.BlockSpec((tm, D_MODEL), lambda b, i: ((k * plane + tok_off) // tm + b * steps + i, 0))
    return pl.pallas_call(
        _combine_kernel,
        grid=(G, steps),
        in_specs=[
            yspec(0), yspec(1), yspec(2), yspec(3),
            pl.BlockSpec((1, tm, D_MODEL), lambda b, i: (b, i, 0)),
            pl.BlockSpec((1, tm, LANES), lambda b, i: (b, i, 0)),
            pl.BlockSpec((1, rm, D_MODEL), mod_map),
            pl.BlockSpec((1, D_MODEL), lambda b, i: (0, 0)),
        ],
        out_specs=pl.BlockSpec((1, tm, D_MODEL), lambda b, i: (b, i, 0)),
        out_shape=jax.ShapeDtypeStruct((G, T, D_MODEL), F32),
        compiler_params=_params(("arbitrary", "arbitrary")),
        name="combine",
    )(yt, yt, yt, yt, x1, route, ga2, g_final.reshape(1, -1))


def _routing(route, counts):
    n = route.shape[0]
    na = n * TOP_K
    topi = route[:, ROUTE_E:ROUTE_E + TOP_K].astype(jnp.int32)
    rank = route[:, ROUTE_R:ROUTE_R + TOP_K].astype(jnp.int32)
    cnt = counts.astype(jnp.int32)
    padded = (cnt + MOE_ROWS - 1) // MOE_ROWS * MOE_ROWS
    pad_end = jnp.cumsum(padded)
    pad_start = pad_end - padded
    start = jnp.cumsum(cnt) - cnt
    onehot = topi[..., None] == jnp.arange(N_EXPERTS, dtype=jnp.int32)
    pos = (jnp.sum(jnp.where(onehot, start, 0), axis=-1) + rank).reshape(na)
    order = jnp.argsort(pos).astype(jnp.int32)
    n_blocks = -(-na // MOE_ROWS) + N_EXPERTS
    blk_start = jnp.arange(n_blocks, dtype=jnp.int32) * MOE_ROWS
    blk_e = jnp.minimum(jnp.sum(blk_start[:, None] >= pad_end[None, :], axis=1), N_EXPERTS - 1).astype(jnp.int32)
    within = blk_start - pad_start[blk_e]
    base_idx = (start[blk_e] + within).astype(jnp.int32)
    valid = jnp.clip(cnt[blk_e] - within, 0, MOE_ROWS).astype(jnp.int32)
    valid = jnp.where(blk_start < pad_end[-1], valid, 0)
    n_used = (pad_end[-1] // MOE_ROWS).astype(jnp.int32).reshape(1)
    return order, blk_e, base_idx, valid, n_used


def _mods(mod):
    return [m[:, None, :] for m in jnp.split(mod, 6, axis=-1)]


def _layer(xp, xs, cp, cs, cache_k, cache_v, state, page_table, w_ada, b_ada, g_norm_mix, w_in, w_gk2, b_gk2,
           g_gla_norm, w_br_a, w_br_b, w_out, g_norm_ffn, w_router, b_router, w_gu, b_gu, w_down, b_down,
           g_final):
    B, T, _ = xp.shape
    S = xs.shape[0]
    c_all = jnp.concatenate([cp, cs], axis=0)
    pad = -c_all.shape[0] % 8
    mod = _ada_mod(jnp.pad(c_all, ((0, pad), (0, 0))), w_ada, b_ada)
    sh1p, sc1p, ga1p, sh2p, sc2p, ga2p = _mods(mod[:B])
    sh1s, sc1s, ga1s, sh2s, sc2s, ga2s = [m[None] for m in jnp.split(mod[B:B + S], 6, axis=-1)]

    w_perm = jnp.concatenate(
        [w_in[:, :COL_ZA], w_in[:, COL_ZA + G_LOWRANK:], w_in[:, COL_ZA:COL_ZA + G_LOWRANK],
         jnp.zeros((D_MODEL, PROJ_W - COL_LR - G_LOWRANK), F32)], axis=1).astype(BF16)
    wa, wb, wo = w_br_a.astype(BF16), w_br_b.astype(BF16), w_out.astype(BF16)

    xs3 = xs.reshape(1, S, D_MODEL)
    proj_p = _inproj(xp, sc1p, sh1p, g_norm_mix, w_perm, tm=1024)
    proj_s = _inproj(xs3, sc1s, sh1s, g_norm_mix, w_perm, tm=S)

    a_p = _moba_prompt(proj_p)
    og_p, s_p = _gla_prompt(proj_p, w_gk2, b_gk2, g_gla_norm)
    ps = proj_s[0]
    k_s = ps[:, COL_AK:COL_AK + A_KVW]
    v_s = ps[:, COL_AV:COL_AV + A_KVW]
    a_s = _moba_sample(ps[:, :A_QW], k_s, v_s, cache_k, cache_v, page_table)
    og_s, s_s = _gla_sample(proj_s.reshape(S, 1, PROJ_W), state, w_gk2, b_gk2, g_gla_norm)

    cnt0 = jnp.zeros((1, LANES), F32)
    x1p, h2p, rtp, cnt1 = _merge_router(xp, a_p, og_p, proj_p, ga1p, sc2p, sh2p, g_norm_ffn, wa, wb, wo,
                                        w_router, b_router, cnt0, tm=512)
    x1s, h2s, rts, cnt2 = _merge_router(xs3, a_s.astype(BF16)[None], og_s.reshape(1, S, -1).astype(BF16), proj_s,
                                        ga1s, sc2s, sh2s, g_norm_ffn, wa, wb, wo, w_router, b_router, cnt1, tm=S)

    n_p = B * T
    n = n_p + S
    h2 = jnp.concatenate([h2p.reshape(n_p, D_MODEL), h2s.reshape(S, D_MODEL)], axis=0)
    route = jnp.concatenate([rtp.reshape(n_p, LANES), rts.reshape(S, LANES)], axis=0)
    order, blk_e, base_idx, valid, n_used = _routing(route, cnt2[0, :N_EXPERTS])
    yt = _moe_experts(h2, order, blk_e, base_idx, valid, n_used, w_gu, b_gu, w_down, b_down)
    plane = n
    y_p = _combine(yt, plane, x1p, rtp, ga2p, g_final, tok_off=0)
    y_s = _combine(yt, plane, x1s, rts, ga2s, g_final, tok_off=n_p)

    kp = proj_p[:, :, COL_AK:COL_AK + A_KVW].reshape(B, T, A_KV_HEADS, A_HEAD_DIM)
    vp = proj_p[:, :, COL_AV:COL_AV + A_KVW].reshape(B, T, A_KV_HEADS, A_HEAD_DIM)
    ks = k_s.reshape(S, 1, A_KV_HEADS, A_HEAD_DIM)
    vs = v_s.reshape(S, 1, A_KV_HEADS, A_HEAD_DIM)
    return y_p, y_s.reshape(S, 1, D_MODEL), kp, vp, s_p, ks, vs, s_s


def kernel(x_prompt, x_sample, c_prompt, c_sample, cache_k, cache_v, state_gla, page_table, w_ada, b_ada, g_norm_mix, w_in, w_gk2, b_gk2, g_gla_norm, w_br_a, w_br_b, w_out, g_norm_ffn, w_router, b_router, w_gu, b_gu, w_down, b_down, g_final):
    assert w_ada.shape[0] == 1, "single-layer step"
    outs = _layer(x_prompt, x_sample.reshape(x_sample.shape[0], D_MODEL), c_prompt, c_sample, cache_k[0],
                  cache_v[0], state_gla[0], page_table, w_ada[0], b_ada[0], g_norm_mix[0], w_in[0], w_gk2[0],
                  b_gk2[0], g_gla_norm[0], w_br_a[0], w_br_b[0], w_out[0], g_norm_ffn[0], w_router[0],
                  b_router[0], w_gu[0], b_gu[0], w_down[0], b_down[0], g_final)
    y_p, y_s, kp, vp, s_p, ks, vs, s_s = outs
    return (y_p, y_s, kp[None], vp[None], s_p[None], ks[None], vs[None], s_s[None])
```

```python
import functools

import jax
import jax.numpy as jnp
from jax import lax
from jax.experimental import pallas as pl
from jax.experimental.pallas import tpu as pltpu

F32 = jnp.float32
BF16 = jnp.bfloat16

D_MODEL = 1024
A_HEADS = 8
A_KV_HEADS = 4
A_GROUP = A_HEADS // A_KV_HEADS
A_HEAD_DIM = 64
A_QW = A_HEADS * A_HEAD_DIM
A_KVW = A_KV_HEADS * A_HEAD_DIM
A_BLOCK = 256
A_TOPK = 3
PAGE_SIZE = 128
G_HEADS = 4
G_KEY_DIM = D_MODEL // 2
G_VAL_DIM = D_MODEL
G_DK = G_KEY_DIM // G_HEADS
G_DV = G_VAL_DIM // G_HEADS
G_LOWRANK = 16
G_NORMALIZER = 16.0
G_CHUNK = 64
N_EXPERTS = 32
TOP_K = 4
D_FF = D_MODEL
SWIGLU_LIMIT = 7.0
SWIGLU_ALPHA = 1.702
EPS = 1e-6

LANES = 128
COL_AQ = 0
COL_AK = COL_AQ + A_QW
COL_AV = COL_AK + A_KVW
COL_GQ = COL_AV + A_KVW
COL_GK = COL_GQ + G_KEY_DIM
COL_GV = COL_GK + G_KEY_DIM
COL_GG = COL_GV + G_VAL_DIM
COL_ZA = COL_GG + G_VAL_DIM
COL_ZB = COL_ZA + D_MODEL
COL_LR = COL_ZB + D_MODEL
PROJ_TN = 1280
PROJ_W = 5 * PROJ_TN
assert PROJ_W >= COL_LR + LANES
NEG_BIG = -1e30
VMEM_LIMIT = 56 * 1024 * 1024

NT_DIMS = (((1,), (1,)), ((), ()))
NN_DIMS = (((1,), (0,)), ((), ()))


def _params(sem):
    return pltpu.CompilerParams(dimension_semantics=sem, vmem_limit_bytes=VMEM_LIMIT)


def _split(a):
    hi = a.astype(BF16)
    lo = (a - hi.astype(F32)).astype(BF16)
    return hi, lo


def _dot3(a, b, dims=NN_DIMS):
    ah, al = _split(a)
    bh, bl = _split(b)
    d = lambda x, y: lax.dot_general(x, y, dims, preferred_element_type=F32)
    return d(ah, bh) + d(al, bh) + d(ah, bl)


def _sigmoid(x):
    return 1.0 / (1.0 + jnp.exp(-x))


def _log_sigmoid(x):
    return -(jnp.maximum(-x, 0.0) + jnp.log(1.0 + jnp.exp(-jnp.abs(x))))


def _rmsnorm(x, g):
    return x * lax.rsqrt(jnp.mean(x * x, axis=-1, keepdims=True) + EPS) * g


def _top_cols(score, colf, n_pick):
    vals, cols = [], []
    big = float(score.shape[1])
    for _ in range(n_pick):
        mx = jnp.max(score, axis=1, keepdims=True)
        first = jnp.min(jnp.where(score == mx, colf, big), axis=1, keepdims=True)
        vals.append(mx)
        cols.append(first)
        score = jnp.where(colf == first, -jnp.inf, score)
    return vals, cols


def _ada_kernel(c_ref, w_ref, b_ref, o_ref):
    c = c_ref[...]
    o_ref[...] = _dot3(c * _sigmoid(c), w_ref[...]) + b_ref[...]


def _ada_mod(c, w_ada, b_ada):
    rows = c.shape[0]
    tn = 1536
    return pl.pallas_call(
        _ada_kernel,
        grid=(6 * D_MODEL // tn,),
        in_specs=[
            pl.BlockSpec((rows, D_MODEL), lambda j: (0, 0)),
            pl.BlockSpec((D_MODEL, tn), lambda j: (0, j)),
            pl.BlockSpec((1, tn), lambda j: (0, j)),
        ],
        out_specs=pl.BlockSpec((rows, tn), lambda j: (0, j)),
        out_shape=jax.ShapeDtypeStruct((rows, 6 * D_MODEL), F32),
        compiler_params=_params(("arbitrary",)),
        name="ada_mod",
    )(c, w_ada, b_ada.reshape(1, -1))


def _inproj_kernel(x_ref, sc_ref, sh_ref, g_ref, w_ref, o_ref, h_ref):
    @pl.when(pl.program_id(2) == 0)
    def _():
        y = _rmsnorm(x_ref[0], g_ref[...])
        h_ref[...] = (y * (1.0 + sc_ref[0]) + sh_ref[0]).astype(BF16)

    o_ref[0] = jnp.dot(h_ref[...], w_ref[...], preferred_element_type=F32)


def _inproj(x, sc, sh, g, w_perm, tm):
    G, T, _ = x.shape
    R = sc.shape[1]
    rm = 1 if R == 1 else tm
    mod_map = (lambda b, i, j: (b, 0, 0)) if R == 1 else (lambda b, i, j: (b, i, 0))
    return pl.pallas_call(
        _inproj_kernel,
        grid=(G, T // tm, PROJ_W // PROJ_TN),
        in_specs=[
            pl.BlockSpec((1, tm, D_MODEL), lambda b, i, j: (b, i, 0)),
            pl.BlockSpec((1, rm, D_MODEL), mod_map),
            pl.BlockSpec((1, rm, D_MODEL), mod_map),
            pl.BlockSpec((1, D_MODEL), lambda b, i, j: (0, 0)),
            pl.BlockSpec((D_MODEL, PROJ_TN), lambda b, i, j: (0, j)),
        ],
        out_specs=pl.BlockSpec((1, tm, PROJ_TN), lambda b, i, j: (b, i, j)),
        out_shape=jax.ShapeDtypeStruct((G, T, PROJ_W), F32),
        scratch_shapes=[pltpu.VMEM((tm, D_MODEL), BF16)],
        compiler_params=_params(("arbitrary", "arbitrary", "arbitrary")),
        name="inproj",
    )(x, sc, sh, g.reshape(1, -1), w_perm)


PREP_T = 2048
AUG_NB = 32
AUG_REST = LANES - A_HEAD_DIM - AUG_NB
KV_GROUP = 4
Q_BLOCKS = 2
Q_TILE = Q_BLOCKS * A_BLOCK


def _moba_prep_kernel(k_ref, v_ref, ka_ref, vt_ref, km_ref):
    t = pl.program_id(1)
    k = k_ref[0]
    v = v_ref[0]
    nblk = PREP_T // A_BLOCK
    km_ref[0] = jnp.mean(k.reshape(nblk, A_BLOCK, A_KVW), axis=1)
    half = LANES - A_HEAD_DIM
    row = lax.broadcasted_iota(jnp.int32, (A_BLOCK, half), 0)
    col = lax.broadcasted_iota(jnp.int32, (A_BLOCK, half), 1)
    vextra = jnp.where(col == 0, 1.0, 0.0)
    for jb in range(nblk):
        blk = t * nblk + jb
        kextra = jnp.where(col == blk, 1.0, 0.0)
        kextra = jnp.where((col == AUG_NB) | (col == AUG_NB + 1), 1.0, kextra)
        kextra = jnp.where(col == AUG_NB + 2, row.astype(F32), kextra)
        kextra = jnp.where(col == AUG_NB + 3, (blk * A_BLOCK).astype(F32), kextra)
        rows = slice(jb * A_BLOCK, (jb + 1) * A_BLOCK)
        for h in range(A_KV_HEADS):
            cols = slice(h * A_HEAD_DIM, (h + 1) * A_HEAD_DIM)
            ka_ref[0, h, jb] = jnp.concatenate([k[rows, cols], kextra], axis=1).astype(BF16)
            vt_ref[0, h, jb] = jnp.concatenate([v[rows, cols], vextra], axis=1).T.astype(BF16)


def _moba_prep(proj):
    B, T, _ = proj.shape
    nblk = PREP_T // A_BLOCK
    nb = T // A_BLOCK
    assert nb <= AUG_NB
    return pl.pallas_call(
        _moba_prep_kernel,
        grid=(B, T // PREP_T),
        in_specs=[
            pl.BlockSpec((1, PREP_T, A_KVW), lambda b, t: (b, t, COL_AK // A_KVW)),
            pl.BlockSpec((1, PREP_T, A_KVW), lambda b, t: (b, t, COL_AV // A_KVW)),
        ],
        out_specs=[
            pl.BlockSpec((1, A_KV_HEADS, nblk, A_BLOCK, LANES), lambda b, t: (b, 0, t, 0, 0)),
            pl.BlockSpec((1, A_KV_HEADS, nblk, LANES, A_BLOCK), lambda b, t: (b, 0, t, 0, 0)),
            pl.BlockSpec((1, nblk, A_KVW), lambda b, t: (b, t, 0)),
        ],
        out_shape=[
            jax.ShapeDtypeStruct((B, A_KV_HEADS, nb, A_BLOCK, LANES), BF16),
            jax.ShapeDtypeStruct((B, A_KV_HEADS, nb, LANES, A_BLOCK), BF16),
            jax.ShapeDtypeStruct((B, nb, A_KVW), F32),
        ],
        compiler_params=_params(("arbitrary", "arbitrary")),
        name="moba_prep",
    )(proj, proj)


def _slope_of(head_plus_one, shape):
    bits = (127 - head_plus_one) << 23
    if shape is not None:
        bits = jnp.full(shape, bits, jnp.int32)
    return lax.bitcast_convert_type(bits, F32)


def _top_rows(score, rowf, n_pick, n_valid):
    picked = jnp.zeros(score.shape, jnp.bool_)
    big = float(score.shape[0])
    for r in range(n_pick):
        mx = jnp.max(score, axis=0, keepdims=True)
        first = jnp.min(jnp.where(score == mx, rowf, big), axis=0, keepdims=True)
        hit = rowf == first
        picked = picked | (hit & ((jnp.zeros_like(rowf) + n_valid) > r))
        score = jnp.where(hit, -jnp.inf, score)
    return picked


def _moba_attn_kernel(q_ref, km_ref, ka_ref, vt_ref, o_ref, qa_ref, qo_ref, m_ref, acc_ref, sa_ref, sb_ref):
    kvh = pl.program_id(1)
    t = pl.program_id(2)
    nb = km_ref.shape[2]
    qt = q_ref[0].T
    km = km_ref[0, 0]
    wide = A_GROUP * Q_TILE
    q2 = jnp.concatenate([qt[hh * A_HEAD_DIM:(hh + 1) * A_HEAD_DIM, :] for hh in range(A_GROUP)], axis=1)
    rowf = lax.broadcasted_iota(jnp.int32, (nb, wide), 0).astype(F32)
    lane_nb = lax.broadcasted_iota(jnp.int32, (nb, wide), 1)
    qblk_nb = (t * Q_BLOCKS + (lane_nb % Q_TILE) // A_BLOCK).astype(F32)
    gate = jnp.where(rowf < qblk_nb, _dot3(km, q2), -jnp.inf)
    selbias = jnp.where(_top_rows(gate, rowf, A_TOPK, qblk_nb), 0.0, NEG_BIG)
    if nb < AUG_NB:
        selbias = jnp.concatenate([selbias, jnp.zeros((AUG_NB - nb, wide), F32)], axis=0)
    erow = lax.broadcasted_iota(jnp.int32, (AUG_REST, wide), 0)
    lane = lax.broadcasted_iota(jnp.int32, (AUG_REST, wide), 1)
    qq = (lane % A_BLOCK).astype(F32)
    qstart = ((t * Q_BLOCKS + (lane % Q_TILE) // A_BLOCK) * A_BLOCK).astype(F32)
    slope = _slope_of(kvh * A_GROUP + lane // Q_TILE + 1, None)
    alibi = jnp.where(erow == 0, -slope * qq, 0.0)
    alibi = jnp.where(erow == 1, -slope * qstart, alibi)
    alibi = jnp.where((erow == 2) | (erow == 3), slope, alibi)
    qs = q2 * (A_HEAD_DIM ** -0.5)
    qa2 = jnp.concatenate([qs, selbias, alibi], axis=0).astype(BF16)
    qo2 = jnp.concatenate([qs, jnp.zeros((AUG_NB, wide), F32), alibi], axis=0).astype(BF16)
    for hh in range(A_GROUP):
        qa_ref[hh] = qa2[:, hh * Q_TILE:(hh + 1) * Q_TILE]
        qo_ref[hh] = qo2[:, hh * Q_TILE:(hh + 1) * Q_TILE]

    for j in range(KV_GROUP):
        for hh in range(A_GROUP):
            sa_ref[hh, j] = jnp.dot(ka_ref[0, 0, j], qa_ref[hh], preferred_element_type=F32)

    key = lax.broadcasted_iota(jnp.int32, (A_BLOCK, A_BLOCK), 0)
    qry = lax.broadcasted_iota(jnp.int32, (A_BLOCK, A_BLOCK), 1)
    for e in range(Q_BLOCKS):
        kb = ka_ref[0, 0, t * Q_BLOCKS + e]
        vb = vt_ref[0, 0, t * Q_BLOCKS + e]
        cols = slice(e * A_BLOCK, (e + 1) * A_BLOCK)
        for hh in range(A_GROUP):
            s = jnp.dot(kb, qo_ref[hh, :, cols], preferred_element_type=F32)
            s = jnp.where(key <= qry, s, NEG_BIG)
            m = jnp.max(s, axis=0, keepdims=True)
            p = jnp.exp(s - m)
            m_ref[hh, :, cols] = m
            acc_ref[hh, :, cols] = jnp.dot(vb, p.astype(BF16), preferred_element_type=F32)

    n_groups = (t * Q_BLOCKS + Q_BLOCKS - 1 + KV_GROUP - 1) // KV_GROUP

    def scores(g, s_ref):
        for j in range(KV_GROUP):
            kj = ka_ref[0, 0, g * KV_GROUP + j]
            for hh in range(A_GROUP):
                s_ref[hh, j] = jnp.dot(kj, qa_ref[hh], preferred_element_type=F32)

    def accumulate(g, s_ref):
        for hh in range(A_GROUP):
            m_old = m_ref[hh]
            m_new = m_old
            for j in range(KV_GROUP):
                m_new = jnp.maximum(m_new, jnp.max(s_ref[hh, j], axis=0, keepdims=True))
            acc = jnp.exp(m_old - m_new) * acc_ref[hh]
            for j in range(KV_GROUP):
                p = jnp.exp(s_ref[hh, j] - m_new).astype(BF16)
                acc = acc + jnp.dot(vt_ref[0, 0, g * KV_GROUP + j], p, preferred_element_type=F32)
            acc_ref[hh] = acc
            m_ref[hh] = m_new

    def on_parity(g, fn):
        @pl.when(g % 2 == 0)
        def _():
            fn(sa_ref, sb_ref)

        @pl.when(g % 2 == 1)
        def _():
            fn(sb_ref, sa_ref)

    def body(g, carry):
        def step(cur, nxt):
            scores(g + 1, nxt)
            accumulate(g, cur)

        on_parity(g, step)
        return carry

    lax.fori_loop(0, n_groups - 1, body, 0)
    on_parity(n_groups - 1, lambda cur, nxt: accumulate(n_groups - 1, cur))

    outs = []
    for hh in range(A_GROUP):
        acc = acc_ref[hh]
        outs.append((acc[:A_HEAD_DIM] / acc[A_HEAD_DIM:A_HEAD_DIM + 1]).T)
    o_ref[0] = jnp.concatenate(outs, axis=1).astype(BF16)


def _moba_prompt(proj):
    B, T, _ = proj.shape
    nb = T // A_BLOCK
    assert nb % KV_GROUP == 0 and nb % Q_BLOCKS == 0 and Q_BLOCKS >= 2
    kaug, vaug_t, km = _moba_prep(proj)
    km = km.reshape(B, nb, A_KV_HEADS, A_HEAD_DIM).transpose(0, 2, 1, 3)
    qw = A_GROUP * A_HEAD_DIM
    return pl.pallas_call(
        _moba_attn_kernel,
        grid=(B, A_KV_HEADS, nb // Q_BLOCKS),
        in_specs=[
            pl.BlockSpec((1, Q_TILE, qw), lambda b, h, i: (b, i, h)),
            pl.BlockSpec((1, 1, nb, A_HEAD_DIM), lambda b, h, i: (b, h, 0, 0)),
            pl.BlockSpec((1, 1, nb, A_BLOCK, LANES), lambda b, h, i: (b, h, 0, 0, 0)),
            pl.BlockSpec((1, 1, nb, LANES, A_BLOCK), lambda b, h, i: (b, h, 0, 0, 0)),
        ],
        out_specs=pl.BlockSpec((1, Q_TILE, qw), lambda b, h, i: (b, i, h)),
        out_shape=jax.ShapeDtypeStruct((B, T, A_QW), BF16),
        scratch_shapes=[
            pltpu.VMEM((A_GROUP, LANES, Q_TILE), BF16),
            pltpu.VMEM((A_GROUP, LANES, Q_TILE), BF16),
            pltpu.VMEM((A_GROUP, 1, Q_TILE), F32),
            pltpu.VMEM((A_GROUP, LANES, Q_TILE), F32),
            pltpu.VMEM((A_GROUP, KV_GROUP, A_BLOCK, Q_TILE), F32),
            pltpu.VMEM((A_GROUP, KV_GROUP, A_BLOCK, Q_TILE), F32),
        ],
        compiler_params=_params(("arbitrary", "arbitrary", "arbitrary")),
        name="moba_attn",
    )(proj, km, kaug, vaug_t)


GLA_T = 1024
GLA_UNROLL = 8


def _gla_kernel(q_ref, k_ref, v_ref, gg_ref, lr_ref, w2_ref, b2_ref, gn_ref, og_ref, st_ref, s_ref, g_ref):
    t = pl.program_id(2)

    @pl.when(t == 0)
    def _():
        s_ref[...] = jnp.zeros_like(s_ref)

    z = _dot3(lr_ref[0][:, :G_LOWRANK], w2_ref[...]) + b2_ref[...]
    g_ref[...] = _log_sigmoid(z) * (1.0 / G_NORMALIZER)
    C = G_CHUNK
    rr = lax.broadcasted_iota(jnp.int32, (C, C), 0)
    cc = lax.broadcasted_iota(jnp.int32, (C, C), 1)
    causal = cc <= rr
    tril = jnp.where(causal, 1.0, 0.0).astype(BF16)
    er = lax.broadcasted_iota(jnp.int32, (G_DK, G_DK), 0)
    ec = lax.broadcasted_iota(jnp.int32, (G_DK, G_DK), 1)
    eye = er == ec
    gn = gn_ref[...]

    def group(gi, s):
        sls = [pl.ds(pl.multiple_of((gi * GLA_UNROLL + c) * C, C), C) for c in range(GLA_UNROLL)]
        cums = []
        for sl in sls:
            gh, gl = _split(g_ref[sl, :])
            cums.append(jnp.dot(tril, gh, preferred_element_type=F32) + jnp.dot(tril, gl, preferred_element_type=F32))
        parts = []
        for sl, cum in zip(sls, cums):
            last = cum[C - 1:C, :]
            kc = k_ref[0, sl, :]
            vc = v_ref[0, sl, :].astype(BF16)
            qe = (q_ref[0, sl, :] * (G_DK ** -0.5) * jnp.exp(cum)).astype(BF16)
            ke = (kc * jnp.exp(-cum)).astype(BF16)
            kd = kc * jnp.exp(last - cum)
            att = lax.dot_general(qe, ke, NT_DIMS, preferred_element_type=F32)
            att = jnp.where(causal, att, 0.0).astype(BF16)
            intra = jnp.dot(att, vc, preferred_element_type=F32)
            inc = jnp.dot(kd.T.astype(BF16), vc, preferred_element_type=F32)
            decay = jnp.sum(jnp.where(eye, jnp.exp(last), 0.0), axis=1, keepdims=True)
            parts.append((qe, intra, inc, decay))
        for sl, (qe, intra, inc, decay) in zip(sls, parts):
            o = jnp.dot(qe, s.astype(BF16), preferred_element_type=F32) + intra
            s = s * decay + inc
            on = _rmsnorm(o, gn)
            gate = gg_ref[0, sl, :]
            og_ref[0, sl, :] = (on * (gate * _sigmoid(gate))).astype(BF16)
        return s

    s_ref[...] = lax.fori_loop(0, GLA_T // (C * GLA_UNROLL), group, s_ref[...])

    @pl.when(t == pl.num_programs(2) - 1)
    def _():
        st_ref[0, 0] = s_ref[...]


def _gla_prompt(proj, w_gk2, b_gk2, g_gla_norm):
    B, T, _ = proj.shape
    tt = min(GLA_T, T)
    assert tt == GLA_T
    return pl.pallas_call(
        _gla_kernel,
        grid=(B, G_HEADS, T // tt),
        in_specs=[
            pl.BlockSpec((1, tt, G_DK), lambda b, h, t: (b, t, COL_GQ // G_DK + h)),
            pl.BlockSpec((1, tt, G_DK), lambda b, h, t: (b, t, COL_GK // G_DK + h)),
            pl.BlockSpec((1, tt, G_DV), lambda b, h, t: (b, t, COL_GV // G_DV + h)),
            pl.BlockSpec((1, tt, G_DV), lambda b, h, t: (b, t, COL_GG // G_DV + h)),
            pl.BlockSpec((1, tt, LANES), lambda b, h, t: (b, t, COL_LR // LANES)),
            pl.BlockSpec((G_LOWRANK, G_DK), lambda b, h, t: (0, h)),
            pl.BlockSpec((1, G_DK), lambda b, h, t: (0, h)),
            pl.BlockSpec((1, G_DV), lambda b, h, t: (0, 0)),
        ],
        out_specs=[
            pl.BlockSpec((1, tt, G_DV), lambda b, h, t: (b, t, h)),
            pl.BlockSpec((1, 1, G_DK, G_DV), lambda b, h, t: (b, h, 0, 0)),
        ],
        out_shape=[
            jax.ShapeDtypeStruct((B, T, G_VAL_DIM), BF16),
            jax.ShapeDtypeStruct((B, G_HEADS, G_DK, G_DV), F32),
        ],
        scratch_shapes=[pltpu.VMEM((G_DK, G_DV), F32), pltpu.VMEM((tt, G_DK), F32)],
        compiler_params=_params(("arbitrary", "arbitrary", "arbitrary")),
        name="gla_prompt",
    )(proj, proj, proj, proj, proj, w_gk2, b_gk2.reshape(1, -1), g_gla_norm.reshape(1, -1))


PPB = A_BLOCK // PAGE_SIZE


def _smoba_kernel(pt_ref, qb_ref, q8_ref, kn_ref, vn_ref, ck_ref, cv_ref, o_ref,
                  kbuf, vbuf, lg_ref, p_ref, pown_ref, idx_v, idx_s, ksem, vsem, isem):
    s = pl.program_id(0)
    n_seq = pl.num_programs(0) - 1
    n_pages = kbuf.shape[1]
    nb = n_pages // PPB
    past = n_pages * PAGE_SIZE
    hd = A_HEAD_DIM

    def start_keys(seq, slot):
        for p in range(n_pages):
            pltpu.make_async_copy(ck_ref.at[pt_ref[seq, p]], kbuf.at[slot, p], ksem.at[slot]).start()

    def value_copy(seq, slot, h, r, e):
        page = pt_ref[seq, idx_s[slot, h, r] * PPB + e]
        return pltpu.make_async_copy(cv_ref.at[page, pl.ds((h // A_GROUP) * hd, hd), :],
                                     vbuf.at[slot, h, r, e], vsem.at[slot])

    @pl.when(s == 0)
    def _():
        start_keys(0, 0)

    @pl.when(s + 1 < n_seq)
    def _():
        start_keys(s + 1, (s + 1) % 2)

    @pl.when(s < n_seq)
    def _():
        slot = s % 2
        pltpu.make_async_copy(ck_ref.at[pl.ds(0, n_pages)], kbuf.at[slot], ksem.at[slot]).wait()

        def page_logits(p, carry):
            kp = kbuf[slot, p]
            rows = []
            for h in range(A_HEADS):
                g = h // A_GROUP
                rows.append(jnp.sum(qb_ref[0, h] * kp[g * hd:(g + 1) * hd, :], axis=0, keepdims=True))
            lg_ref[slot, p] = jnp.concatenate(rows, axis=0)
            return carry

        lax.fori_loop(0, n_pages, page_logits, 0)
        lg = lg_ref[slot]
        gate = jnp.sum(jnp.sum(lg.reshape(nb, PPB, A_HEADS, PAGE_SIZE), axis=1), axis=2, keepdims=True)
        bidx = lax.broadcasted_iota(jnp.int32, (nb, A_HEADS, 1), 0).astype(F32)
        picked = jnp.zeros((nb, A_HEADS, 1), F32)
        firsts = []
        for r in range(A_TOPK):
            mx = jnp.max(gate, axis=0, keepdims=True)
            first = jnp.min(jnp.where(gate == mx, bidx, float(nb)), axis=0, keepdims=True)
            hit = bidx == first
            picked = jnp.where(hit, 1.0, picked)
            gate = jnp.where(hit, -jnp.inf, gate)
            firsts.append(first.reshape(A_HEADS, 1))
        sel = jnp.broadcast_to(picked[:, None], (nb, PPB, A_HEADS, 1)).reshape(n_pages, A_HEADS, 1)
        head1 = lax.broadcasted_iota(jnp.int32, (1, A_HEADS, 1), 1) + 1
        slope = lax.bitcast_convert_type((127 - head1) << 23, F32)
        tpos = (lax.broadcasted_iota(jnp.int32, (n_pages, 1, PAGE_SIZE), 0) * PAGE_SIZE
                + lax.broadcasted_iota(jnp.int32, (n_pages, 1, PAGE_SIZE), 2))
        dist = (past - tpos).astype(F32)
        att = jnp.where(sel > 0.5, lg * (hd ** -0.5) - slope * dist, NEG_BIG)
        own = jnp.sum(q8_ref[0] * kn_ref[0], axis=1, keepdims=True) * (hd ** -0.5)
        m = jnp.maximum(jnp.max(jnp.max(att, axis=0), axis=1, keepdims=True), own)
        p = jnp.exp(att - m.reshape(1, A_HEADS, 1))
        p_own = jnp.exp(own - m)
        inv = 1.0 / (jnp.sum(jnp.sum(p, axis=0), axis=1, keepdims=True) + p_own)
        p_ref[slot] = p * inv.reshape(1, A_HEADS, 1)
        pown_ref[slot] = jnp.broadcast_to(p_own * inv, (A_HEADS, LANES))
        lane = lax.broadcasted_iota(jnp.int32, (A_HEADS, LANES), 1)
        it = jnp.zeros((A_HEADS, LANES), F32)
        for r in range(A_TOPK):
            it = jnp.where(lane == r, firsts[r], it)
        idx_v[...] = it.astype(jnp.int32)
        to_smem = pltpu.make_async_copy(idx_v, idx_s.at[slot], isem)
        to_smem.start()
        to_smem.wait()
        for h in range(A_HEADS):
            for r in range(A_TOPK):
                for e in range(PPB):
                    value_copy(s, slot, h, r, e).start()

    @pl.when(s >= 1)
    def _():
        sb = s - 1
        b = sb % 2
        for h in range(A_HEADS):
            for r in range(A_TOPK):
                for e in range(PPB):
                    value_copy(sb, b, h, r, e).wait()
        lane = lax.broadcasted_iota(jnp.int32, (hd, LANES), 1)
        cols = jnp.zeros((hd, LANES), F32)
        for h in range(A_HEADS):
            acc = jnp.zeros((hd, PAGE_SIZE), F32)
            for r in range(A_TOPK):
                for e in range(PPB):
                    pg = idx_s[b, h, r] * PPB + e
                    acc = acc + vbuf[b, h, r, e] * p_ref[b, pg, h:h + 1, :]
            cols = jnp.where(lane == h, jnp.sum(acc, axis=1, keepdims=True), cols)
        o_ref[0] = cols.T[:A_HEADS, :] + pown_ref[b][:, :hd] * vn_ref[0]


def _moba_sample(q, k_new, v_new, cache_k, cache_v, page_table):
    S = q.shape[0]
    n_pages = page_table.shape[1]
    n_pool = cache_k.shape[0]
    hd = A_HEAD_DIM
    ck = cache_k.transpose(0, 2, 3, 1).reshape(n_pool, A_KVW, PAGE_SIZE)
    cv = cache_v.transpose(0, 2, 3, 1).reshape(n_pool, A_KVW, PAGE_SIZE)
    q8 = q.reshape(S, A_HEADS, hd)
    qb = jnp.broadcast_to(q8[..., None], (S, A_HEADS, hd, LANES))
    kn8 = jnp.repeat(k_new.reshape(S, A_KV_HEADS, hd), A_GROUP, axis=1)
    vn8 = jnp.repeat(v_new.reshape(S, A_KV_HEADS, hd), A_GROUP, axis=1)
    seq = lambda s, pt: (jnp.minimum(s, S - 1), 0, 0)
    out = pl.pallas_call(
        _smoba_kernel,
        grid_spec=pltpu.PrefetchScalarGridSpec(
            num_scalar_prefetch=1,
            grid=(S + 1,),
            in_specs=[
                pl.BlockSpec((1, A_HEADS, hd, LANES), lambda s, pt: (jnp.minimum(s, S - 1), 0, 0, 0)),
                pl.BlockSpec((1, A_HEADS, hd), seq),
                pl.BlockSpec((1, A_HEADS, hd), seq),
                pl.BlockSpec((1, A_HEADS, hd), lambda s, pt: (jnp.maximum(s - 1, 0), 0, 0)),
                pl.BlockSpec(memory_space=pl.ANY),
                pl.BlockSpec(memory_space=pl.ANY),
            ],
            out_specs=pl.BlockSpec((1, A_HEADS, hd), lambda s, pt: (jnp.maximum(s - 1, 0), 0, 0)),
            scratch_shapes=[
                pltpu.VMEM((2, n_pages, A_KVW, PAGE_SIZE), F32),
                pltpu.VMEM((2, A_HEADS, A_TOPK, PPB, hd, PAGE_SIZE), F32),
                pltpu.VMEM((2, n_pages, A_HEADS, PAGE_SIZE), F32),
                pltpu.VMEM((2, n_pages, A_HEADS, PAGE_SIZE), F32),
                pltpu.VMEM((2, A_HEADS, LANES), F32),
                pltpu.VMEM((A_HEADS, LANES), jnp.int32),
                pltpu.SMEM((2, A_HEADS, LANES), jnp.int32),
                pltpu.SemaphoreType.DMA((2,)),
                pltpu.SemaphoreType.DMA((2,)),
                pltpu.SemaphoreType.DMA(()),
            ],
        ),
        out_shape=jax.ShapeDtypeStruct((S, A_HEADS, hd), F32),
        compiler_params=_params(("arbitrary",)),
        name="moba_sample",
    )(page_table, qb, q8, kn8, vn8, ck, cv)
    return out.reshape(S, A_QW)


def _sgla_kernel(p_ref, s0_ref, w2_ref, b2_ref, gn_ref, og_ref, s1_ref):
    row = p_ref[0]
    er = lax.broadcasted_iota(jnp.int32, (G_DK, G_DK), 0)
    ec = lax.broadcasted_iota(jnp.int32, (G_DK, G_DK), 1)
    eye = er == ec
    col_of = lambda r: jnp.sum(jnp.where(eye, r, 0.0), axis=1, keepdims=True)
    lr = row[:, COL_LR:COL_LR + G_LOWRANK]
    z = _dot3(lr, w2_ref[...]) + b2_ref[...]
    g_all = _log_sigmoid(z) * (1.0 / G_NORMALIZER)
    outs = []
    for h in range(G_HEADS):
        g = g_all[:, h * G_DK:(h + 1) * G_DK]
        q = row[:, COL_GQ + h * G_DK:COL_GQ + (h + 1) * G_DK] * (G_DK ** -0.5)
        k = row[:, COL_GK + h * G_DK:COL_GK + (h + 1) * G_DK]
        v = row[:, COL_GV + h * G_DV:COL_GV + (h + 1) * G_DV]
        gate = row[:, COL_GG + h * G_DV:COL_GG + (h + 1) * G_DV]
        qe = q * jnp.exp(g)
        ke = k * jnp.exp(-g)
        att = jnp.sum(qe * ke, axis=1, keepdims=True)
        s0 = s0_ref[0, h]
        o = _dot3(qe, s0) + att * v
        s1_ref[0, h] = s0 * col_of(jnp.exp(g)) + col_of(k) * v
        on = _rmsnorm(o, gn_ref[...])
        outs.append(on * (gate * _sigmoid(gate)))
    og_ref[0] = jnp.concatenate(outs, axis=1)


def _gla_sample(proj_rows, state, w_gk2, b_gk2, g_gla_norm):
    S = proj_rows.shape[0]
    return pl.pallas_call(
        _sgla_kernel,
        grid=(S,),
        in_specs=[
            pl.BlockSpec((1, 1, PROJ_W), lambda s: (s, 0, 0)),
            pl.BlockSpec((1, G_HEADS, G_DK, G_DV), lambda s: (s, 0, 0, 0)),
            pl.BlockSpec((G_LOWRANK, G_KEY_DIM), lambda s: (0, 0)),
            pl.BlockSpec((1, G_KEY_DIM), lambda s: (0, 0)),
            pl.BlockSpec((1, G_DV), lambda s: (0, 0)),
        ],
        out_specs=[
            pl.BlockSpec((1, 1, G_VAL_DIM), lambda s: (s, 0, 0)),
            pl.BlockSpec((1, G_HEADS, G_DK, G_DV), lambda s: (s, 0, 0, 0)),
        ],
        out_shape=[
            jax.ShapeDtypeStruct((S, 1, G_VAL_DIM), F32),
            jax.ShapeDtypeStruct((S, G_HEADS, G_DK, G_DV), F32),
        ],
        compiler_params=_params(("arbitrary",)),
        name="gla_sample",
    )(proj_rows, state, w_gk2, b_gk2.reshape(1, -1), g_gla_norm.reshape(1, -1))


ROUTE_E = 0
ROUTE_W = TOP_K
ROUTE_R = 2 * TOP_K


def _merge_kernel(x_ref, a_ref, og_ref, za_ref, zb_ref, ga1_ref, sc2_ref, sh2_ref, g2_ref, wa_ref, wb_ref,
                  wo_ref, wr_ref, br_ref, c0_ref, x1_ref, h2_ref, rt_ref, cnt_ref, run_ref):
    first = (pl.program_id(0) == 0) & (pl.program_id(1) == 0)

    @pl.when(first)
    def _():
        run_ref[...] = c0_ref[...]

    pa = jnp.dot(a_ref[0], wa_ref[...], preferred_element_type=F32)
    pb = jnp.dot(og_ref[0], wb_ref[...], preferred_element_type=F32)
    merged = _sigmoid(za_ref[0]) * pa + _sigmoid(zb_ref[0]) * pb
    mix = jnp.dot(merged.astype(BF16), wo_ref[...], preferred_element_type=F32)
    x1 = x_ref[0] + ga1_ref[0] * mix
    x1_ref[0] = x1
    h2 = _rmsnorm(x1, g2_ref[...]) * (1.0 + sc2_ref[0]) + sh2_ref[0]
    h2_ref[0] = h2
    logits = _dot3(h2, wr_ref[...]) + br_ref[...]
    tm = logits.shape[0]
    colf = lax.broadcasted_iota(jnp.int32, (tm, N_EXPERTS), 1).astype(F32)
    vals, cols = _top_cols(logits, colf, TOP_K)
    ex = [jnp.exp(v - vals[0]) for v in vals]
    tot = ex[0] + ex[1] + ex[2] + ex[3]
    onehot = [jnp.where(colf == c, 1.0, 0.0) for c in cols]
    osum = onehot[0] + onehot[1] + onehot[2] + onehot[3]
    tr = lax.broadcasted_iota(jnp.int32, (tm, tm), 0)
    tc = lax.broadcasted_iota(jnp.int32, (tm, tm), 1)
    lower = jnp.where(tc < tr, 1.0, 0.0).astype(BF16)
    before = jnp.dot(lower, osum.astype(BF16), preferred_element_type=F32) + run_ref[:, :N_EXPERTS]
    lane = lax.broadcasted_iota(jnp.int32, (tm, LANES), 1)
    rec = jnp.zeros((tm, LANES), F32)
    for r in range(TOP_K):
        rank = jnp.sum(onehot[r] * before, axis=1, keepdims=True)
        rec = jnp.where(lane == ROUTE_E + r, cols[r], rec)
        rec = jnp.where(lane == ROUTE_W + r, ex[r] / tot, rec)
        rec = jnp.where(lane == ROUTE_R + r, rank, rec)
    rt_ref[0] = rec
    pad = jnp.zeros((1, LANES - N_EXPERTS), F32)
    run_ref[...] = run_ref[...] + jnp.concatenate([jnp.sum(osum, axis=0, keepdims=True), pad], axis=1)
    cnt_ref[...] = run_ref[...]


def _merge_router(x, a, og, proj, ga1, sc2, sh2, g2, wa, wb, wo, wr, br, cnt0, tm):
    G, T, _ = x.shape
    R = ga1.shape[1]
    rm = 1 if R == 1 else tm
    mod_map = (lambda b, i: (b, 0, 0)) if R == 1 else (lambda b, i: (b, i, 0))
    tok = lambda w: pl.BlockSpec((1, tm, w), lambda b, i: (b, i, 0))
    full = lambda r, c: pl.BlockSpec((r, c), lambda b, i: (0, 0))
    return pl.pallas_call(
        _merge_kernel,
        grid=(G, T // tm),
        in_specs=[
            tok(D_MODEL), tok(A_QW), tok(G_VAL_DIM),
            pl.BlockSpec((1, tm, D_MODEL), lambda b, i: (b, i, COL_ZA // D_MODEL)),
            pl.BlockSpec((1, tm, D_MODEL), lambda b, i: (b, i, COL_ZB // D_MODEL)),
            pl.BlockSpec((1, rm, D_MODEL), mod_map),
            pl.BlockSpec((1, rm, D_MODEL), mod_map),
            pl.BlockSpec((1, rm, D_MODEL), mod_map),
            full(1, D_MODEL), full(A_QW, D_MODEL), full(G_VAL_DIM, D_MODEL), full(D_MODEL, D_MODEL),
            full(D_MODEL, N_EXPERTS), full(1, N_EXPERTS), full(1, LANES),
        ],
        out_specs=[tok(D_MODEL), tok(D_MODEL), tok(LANES), full(1, LANES)],
        out_shape=[
            jax.ShapeDtypeStruct((G, T, D_MODEL), F32),
            jax.ShapeDtypeStruct((G, T, D_MODEL), F32),
            jax.ShapeDtypeStruct((G, T, LANES), F32),
            jax.ShapeDtypeStruct((1, LANES), F32),
        ],
        scratch_shapes=[pltpu.VMEM((1, LANES), F32)],
        compiler_params=_params(("arbitrary", "arbitrary")),
        name="merge_router",
    )(x, a, og, proj, proj, ga1, sc2, sh2, g2.reshape(1, -1), wa, wb, wo, wr, br.reshape(1, -1), cnt0)


MOE_ROWS = 256
DISP_T = 128


def _zero_unwritten(pe_ref, nu_ref, xs_ref, zbuf, zsem):
    zbuf[...] = jnp.zeros_like(zbuf)
    n_blocks = xs_ref.shape[0] // MOE_ROWS

    def block_copy(start):
        return pltpu.make_async_copy(zbuf, xs_ref.at[pl.ds(pl.multiple_of(start, MOE_ROWS), MOE_ROWS), :], zsem)

    todo = [(pe_ref[e + 1] > pe_ref[e], pe_ref[e + 1] - MOE_ROWS) for e in range(N_EXPERTS)]
    todo += [(n_blocks - 1 - u >= nu_ref[0], (n_blocks - 1 - u) * MOE_ROWS) for u in range(N_EXPERTS)]
    for cond, start in todo:
        @pl.when(cond)
        def _():
            block_copy(start).start()

    for cond, start in todo:
        @pl.when(cond)
        def _():
            block_copy(start).wait()


def _scatter_rows(dest_ref, h_ref, xs_ref, sem, first_tok):
    tm = h_ref.shape[0]
    base = first_tok * TOP_K

    def body(r, carry):
        for k in range(TOP_K):
            row = dest_ref[base + r * TOP_K + k]
            pltpu.make_async_copy(h_ref.at[pl.ds(r, 1), :], xs_ref.at[pl.ds(row, 1), :], sem).start()
        return carry

    lax.fori_loop(0, tm, body, 0)
    for k in range(TOP_K):
        pltpu.make_async_copy(h_ref, xs_ref.at[pl.ds(0, tm), :], sem).wait()


def _dispatch_kernel(dest_ref, pe_ref, nu_ref, hp_ref, hs_ref, xs_ref, sem, zbuf, zsem, *, steps_p):
    i = pl.program_id(0)

    @pl.when(i == 0)
    def _():
        _zero_unwritten(pe_ref, nu_ref, xs_ref, zbuf, zsem)

    @pl.when(i < steps_p)
    def _():
        _scatter_rows(dest_ref, hp_ref, xs_ref, sem, i * hp_ref.shape[0])

    @pl.when(i >= steps_p)
    def _():
        _scatter_rows(dest_ref, hs_ref, xs_ref, sem, steps_p * hp_ref.shape[0] + (i - steps_p) * hs_ref.shape[0])


def _dispatch(h2p, h2s, dest, pad_edges, n_used, rows):
    n_p, n_s = h2p.shape[0], h2s.shape[0]
    tp, ts = min(DISP_T, n_p), min(DISP_T, n_s)
    assert n_p % tp == 0 and n_s % ts == 0 and rows // MOE_ROWS >= N_EXPERTS
    steps_p, steps_s = n_p // tp, n_s // ts
    return pl.pallas_call(
        functools.partial(_dispatch_kernel, steps_p=steps_p),
        grid_spec=pltpu.PrefetchScalarGridSpec(
            num_scalar_prefetch=3,
            grid=(steps_p + steps_s,),
            in_specs=[
                pl.BlockSpec((tp, D_MODEL), lambda i, d, pe, nu: (jnp.minimum(i, steps_p - 1), 0)),
                pl.BlockSpec((ts, D_MODEL), lambda i, d, pe, nu: (jnp.clip(i - steps_p, 0, steps_s - 1), 0)),
            ],
            out_specs=pl.BlockSpec(memory_space=pl.ANY),
            scratch_shapes=[pltpu.SemaphoreType.DMA(()), pltpu.VMEM((MOE_ROWS, D_MODEL), F32),
                            pltpu.SemaphoreType.DMA(())],
        ),
        out_shape=jax.ShapeDtypeStruct((rows, D_MODEL), F32),
        compiler_params=_params(("arbitrary",)),
        name="dispatch",
    )(dest, pad_edges, n_used, h2p, h2s)


def _moe_kernel(be_ref, nu_ref, x_ref, wgu_ref, bgu_ref, wdn_ref, bdn_ref, y_ref, wgu_bf, wdn_bf):
    i = pl.program_id(0)

    @pl.when(i < nu_ref[0])
    def _():
        @pl.when((i == 0) | (be_ref[i] != be_ref[jnp.maximum(i - 1, 0)]))
        def _():
            wgu_bf[...] = wgu_ref[0].astype(BF16)
            wdn_bf[...] = wdn_ref[0].astype(BF16)

        gu = jnp.dot(x_ref[...].astype(BF16), wgu_bf[...], preferred_element_type=F32) + bgu_ref[0]
        g = jnp.minimum(gu[:, :D_FF], SWIGLU_LIMIT)
        u = jnp.clip(gu[:, D_FF:], -SWIGLU_LIMIT, SWIGLU_LIMIT)
        act = ((u + 1.0) * g * _sigmoid(SWIGLU_ALPHA * g)).astype(BF16)
        y_ref[...] = jnp.dot(act, wdn_bf[...], preferred_element_type=F32) + bdn_ref[0]

    @pl.when(i >= nu_ref[0])
    def _():
        y_ref[...] = jnp.zeros_like(y_ref)


def _moe_experts(xs, blk_e, n_used, w_gu, b_gu, w_down, b_down):
    n_blocks = blk_e.shape[0]
    return pl.pallas_call(
        _moe_kernel,
        grid_spec=pltpu.PrefetchScalarGridSpec(
            num_scalar_prefetch=2,
            grid=(n_blocks,),
            in_specs=[
                pl.BlockSpec((MOE_ROWS, D_MODEL), lambda i, be, nu: (i, 0)),
                pl.BlockSpec((1, D_MODEL, 2 * D_FF), lambda i, be, nu: (be[i], 0, 0)),
                pl.BlockSpec((1, 1, 2 * D_FF), lambda i, be, nu: (be[i], 0, 0)),
                pl.BlockSpec((1, D_FF, D_MODEL), lambda i, be, nu: (be[i], 0, 0)),
                pl.BlockSpec((1, 1, D_MODEL), lambda i, be, nu: (be[i], 0, 0)),
            ],
            out_specs=pl.BlockSpec((MOE_ROWS, D_MODEL), lambda i, be, nu: (i, 0)),
            scratch_shapes=[
                pltpu.VMEM((D_MODEL, 2 * D_FF), BF16),
                pltpu.VMEM((D_FF, D_MODEL), BF16),
            ],
        ),
        out_shape=jax.ShapeDtypeStruct((n_blocks * MOE_ROWS, D_MODEL), F32),
        compiler_params=_params(("arbitrary",)),
        name="moe_experts",
    )(blk_e, n_used, xs, w_gu, b_gu.reshape(N_EXPERTS, 1, -1), w_down, b_down.reshape(N_EXPERTS, 1, -1))


COMB_T = 128


def _combine_kernel(dest_ref, yb_ref, x1_ref, rt_ref, ga2_ref, gf_ref, y_ref, gbuf, sem, *, tok_off):
    i = pl.program_id(1)
    b = pl.program_id(0)
    n_i = pl.num_programs(1)
    step = b * n_i + i
    n_steps = pl.num_programs(0) * n_i
    tm = gbuf.shape[2]

    def start_gather(stp, slot):
        base = (tok_off + stp * tm) * TOP_K

        def body(r, carry):
            for k in range(TOP_K):
                row = dest_ref[base + r * TOP_K + k]
                pltpu.make_async_copy(yb_ref.at[pl.ds(row, 1), :], gbuf.at[slot, k, pl.ds(r, 1), :],
                                      sem.at[slot]).start()
            return carry

        lax.fori_loop(0, tm, body, 0)

    @pl.when(step == 0)
    def _():
        start_gather(0, 0)

    slot = step % 2

    @pl.when(step + 1 < n_steps)
    def _():
        start_gather(step + 1, 1 - slot)

    for k in range(TOP_K):
        pltpu.make_async_copy(yb_ref.at[pl.ds(0, tm), :], gbuf.at[slot, k], sem.at[slot]).wait()
    rec = rt_ref[0]
    moe = gbuf[slot, 0] * rec[:, ROUTE_W:ROUTE_W + 1]
    for k in range(1, TOP_K):
        moe = moe + gbuf[slot, k] * rec[:, ROUTE_W + k:ROUTE_W + k + 1]
    y_ref[0] = _rmsnorm(x1_ref[0] + ga2_ref[0] * moe, gf_ref[...])


def _combine(yb, dest, x1, route, ga2, g_final, tok_off):
    G, T, _ = x1.shape
    R = ga2.shape[1]
    tm = min(COMB_T, T)
    rm = 1 if R == 1 else tm
    mod_map = (lambda b, i, d: (b, 0, 0)) if R == 1 else (lambda b, i, d: (b, i, 0))
    return pl.pallas_call(
        functools.partial(_combine_kernel, tok_off=tok_off),
        grid_spec=pltpu.PrefetchScalarGridSpec(
            num_scalar_prefetch=1,
            grid=(G, T // tm),
            in_specs=[
                pl.BlockSpec(memory_space=pl.ANY),
                pl.BlockSpec((1, tm, D_MODEL), lambda b, i, d: (b, i, 0)),
                pl.BlockSpec((1, tm, LANES), lambda b, i, d: (b, i, 0)),
                pl.BlockSpec((1, rm, D_MODEL), mod_map),
                pl.BlockSpec((1, D_MODEL), lambda b, i, d: (0, 0)),
            ],
            out_specs=pl.BlockSpec((1, tm, D_MODEL), lambda b, i, d: (b, i, 0)),
            scratch_shapes=[
                pltpu.VMEM((2, TOP_K, tm, D_MODEL), F32),
                pltpu.SemaphoreType.DMA((2,)),
            ],
        ),
        out_shape=jax.ShapeDtypeStruct((G, T, D_MODEL), F32),
        compiler_params=_params(("arbitrary", "arbitrary")),
        name="combine",
    )(dest, yb, x1, route, ga2, g_final.reshape(1, -1))


def _routing(route, counts):
    n = route.shape[0]
    na = n * TOP_K
    topi = route[:, ROUTE_E:ROUTE_E + TOP_K].astype(jnp.int32)
    rank = route[:, ROUTE_R:ROUTE_R + TOP_K].astype(jnp.int32)
    cnt = counts.astype(jnp.int32)
    padded = (cnt + MOE_ROWS - 1) // MOE_ROWS * MOE_ROWS
    pad_end = jnp.cumsum(padded)
    pad_start = pad_end - padded
    onehot = topi[..., None] == jnp.arange(N_EXPERTS, dtype=jnp.int32)
    dest = (jnp.sum(jnp.where(onehot, pad_start, 0), axis=-1) + rank).reshape(na)
    n_blocks = -(-na // MOE_ROWS) + N_EXPERTS
    blk_start = jnp.arange(n_blocks, dtype=jnp.int32) * MOE_ROWS
    blk_e = jnp.minimum(jnp.sum(blk_start[:, None] >= pad_end[None, :], axis=1), N_EXPERTS - 1).astype(jnp.int32)
    n_used = (pad_end[-1] // MOE_ROWS).astype(jnp.int32).reshape(1)
    pad_edges = jnp.concatenate([jnp.zeros((1,), jnp.int32), pad_end.astype(jnp.int32)])
    return dest, blk_e, n_used, pad_edges


def _mods(mod):
    return [m[:, None, :] for m in jnp.split(mod, 6, axis=-1)]


def _layer(xp, xs, cp, cs, cache_k, cache_v, state, page_table, w_ada, b_ada, g_norm_mix, w_in, w_gk2, b_gk2,
           g_gla_norm, w_br_a, w_br_b, w_out, g_norm_ffn, w_router, b_router, w_gu, b_gu, w_down, b_down,
           g_final):
    B, T, _ = xp.shape
    S = xs.shape[0]
    c_all = jnp.concatenate([cp, cs], axis=0)
    pad = -c_all.shape[0] % 8
    mod = _ada_mod(jnp.pad(c_all, ((0, pad), (0, 0))), w_ada, b_ada)
    sh1p, sc1p, ga1p, sh2p, sc2p, ga2p = _mods(mod[:B])
    sh1s, sc1s, ga1s, sh2s, sc2s, ga2s = [m[None] for m in jnp.split(mod[B:B + S], 6, axis=-1)]

    w_perm = jnp.concatenate(
        [w_in[:, :COL_ZA], w_in[:, COL_ZA + G_LOWRANK:], w_in[:, COL_ZA:COL_ZA + G_LOWRANK],
         jnp.zeros((D_MODEL, PROJ_W - COL_LR - G_LOWRANK), F32)], axis=1).astype(BF16)
    wa, wb, wo = w_br_a.astype(BF16), w_br_b.astype(BF16), w_out.astype(BF16)

    xs3 = xs.reshape(1, S, D_MODEL)
    proj_p = _inproj(xp, sc1p, sh1p, g_norm_mix, w_perm, tm=1024)
    proj_s = _inproj(xs3, sc1s, sh1s, g_norm_mix, w_perm, tm=S)

    a_p = _moba_prompt(proj_p)
    og_p, s_p = _gla_prompt(proj_p, w_gk2, b_gk2, g_gla_norm)
    ps = proj_s[0]
    k_s = ps[:, COL_AK:COL_AK + A_KVW]
    v_s = ps[:, COL_AV:COL_AV + A_KVW]
    a_s = _moba_sample(ps[:, :A_QW], k_s, v_s, cache_k, cache_v, page_table)
    og_s, s_s = _gla_sample(proj_s.reshape(S, 1, PROJ_W), state, w_gk2, b_gk2, g_gla_norm)

    cnt0 = jnp.zeros((1, LANES), F32)
    x1p, h2p, rtp, cnt1 = _merge_router(xp, a_p, og_p, proj_p, ga1p, sc2p, sh2p, g_norm_ffn, wa, wb, wo,
                                        w_router, b_router, cnt0, tm=512)
    x1s, h2s, rts, cnt2 = _merge_router(xs3, a_s.astype(BF16)[None], og_s.reshape(1, S, -1).astype(BF16), proj_s,
                                        ga1s, sc2s, sh2s, g_norm_ffn, wa, wb, wo, w_router, b_router, cnt1, tm=S)

    n_p = B * T
    route = jnp.concatenate([rtp.reshape(n_p, LANES), rts.reshape(S, LANES)], axis=0)
    dest, blk_e, n_used, pad_edges = _routing(route, cnt2[0, :N_EXPERTS])
    xsort = _dispatch(h2p.reshape(n_p, D_MODEL), h2s.reshape(S, D_MODEL), dest, pad_edges, n_used,
                      rows=blk_e.shape[0] * MOE_ROWS)
    yb = _moe_experts(xsort, blk_e, n_used, w_gu, b_gu, w_down, b_down)
    y_p = _combine(yb, dest, x1p, rtp, ga2p, g_final, tok_off=0)
    y_s = _combine(yb, dest, x1s, rts, ga2s, g_final, tok_off=n_p)

    kp = proj_p[:, :, COL_AK:COL_AK + A_KVW].reshape(B, T, A_KV_HEADS, A_HEAD_DIM)
    vp = proj_p[:, :, COL_AV:COL_AV + A_KVW].reshape(B, T, A_KV_HEADS, A_HEAD_DIM)
    ks = k_s.reshape(S, 1, A_KV_HEADS, A_HEAD_DIM)
    vs = v_s.reshape(S, 1, A_KV_HEADS, A_HEAD_DIM)
    return y_p, y_s.reshape(S, 1, D_MODEL), kp, vp, s_p, ks, vs, s_s


def kernel(x_prompt, x_sample, c_prompt, c_sample, cache_k, cache_v, state_gla, page_table, w_ada, b_ada, g_norm_mix, w_in, w_gk2, b_gk2, g_gla_norm, w_br_a, w_br_b, w_out, g_norm_ffn, w_router, b_router, w_gu, b_gu, w_down, b_down, g_final):
    assert w_ada.shape[0] == 1, "single-layer step"
    outs = _layer(x_prompt, x_sample.reshape(x_sample.shape[0], D_MODEL), c_prompt, c_sample, cache_k[0],
                  cache_v[0], state_gla[0], page_table, w_ada[0], b_ada[0], g_norm_mix[0], w_in[0], w_gk2[0],
                  b_gk2[0], g_gla_norm[0], w_br_a[0], w_br_b[0], w_out[0], g_norm_ffn[0], w_router[0],
                  b_router[0], w_gu[0], b_gu[0], w_down[0], b_down[0], g_final)
    y_p, y_s, kp, vp, s_p, ks, vs, s_s = outs
    return (y_p, y_s, kp[None], vp[None], s_p[None], ks[None], vs[None], s_s[None])
```

```python
import functools

import jax
import jax.numpy as jnp
from jax import lax
from jax.experimental import pallas as pl
from jax.experimental.pallas import tpu as pltpu

F32 = jnp.float32
BF16 = jnp.bfloat16

D_MODEL = 1024
A_HEADS = 8
A_KV_HEADS = 4
A_GROUP = A_HEADS // A_KV_HEADS
A_HEAD_DIM = 64
A_QW = A_HEADS * A_HEAD_DIM
A_KVW = A_KV_HEADS * A_HEAD_DIM
A_BLOCK = 256
A_TOPK = 3
PAGE_SIZE = 128
G_HEADS = 4
G_KEY_DIM = D_MODEL // 2
G_VAL_DIM = D_MODEL
G_DK = G_KEY_DIM // G_HEADS
G_DV = G_VAL_DIM // G_HEADS
G_LOWRANK = 16
G_NORMALIZER = 16.0
G_CHUNK = 64
N_EXPERTS = 32
TOP_K = 4
D_FF = D_MODEL
SWIGLU_LIMIT = 7.0
SWIGLU_ALPHA = 1.702
EPS = 1e-6

LANES = 128
COL_AQ = 0
COL_AK = COL_AQ + A_QW
COL_AV = COL_AK + A_KVW
COL_GQ = COL_AV + A_KVW
COL_GK = COL_GQ + G_KEY_DIM
COL_GV = COL_GK + G_KEY_DIM
COL_GG = COL_GV + G_VAL_DIM
COL_ZA = COL_GG + G_VAL_DIM
COL_ZB = COL_ZA + D_MODEL
COL_LR = COL_ZB + D_MODEL
PROJ_TN = 1280
PROJ_W = 5 * PROJ_TN
assert PROJ_W >= COL_LR + LANES
NEG_BIG = -1e30
VMEM_LIMIT = 56 * 1024 * 1024

NT_DIMS = (((1,), (1,)), ((), ()))
NN_DIMS = (((1,), (0,)), ((), ()))


def _params(sem):
    return pltpu.CompilerParams(dimension_semantics=sem, vmem_limit_bytes=VMEM_LIMIT)


def _split(a):
    hi = a.astype(BF16)
    lo = (a - hi.astype(F32)).astype(BF16)
    return hi, lo


def _dot3(a, b, dims=NN_DIMS):
    ah, al = _split(a)
    bh, bl = _split(b)
    d = lambda x, y: lax.dot_general(x, y, dims, preferred_element_type=F32)
    return d(ah, bh) + d(al, bh) + d(ah, bl)


def _sigmoid(x):
    return 1.0 / (1.0 + jnp.exp(-x))


def _log_sigmoid(x):
    return -(jnp.maximum(-x, 0.0) + jnp.log(1.0 + jnp.exp(-jnp.abs(x))))


def _rmsnorm(x, g):
    return x * lax.rsqrt(jnp.mean(x * x, axis=-1, keepdims=True) + EPS) * g


def _top_cols(score, colf, n_pick):
    vals, cols = [], []
    big = float(score.shape[1])
    for _ in range(n_pick):
        mx = jnp.max(score, axis=1, keepdims=True)
        first = jnp.min(jnp.where(score == mx, colf, big), axis=1, keepdims=True)
        vals.append(mx)
        cols.append(first)
        score = jnp.where(colf == first, -jnp.inf, score)
    return vals, cols


def _ada_kernel(c_ref, w_ref, b_ref, o_ref):
    c = c_ref[...]
    o_ref[...] = _dot3(c * _sigmoid(c), w_ref[...]) + b_ref[...]


def _ada_mod(c, w_ada, b_ada):
    rows = c.shape[0]
    tn = 1536
    return pl.pallas_call(
        _ada_kernel,
        grid=(6 * D_MODEL // tn,),
        in_specs=[
            pl.BlockSpec((rows, D_MODEL), lambda j: (0, 0)),
            pl.BlockSpec((D_MODEL, tn), lambda j: (0, j)),
            pl.BlockSpec((1, tn), lambda j: (0, j)),
        ],
        out_specs=pl.BlockSpec((rows, tn), lambda j: (0, j)),
        out_shape=jax.ShapeDtypeStruct((rows, 6 * D_MODEL), F32),
        compiler_params=_params(("arbitrary",)),
        name="ada_mod",
    )(c, w_ada, b_ada.reshape(1, -1))


def _inproj_kernel(x_ref, sc_ref, sh_ref, g_ref, w_ref, o_ref, h_ref):
    @pl.when(pl.program_id(2) == 0)
    def _():
        y = _rmsnorm(x_ref[0], g_ref[...])
        h_ref[...] = (y * (1.0 + sc_ref[0]) + sh_ref[0]).astype(BF16)

    o_ref[0] = jnp.dot(h_ref[...], w_ref[...], preferred_element_type=F32)


def _inproj(x, sc, sh, g, w_perm, tm):
    G, T, _ = x.shape
    R = sc.shape[1]
    rm = 1 if R == 1 else tm
    mod_map = (lambda b, i, j: (b, 0, 0)) if R == 1 else (lambda b, i, j: (b, i, 0))
    return pl.pallas_call(
        _inproj_kernel,
        grid=(G, T // tm, PROJ_W // PROJ_TN),
        in_specs=[
            pl.BlockSpec((1, tm, D_MODEL), lambda b, i, j: (b, i, 0)),
            pl.BlockSpec((1, rm, D_MODEL), mod_map),
            pl.BlockSpec((1, rm, D_MODEL), mod_map),
            pl.BlockSpec((1, D_MODEL), lambda b, i, j: (0, 0)),
            pl.BlockSpec((D_MODEL, PROJ_TN), lambda b, i, j: (0, j)),
        ],
        out_specs=pl.BlockSpec((1, tm, PROJ_TN), lambda b, i, j: (b, i, j)),
        out_shape=jax.ShapeDtypeStruct((G, T, PROJ_W), F32),
        scratch_shapes=[pltpu.VMEM((tm, D_MODEL), BF16)],
        compiler_params=_params(("arbitrary", "arbitrary", "arbitrary")),
        name="inproj",
    )(x, sc, sh, g.reshape(1, -1), w_perm)


PREP_T = 2048
AUG_NB = 32
AUG_REST = LANES - A_HEAD_DIM - AUG_NB
KV_GROUP = 4
Q_BLOCKS = 4
Q_TILE = Q_BLOCKS * A_BLOCK


def _moba_prep_kernel(k_ref, v_ref, ka_ref, vt_ref, km_ref):
    t = pl.program_id(1)
    k = k_ref[0]
    v = v_ref[0]
    nblk = PREP_T // A_BLOCK
    km_ref[0] = jnp.mean(k.reshape(nblk, A_BLOCK, A_KVW), axis=1)
    half = LANES - A_HEAD_DIM
    row = lax.broadcasted_iota(jnp.int32, (A_BLOCK, half), 0)
    col = lax.broadcasted_iota(jnp.int32, (A_BLOCK, half), 1)
    vextra = jnp.where(col == 0, 1.0, 0.0)
    for jb in range(nblk):
        blk = t * nblk + jb
        kextra = jnp.where(col == blk, 1.0, 0.0)
        kextra = jnp.where((col == AUG_NB) | (col == AUG_NB + 1), 1.0, kextra)
        kextra = jnp.where(col == AUG_NB + 2, row.astype(F32), kextra)
        kextra = jnp.where(col == AUG_NB + 3, (blk * A_BLOCK).astype(F32), kextra)
        rows = slice(jb * A_BLOCK, (jb + 1) * A_BLOCK)
        for h in range(A_KV_HEADS):
            cols = slice(h * A_HEAD_DIM, (h + 1) * A_HEAD_DIM)
            ka_ref[0, h, jb] = jnp.concatenate([k[rows, cols], kextra], axis=1).astype(BF16)
            vt_ref[0, h, jb] = jnp.concatenate([v[rows, cols], vextra], axis=1).T.astype(BF16)


def _moba_prep(proj):
    B, T, _ = proj.shape
    nblk = PREP_T // A_BLOCK
    nb = T // A_BLOCK
    assert nb <= AUG_NB
    return pl.pallas_call(
        _moba_prep_kernel,
        grid=(B, T // PREP_T),
        in_specs=[
            pl.BlockSpec((1, PREP_T, A_KVW), lambda b, t: (b, t, COL_AK // A_KVW)),
            pl.BlockSpec((1, PREP_T, A_KVW), lambda b, t: (b, t, COL_AV // A_KVW)),
        ],
        out_specs=[
            pl.BlockSpec((1, A_KV_HEADS, nblk, A_BLOCK, LANES), lambda b, t: (b, 0, t, 0, 0)),
            pl.BlockSpec((1, A_KV_HEADS, nblk, LANES, A_BLOCK), lambda b, t: (b, 0, t, 0, 0)),
            pl.BlockSpec((1, nblk, A_KVW), lambda b, t: (b, t, 0)),
        ],
        out_shape=[
            jax.ShapeDtypeStruct((B, A_KV_HEADS, nb, A_BLOCK, LANES), BF16),
            jax.ShapeDtypeStruct((B, A_KV_HEADS, nb, LANES, A_BLOCK), BF16),
            jax.ShapeDtypeStruct((B, nb, A_KVW), F32),
        ],
        compiler_params=_params(("arbitrary", "arbitrary")),
        name="moba_prep",
    )(proj, proj)


def _slope_of(head_plus_one, shape):
    bits = (127 - head_plus_one) << 23
    if shape is not None:
        bits = jnp.full(shape, bits, jnp.int32)
    return lax.bitcast_convert_type(bits, F32)


def _top_rows(score, rowf, n_pick, n_valid):
    picked = jnp.zeros(score.shape, jnp.bool_)
    big = float(score.shape[0])
    for r in range(n_pick):
        mx = jnp.max(score, axis=0, keepdims=True)
        first = jnp.min(jnp.where(score == mx, rowf, big), axis=0, keepdims=True)
        hit = rowf == first
        picked = picked | (hit & ((jnp.zeros_like(rowf) + n_valid) > r))
        score = jnp.where(hit, -jnp.inf, score)
    return picked


def _moba_attn_kernel(q_ref, km_ref, ka_ref, vt_ref, o_ref, qa_ref, qo_ref, m_ref, acc_ref, sa_ref, sb_ref):
    kvh = pl.program_id(1)
    t = pl.program_id(2)
    nb = km_ref.shape[2]
    qt = q_ref[0].T
    km = km_ref[0, 0]
    wide = A_GROUP * Q_TILE
    q2 = jnp.concatenate([qt[hh * A_HEAD_DIM:(hh + 1) * A_HEAD_DIM, :] for hh in range(A_GROUP)], axis=1)
    rowf = lax.broadcasted_iota(jnp.int32, (nb, wide), 0).astype(F32)
    lane_nb = lax.broadcasted_iota(jnp.int32, (nb, wide), 1)
    qblk_nb = (t * Q_BLOCKS + (lane_nb % Q_TILE) // A_BLOCK).astype(F32)
    gate = jnp.where(rowf < qblk_nb, _dot3(km, q2), -jnp.inf)
    selbias = jnp.where(_top_rows(gate, rowf, A_TOPK, qblk_nb), 0.0, NEG_BIG)
    if nb < AUG_NB:
        selbias = jnp.concatenate([selbias, jnp.zeros((AUG_NB - nb, wide), F32)], axis=0)
    erow = lax.broadcasted_iota(jnp.int32, (AUG_REST, wide), 0)
    lane = lax.broadcasted_iota(jnp.int32, (AUG_REST, wide), 1)
    qq = (lane % A_BLOCK).astype(F32)
    qstart = ((t * Q_BLOCKS + (lane % Q_TILE) // A_BLOCK) * A_BLOCK).astype(F32)
    slope = _slope_of(kvh * A_GROUP + lane // Q_TILE + 1, None)
    alibi = jnp.where(erow == 0, -slope * qq, 0.0)
    alibi = jnp.where(erow == 1, -slope * qstart, alibi)
    alibi = jnp.where((erow == 2) | (erow == 3), slope, alibi)
    qs = q2 * (A_HEAD_DIM ** -0.5)
    qa2 = jnp.concatenate([qs, selbias, alibi], axis=0).astype(BF16)
    qo2 = jnp.concatenate([qs, jnp.zeros((AUG_NB, wide), F32), alibi], axis=0).astype(BF16)
    for hh in range(A_GROUP):
        qa_ref[hh] = qa2[:, hh * Q_TILE:(hh + 1) * Q_TILE]
        qo_ref[hh] = qo2[:, hh * Q_TILE:(hh + 1) * Q_TILE]

    for j in range(KV_GROUP):
        for hh in range(A_GROUP):
            sa_ref[hh, j] = jnp.dot(ka_ref[0, 0, j], qa_ref[hh], preferred_element_type=F32)

    key = lax.broadcasted_iota(jnp.int32, (A_BLOCK, A_BLOCK), 0)
    qry = lax.broadcasted_iota(jnp.int32, (A_BLOCK, A_BLOCK), 1)
    for e in range(Q_BLOCKS):
        kb = ka_ref[0, 0, t * Q_BLOCKS + e]
        vb = vt_ref[0, 0, t * Q_BLOCKS + e]
        cols = slice(e * A_BLOCK, (e + 1) * A_BLOCK)
        for hh in range(A_GROUP):
            s = jnp.dot(kb, qo_ref[hh, :, cols], preferred_element_type=F32)
            s = jnp.where(key <= qry, s, NEG_BIG)
            m = jnp.max(s, axis=0, keepdims=True)
            p = jnp.exp(s - m)
            m_ref[hh, :, cols] = m
            acc_ref[hh, :, cols] = jnp.dot(vb, p.astype(BF16), preferred_element_type=F32)

    n_groups = (t * Q_BLOCKS + Q_BLOCKS - 1 + KV_GROUP - 1) // KV_GROUP

    def scores(g, s_ref):
        for j in range(KV_GROUP):
            kj = ka_ref[0, 0, g * KV_GROUP + j]
            for hh in range(A_GROUP):
                s_ref[hh, j] = jnp.dot(kj, qa_ref[hh], preferred_element_type=F32)

    def accumulate(g, s_ref):
        for hh in range(A_GROUP):
            m_old = m_ref[hh]
            m_new = m_old
            for j in range(KV_GROUP):
                m_new = jnp.maximum(m_new, jnp.max(s_ref[hh, j], axis=0, keepdims=True))
            acc = jnp.exp(m_old - m_new) * acc_ref[hh]
            for j in range(KV_GROUP):
                p = jnp.exp(s_ref[hh, j] - m_new).astype(BF16)
                acc = acc + jnp.dot(vt_ref[0, 0, g * KV_GROUP + j], p, preferred_element_type=F32)
            acc_ref[hh] = acc
            m_ref[hh] = m_new

    def on_parity(g, fn):
        @pl.when(g % 2 == 0)
        def _():
            fn(sa_ref, sb_ref)

        @pl.when(g % 2 == 1)
        def _():
            fn(sb_ref, sa_ref)

    def body(g, carry):
        def step(cur, nxt):
            scores(g + 1, nxt)
            accumulate(g, cur)

        on_parity(g, step)
        return carry

    lax.fori_loop(0, n_groups - 1, body, 0)
    on_parity(n_groups - 1, lambda cur, nxt: accumulate(n_groups - 1, cur))

    outs = []
    for hh in range(A_GROUP):
        acc = acc_ref[hh]
        outs.append((acc[:A_HEAD_DIM] / acc[A_HEAD_DIM:A_HEAD_DIM + 1]).T)
    o_ref[0] = jnp.concatenate(outs, axis=1).astype(BF16)


def _moba_prompt(proj):
    B, T, _ = proj.shape
    nb = T // A_BLOCK
    assert nb % KV_GROUP == 0 and nb % Q_BLOCKS == 0 and Q_BLOCKS >= 2
    kaug, vaug_t, km = _moba_prep(proj)
    km = km.reshape(B, nb, A_KV_HEADS, A_HEAD_DIM).transpose(0, 2, 1, 3)
    qw = A_GROUP * A_HEAD_DIM
    return pl.pallas_call(
        _moba_attn_kernel,
        grid=(B, A_KV_HEADS, nb // Q_BLOCKS),
        in_specs=[
            pl.BlockSpec((1, Q_TILE, qw), lambda b, h, i: (b, i, h)),
            pl.BlockSpec((1, 1, nb, A_HEAD_DIM), lambda b, h, i: (b, h, 0, 0)),
            pl.BlockSpec((1, 1, nb, A_BLOCK, LANES), lambda b, h, i: (b, h, 0, 0, 0)),
            pl.BlockSpec((1, 1, nb, LANES, A_BLOCK), lambda b, h, i: (b, h, 0, 0, 0)),
        ],
        out_specs=pl.BlockSpec((1, Q_TILE, qw), lambda b, h, i: (b, i, h)),
        out_shape=jax.ShapeDtypeStruct((B, T, A_QW), BF16),
        scratch_shapes=[
            pltpu.VMEM((A_GROUP, LANES, Q_TILE), BF16),
            pltpu.VMEM((A_GROUP, LANES, Q_TILE), BF16),
            pltpu.VMEM((A_GROUP, 1, Q_TILE), F32),
            pltpu.VMEM((A_GROUP, LANES, Q_TILE), F32),
            pltpu.VMEM((A_GROUP, KV_GROUP, A_BLOCK, Q_TILE), F32),
            pltpu.VMEM((A_GROUP, KV_GROUP, A_BLOCK, Q_TILE), F32),
        ],
        compiler_params=_params(("arbitrary", "arbitrary", "arbitrary")),
        name="moba_attn",
    )(proj, km, kaug, vaug_t)


GLA_T = 1024
GLA_UNROLL = 8


def _gla_kernel(q_ref, k_ref, v_ref, gg_ref, lr_ref, w2_ref, b2_ref, gn_ref, og_ref, st_ref, s_ref, g_ref):
    t = pl.program_id(2)

    @pl.when(t == 0)
    def _():
        s_ref[...] = jnp.zeros_like(s_ref)

    z = _dot3(lr_ref[0][:, :G_LOWRANK], w2_ref[...]) + b2_ref[...]
    g_ref[...] = _log_sigmoid(z) * (1.0 / G_NORMALIZER)
    C = G_CHUNK
    rr = lax.broadcasted_iota(jnp.int32, (C, C), 0)
    cc = lax.broadcasted_iota(jnp.int32, (C, C), 1)
    causal = cc <= rr
    tril = jnp.where(causal, 1.0, 0.0).astype(BF16)
    er = lax.broadcasted_iota(jnp.int32, (G_DK, G_DK), 0)
    ec = lax.broadcasted_iota(jnp.int32, (G_DK, G_DK), 1)
    eye = er == ec
    gn = gn_ref[...]

    def group(gi, s):
        sls = [pl.ds(pl.multiple_of((gi * GLA_UNROLL + c) * C, C), C) for c in range(GLA_UNROLL)]
        cums = []
        for sl in sls:
            gh, gl = _split(g_ref[sl, :])
            cums.append(jnp.dot(tril, gh, preferred_element_type=F32) + jnp.dot(tril, gl, preferred_element_type=F32))
        parts = []
        for sl, cum in zip(sls, cums):
            last = cum[C - 1:C, :]
            kc = k_ref[0, sl, :]
            vc = v_ref[0, sl, :].astype(BF16)
            qe = (q_ref[0, sl, :] * (G_DK ** -0.5) * jnp.exp(cum)).astype(BF16)
            ke = (kc * jnp.exp(-cum)).astype(BF16)
            kd = kc * jnp.exp(last - cum)
            att = lax.dot_general(qe, ke, NT_DIMS, preferred_element_type=F32)
            att = jnp.where(causal, att, 0.0).astype(BF16)
            intra = jnp.dot(att, vc, preferred_element_type=F32)
            inc = jnp.dot(kd.T.astype(BF16), vc, preferred_element_type=F32)
            decay = jnp.sum(jnp.where(eye, jnp.exp(last), 0.0), axis=1, keepdims=True)
            parts.append((qe, intra, inc, decay))
        for sl, (qe, intra, inc, decay) in zip(sls, parts):
            o = jnp.dot(qe, s.astype(BF16), preferred_element_type=F32) + intra
            s = s * decay + inc
            on = _rmsnorm(o, gn)
            gate = gg_ref[0, sl, :]
            og_ref[0, sl, :] = (on * (gate * _sigmoid(gate))).astype(BF16)
        return s

    s_ref[...] = lax.fori_loop(0, GLA_T // (C * GLA_UNROLL), group, s_ref[...])

    @pl.when(t == pl.num_programs(2) - 1)
    def _():
        st_ref[0, 0] = s_ref[...]


def _gla_prompt(proj, w_gk2, b_gk2, g_gla_norm):
    B, T, _ = proj.shape
    tt = min(GLA_T, T)
    assert tt == GLA_T
    return pl.pallas_call(
        _gla_kernel,
        grid=(B, G_HEADS, T // tt),
        in_specs=[
            pl.BlockSpec((1, tt, G_DK), lambda b, h, t: (b, t, COL_GQ // G_DK + h)),
            pl.BlockSpec((1, tt, G_DK), lambda b, h, t: (b, t, COL_GK // G_DK + h)),
            pl.BlockSpec((1, tt, G_DV), lambda b, h, t: (b, t, COL_GV // G_DV + h)),
            pl.BlockSpec((1, tt, G_DV), lambda b, h, t: (b, t, COL_GG // G_DV + h)),
            pl.BlockSpec((1, tt, LANES), lambda b, h, t: (b, t, COL_LR // LANES)),
            pl.BlockSpec((G_LOWRANK, G_DK), lambda b, h, t: (0, h)),
            pl.BlockSpec((1, G_DK), lambda b, h, t: (0, h)),
            pl.BlockSpec((1, G_DV), lambda b, h, t: (0, 0)),
        ],
        out_specs=[
            pl.BlockSpec((1, tt, G_DV), lambda b, h, t: (b, t, h)),
            pl.BlockSpec((1, 1, G_DK, G_DV), lambda b, h, t: (b, h, 0, 0)),
        ],
        out_shape=[
            jax.ShapeDtypeStruct((B, T, G_VAL_DIM), BF16),
            jax.ShapeDtypeStruct((B, G_HEADS, G_DK, G_DV), F32),
        ],
        scratch_shapes=[pltpu.VMEM((G_DK, G_DV), F32), pltpu.VMEM((tt, G_DK), F32)],
        compiler_params=_params(("arbitrary", "arbitrary", "arbitrary")),
        name="gla_prompt",
    )(proj, proj, proj, proj, proj, w_gk2, b_gk2.reshape(1, -1), g_gla_norm.reshape(1, -1))


PPB = A_BLOCK // PAGE_SIZE


def _smoba_kernel(pt_ref, qb_ref, q8_ref, kn_ref, vn_ref, ck_ref, cv_ref, o_ref,
                  kbuf, vbuf, lg_ref, p_ref, pown_ref, idx_v, idx_s, ksem, vsem, isem):
    s = pl.program_id(0)
    n_seq = pl.num_programs(0) - 1
    n_pages = kbuf.shape[1]
    nb = n_pages // PPB
    past = n_pages * PAGE_SIZE
    hd = A_HEAD_DIM

    def start_keys(seq, slot):
        for p in range(n_pages):
            pltpu.make_async_copy(ck_ref.at[pt_ref[seq, p]], kbuf.at[slot, p], ksem.at[slot]).start()

    def value_copy(seq, slot, h, r, e):
        page = pt_ref[seq, idx_s[slot, h, r] * PPB + e]
        return pltpu.make_async_copy(cv_ref.at[page, pl.ds((h // A_GROUP) * hd, hd), :],
                                     vbuf.at[slot, h, r, e], vsem.at[slot])

    @pl.when(s == 0)
    def _():
        start_keys(0, 0)

    @pl.when(s + 1 < n_seq)
    def _():
        start_keys(s + 1, (s + 1) % 2)

    @pl.when(s < n_seq)
    def _():
        slot = s % 2
        pltpu.make_async_copy(ck_ref.at[pl.ds(0, n_pages)], kbuf.at[slot], ksem.at[slot]).wait()

        def page_logits(p, carry):
            kp = kbuf[slot, p]
            rows = []
            for h in range(A_HEADS):
                g = h // A_GROUP
                rows.append(jnp.sum(qb_ref[0, h] * kp[g * hd:(g + 1) * hd, :], axis=0, keepdims=True))
            lg_ref[slot, p] = jnp.concatenate(rows, axis=0)
            return carry

        lax.fori_loop(0, n_pages, page_logits, 0)
        lg = lg_ref[slot]
        gate = jnp.sum(jnp.sum(lg.reshape(nb, PPB, A_HEADS, PAGE_SIZE), axis=1), axis=2, keepdims=True)
        bidx = lax.broadcasted_iota(jnp.int32, (nb, A_HEADS, 1), 0).astype(F32)
        picked = jnp.zeros((nb, A_HEADS, 1), F32)
        firsts = []
        for r in range(A_TOPK):
            mx = jnp.max(gate, axis=0, keepdims=True)
            first = jnp.min(jnp.where(gate == mx, bidx, float(nb)), axis=0, keepdims=True)
            hit = bidx == first
            picked = jnp.where(hit, 1.0, picked)
            gate = jnp.where(hit, -jnp.inf, gate)
            firsts.append(first.reshape(A_HEADS, 1))
        sel = jnp.broadcast_to(picked[:, None], (nb, PPB, A_HEADS, 1)).reshape(n_pages, A_HEADS, 1)
        head1 = lax.broadcasted_iota(jnp.int32, (1, A_HEADS, 1), 1) + 1
        slope = lax.bitcast_convert_type((127 - head1) << 23, F32)
        tpos = (lax.broadcasted_iota(jnp.int32, (n_pages, 1, PAGE_SIZE), 0) * PAGE_SIZE
                + lax.broadcasted_iota(jnp.int32, (n_pages, 1, PAGE_SIZE), 2))
        dist = (past - tpos).astype(F32)
        att = jnp.where(sel > 0.5, lg * (hd ** -0.5) - slope * dist, NEG_BIG)
        own = jnp.sum(q8_ref[0] * kn_ref[0], axis=1, keepdims=True) * (hd ** -0.5)
        m = jnp.maximum(jnp.max(jnp.max(att, axis=0), axis=1, keepdims=True), own)
        p = jnp.exp(att - m.reshape(1, A_HEADS, 1))
        p_own = jnp.exp(own - m)
        inv = 1.0 / (jnp.sum(jnp.sum(p, axis=0), axis=1, keepdims=True) + p_own)
        p_ref[slot] = p * inv.reshape(1, A_HEADS, 1)
        pown_ref[slot] = jnp.broadcast_to(p_own * inv, (A_HEADS, LANES))
        lane = lax.broadcasted_iota(jnp.int32, (A_HEADS, LANES), 1)
        it = jnp.zeros((A_HEADS, LANES), F32)
        for r in range(A_TOPK):
            it = jnp.where(lane == r, firsts[r], it)
        idx_v[...] = it.astype(jnp.int32)
        to_smem = pltpu.make_async_copy(idx_v, idx_s.at[slot], isem)
        to_smem.start()
        to_smem.wait()
        for h in range(A_HEADS):
            for r in range(A_TOPK):
                for e in range(PPB):
                    value_copy(s, slot, h, r, e).start()

    @pl.when(s >= 1)
    def _():
        sb = s - 1
        b = sb % 2
        for h in range(A_HEADS):
            for r in range(A_TOPK):
                for e in range(PPB):
                    value_copy(sb, b, h, r, e).wait()
        lane = lax.broadcasted_iota(jnp.int32, (hd, LANES), 1)
        cols = jnp.zeros((hd, LANES), F32)
        for h in range(A_HEADS):
            acc = jnp.zeros((hd, PAGE_SIZE), F32)
            for r in range(A_TOPK):
                for e in range(PPB):
                    pg = idx_s[b, h, r] * PPB + e
                    acc = acc + vbuf[b, h, r, e] * p_ref[b, pg, h:h + 1, :]
            cols = jnp.where(lane == h, jnp.sum(acc, axis=1, keepdims=True), cols)
        o_ref[0] = cols.T[:A_HEADS, :] + pown_ref[b][:, :hd] * vn_ref[0]


def _moba_sample(q, k_new, v_new, cache_k, cache_v, page_table):
    S = q.shape[0]
    n_pages = page_table.shape[1]
    n_pool = cache_k.shape[0]
    hd = A_HEAD_DIM
    ck = cache_k.transpose(0, 2, 3, 1).reshape(n_pool, A_KVW, PAGE_SIZE)
    cv = cache_v.transpose(0, 2, 3, 1).reshape(n_pool, A_KVW, PAGE_SIZE)
    q8 = q.reshape(S, A_HEADS, hd)
    qb = jnp.broadcast_to(q8[..., None], (S, A_HEADS, hd, LANES))
    kn8 = jnp.repeat(k_new.reshape(S, A_KV_HEADS, hd), A_GROUP, axis=1)
    vn8 = jnp.repeat(v_new.reshape(S, A_KV_HEADS, hd), A_GROUP, axis=1)
    seq = lambda s, pt: (jnp.minimum(s, S - 1), 0, 0)
    out = pl.pallas_call(
        _smoba_kernel,
        grid_spec=pltpu.PrefetchScalarGridSpec(
            num_scalar_prefetch=1,
            grid=(S + 1,),
            in_specs=[
                pl.BlockSpec((1, A_HEADS, hd, LANES), lambda s, pt: (jnp.minimum(s, S - 1), 0, 0, 0)),
                pl.BlockSpec((1, A_HEADS, hd), seq),
                pl.BlockSpec((1, A_HEADS, hd), seq),
                pl.BlockSpec((1, A_HEADS, hd), lambda s, pt: (jnp.maximum(s - 1, 0), 0, 0)),
                pl.BlockSpec(memory_space=pl.ANY),
                pl.BlockSpec(memory_space=pl.ANY),
            ],
            out_specs=pl.BlockSpec((1, A_HEADS, hd), lambda s, pt: (jnp.maximum(s - 1, 0), 0, 0)),
            scratch_shapes=[
                pltpu.VMEM((2, n_pages, A_KVW, PAGE_SIZE), F32),
                pltpu.VMEM((2, A_HEADS, A_TOPK, PPB, hd, PAGE_SIZE), F32),
                pltpu.VMEM((2, n_pages, A_HEADS, PAGE_SIZE), F32),
                pltpu.VMEM((2, n_pages, A_HEADS, PAGE_SIZE), F32),
                pltpu.VMEM((2, A_HEADS, LANES), F32),
                pltpu.VMEM((A_HEADS, LANES), jnp.int32),
                pltpu.SMEM((2, A_HEADS, LANES), jnp.int32),
                pltpu.SemaphoreType.DMA((2,)),
                pltpu.SemaphoreType.DMA((2,)),
                pltpu.SemaphoreType.DMA(()),
            ],
        ),
        out_shape=jax.ShapeDtypeStruct((S, A_HEADS, hd), F32),
        compiler_params=_params(("arbitrary",)),
        name="moba_sample",
    )(page_table, qb, q8, kn8, vn8, ck, cv)
    return out.reshape(S, A_QW)


SGLA_SEQS = 4


def _sgla_kernel(p_ref, s0_ref, w2_ref, b2_ref, gn_ref, og_ref, s1_ref):
    er = lax.broadcasted_iota(jnp.int32, (G_DK, G_DK), 0)
    ec = lax.broadcasted_iota(jnp.int32, (G_DK, G_DK), 1)
    eye = er == ec
    col_of = lambda r: jnp.sum(jnp.where(eye, r, 0.0), axis=1, keepdims=True)
    for j in range(p_ref.shape[0]):
        row = p_ref[j]
        lr = row[:, COL_LR:COL_LR + G_LOWRANK]
        z = _dot3(lr, w2_ref[...]) + b2_ref[...]
        g_all = _log_sigmoid(z) * (1.0 / G_NORMALIZER)
        outs = []
        for h in range(G_HEADS):
            g = g_all[:, h * G_DK:(h + 1) * G_DK]
            q = row[:, COL_GQ + h * G_DK:COL_GQ + (h + 1) * G_DK] * (G_DK ** -0.5)
            k = row[:, COL_GK + h * G_DK:COL_GK + (h + 1) * G_DK]
            v = row[:, COL_GV + h * G_DV:COL_GV + (h + 1) * G_DV]
            gate = row[:, COL_GG + h * G_DV:COL_GG + (h + 1) * G_DV]
            qe = q * jnp.exp(g)
            ke = k * jnp.exp(-g)
            att = jnp.sum(qe * ke, axis=1, keepdims=True)
            s0 = s0_ref[j, h]
            o = _dot3(qe, s0) + att * v
            s1_ref[j, h] = s0 * col_of(jnp.exp(g)) + col_of(k) * v
            on = _rmsnorm(o, gn_ref[...])
            outs.append(on * (gate * _sigmoid(gate)))
        og_ref[j] = jnp.concatenate(outs, axis=1)


def _gla_sample(proj_rows, state, w_gk2, b_gk2, g_gla_norm):
    S = proj_rows.shape[0]
    ns = SGLA_SEQS if S % SGLA_SEQS == 0 else 1
    return pl.pallas_call(
        _sgla_kernel,
        grid=(S // ns,),
        in_specs=[
            pl.BlockSpec((ns, 1, PROJ_W), lambda s: (s, 0, 0)),
            pl.BlockSpec((ns, G_HEADS, G_DK, G_DV), lambda s: (s, 0, 0, 0)),
            pl.BlockSpec((G_LOWRANK, G_KEY_DIM), lambda s: (0, 0)),
            pl.BlockSpec((1, G_KEY_DIM), lambda s: (0, 0)),
            pl.BlockSpec((1, G_DV), lambda s: (0, 0)),
        ],
        out_specs=[
            pl.BlockSpec((ns, 1, G_VAL_DIM), lambda s: (s, 0, 0)),
            pl.BlockSpec((ns, G_HEADS, G_DK, G_DV), lambda s: (s, 0, 0, 0)),
        ],
        out_shape=[
            jax.ShapeDtypeStruct((S, 1, G_VAL_DIM), F32),
            jax.ShapeDtypeStruct((S, G_HEADS, G_DK, G_DV), F32),
        ],
        compiler_params=_params(("arbitrary",)),
        name="gla_sample",
    )(proj_rows, state, w_gk2, b_gk2.reshape(1, -1), g_gla_norm.reshape(1, -1))


ROUTE_E = 0
ROUTE_W = TOP_K
ROUTE_R = 2 * TOP_K


def _merge_kernel(x_ref, a_ref, og_ref, za_ref, zb_ref, ga1_ref, sc2_ref, sh2_ref, g2_ref, wa_ref, wb_ref,
                  wo_ref, wr_ref, br_ref, c0_ref, x1_ref, h2_ref, rt_ref, cnt_ref, run_ref):
    first = (pl.program_id(0) == 0) & (pl.program_id(1) == 0)

    @pl.when(first)
    def _():
        run_ref[...] = c0_ref[...]

    pa = jnp.dot(a_ref[0], wa_ref[...], preferred_element_type=F32)
    pb = jnp.dot(og_ref[0], wb_ref[...], preferred_element_type=F32)
    merged = _sigmoid(za_ref[0]) * pa + _sigmoid(zb_ref[0]) * pb
    mix = jnp.dot(merged.astype(BF16), wo_ref[...], preferred_element_type=F32)
    x1 = x_ref[0] + ga1_ref[0] * mix
    x1_ref[0] = x1
    h2 = _rmsnorm(x1, g2_ref[...]) * (1.0 + sc2_ref[0]) + sh2_ref[0]
    h2_ref[0] = h2
    logits = _dot3(h2, wr_ref[...]) + br_ref[...]
    tm = logits.shape[0]
    colf = lax.broadcasted_iota(jnp.int32, (tm, N_EXPERTS), 1).astype(F32)
    vals, cols = _top_cols(logits, colf, TOP_K)
    ex = [jnp.exp(v - vals[0]) for v in vals]
    tot = ex[0] + ex[1] + ex[2] + ex[3]
    onehot = [jnp.where(colf == c, 1.0, 0.0) for c in cols]
    osum = onehot[0] + onehot[1] + onehot[2] + onehot[3]
    tr = lax.broadcasted_iota(jnp.int32, (tm, tm), 0)
    tc = lax.broadcasted_iota(jnp.int32, (tm, tm), 1)
    lower = jnp.where(tc < tr, 1.0, 0.0).astype(BF16)
    before = jnp.dot(lower, osum.astype(BF16), preferred_element_type=F32) + run_ref[:, :N_EXPERTS]
    lane = lax.broadcasted_iota(jnp.int32, (tm, LANES), 1)
    rec = jnp.zeros((tm, LANES), F32)
    for r in range(TOP_K):
        rank = jnp.sum(onehot[r] * before, axis=1, keepdims=True)
        rec = jnp.where(lane == ROUTE_E + r, cols[r], rec)
        rec = jnp.where(lane == ROUTE_W + r, ex[r] / tot, rec)
        rec = jnp.where(lane == ROUTE_R + r, rank, rec)
    rt_ref[0] = rec
    pad = jnp.zeros((1, LANES - N_EXPERTS), F32)
    run_ref[...] = run_ref[...] + jnp.concatenate([jnp.sum(osum, axis=0, keepdims=True), pad], axis=1)
    cnt_ref[...] = run_ref[...]


def _merge_router(x, a, og, proj, ga1, sc2, sh2, g2, wa, wb, wo, wr, br, cnt0, tm):
    G, T, _ = x.shape
    R = ga1.shape[1]
    rm = 1 if R == 1 else tm
    mod_map = (lambda b, i: (b, 0, 0)) if R == 1 else (lambda b, i: (b, i, 0))
    tok = lambda w: pl.BlockSpec((1, tm, w), lambda b, i: (b, i, 0))
    full = lambda r, c: pl.BlockSpec((r, c), lambda b, i: (0, 0))
    return pl.pallas_call(
        _merge_kernel,
        grid=(G, T // tm),
        in_specs=[
            tok(D_MODEL), tok(A_QW), tok(G_VAL_DIM),
            pl.BlockSpec((1, tm, D_MODEL), lambda b, i: (b, i, COL_ZA // D_MODEL)),
            pl.BlockSpec((1, tm, D_MODEL), lambda b, i: (b, i, COL_ZB // D_MODEL)),
            pl.BlockSpec((1, rm, D_MODEL), mod_map),
            pl.BlockSpec((1, rm, D_MODEL), mod_map),
            pl.BlockSpec((1, rm, D_MODEL), mod_map),
            full(1, D_MODEL), full(A_QW, D_MODEL), full(G_VAL_DIM, D_MODEL), full(D_MODEL, D_MODEL),
            full(D_MODEL, N_EXPERTS), full(1, N_EXPERTS), full(1, LANES),
        ],
        out_specs=[tok(D_MODEL), tok(D_MODEL), tok(LANES), full(1, LANES)],
        out_shape=[
            jax.ShapeDtypeStruct((G, T, D_MODEL), F32),
            jax.ShapeDtypeStruct((G, T, D_MODEL), F32),
            jax.ShapeDtypeStruct((G, T, LANES), F32),
            jax.ShapeDtypeStruct((1, LANES), F32),
        ],
        scratch_shapes=[pltpu.VMEM((1, LANES), F32)],
        compiler_params=_params(("arbitrary", "arbitrary")),
        name="merge_router",
    )(x, a, og, proj, proj, ga1, sc2, sh2, g2.reshape(1, -1), wa, wb, wo, wr, br.reshape(1, -1), cnt0)


MOE_ROWS = 256
DISP_T = 128


def _zero_unwritten(pe_ref, nu_ref, xs_ref, zbuf, zsem):
    zbuf[...] = jnp.zeros_like(zbuf)
    n_blocks = xs_ref.shape[0] // MOE_ROWS

    def block_copy(start):
        return pltpu.make_async_copy(zbuf, xs_ref.at[pl.ds(pl.multiple_of(start, MOE_ROWS), MOE_ROWS), :], zsem)

    todo = [(pe_ref[e + 1] > pe_ref[e], pe_ref[e + 1] - MOE_ROWS) for e in range(N_EXPERTS)]
    todo += [(n_blocks - 1 - u >= nu_ref[0], (n_blocks - 1 - u) * MOE_ROWS) for u in range(N_EXPERTS)]
    for cond, start in todo:
        @pl.when(cond)
        def _():
            block_copy(start).start()

    for cond, start in todo:
        @pl.when(cond)
        def _():
            block_copy(start).wait()


def _scatter_rows(dest_ref, h_ref, xs_ref, sem, first_tok):
    tm = h_ref.shape[0]
    base = first_tok * TOP_K

    def body(r, carry):
        for k in range(TOP_K):
            row = dest_ref[base + r * TOP_K + k]
            pltpu.make_async_copy(h_ref.at[pl.ds(r, 1), :], xs_ref.at[pl.ds(row, 1), :], sem).start()
        return carry

    lax.fori_loop(0, tm, body, 0)
    for k in range(TOP_K):
        pltpu.make_async_copy(h_ref, xs_ref.at[pl.ds(0, tm), :], sem).wait()


def _dispatch_kernel(dest_ref, pe_ref, nu_ref, hp_ref, hs_ref, xs_ref, sem, zbuf, zsem, *, steps_p):
    i = pl.program_id(0)

    @pl.when(i == 0)
    def _():
        _zero_unwritten(pe_ref, nu_ref, xs_ref, zbuf, zsem)

    @pl.when(i < steps_p)
    def _():
        _scatter_rows(dest_ref, hp_ref, xs_ref, sem, i * hp_ref.shape[0])

    @pl.when(i >= steps_p)
    def _():
        _scatter_rows(dest_ref, hs_ref, xs_ref, sem, steps_p * hp_ref.shape[0] + (i - steps_p) * hs_ref.shape[0])


def _dispatch(h2p, h2s, dest, pad_edges, n_used, rows):
    n_p, n_s = h2p.shape[0], h2s.shape[0]
    tp, ts = min(DISP_T, n_p), min(DISP_T, n_s)
    assert n_p % tp == 0 and n_s % ts == 0 and rows // MOE_ROWS >= N_EXPERTS
    steps_p, steps_s = n_p // tp, n_s // ts
    return pl.pallas_call(
        functools.partial(_dispatch_kernel, steps_p=steps_p),
        grid_spec=pltpu.PrefetchScalarGridSpec(
            num_scalar_prefetch=3,
            grid=(steps_p + steps_s,),
            in_specs=[
                pl.BlockSpec((tp, D_MODEL), lambda i, d, pe, nu: (jnp.minimum(i, steps_p - 1), 0)),
                pl.BlockSpec((ts, D_MODEL), lambda i, d, pe, nu: (jnp.clip(i - steps_p, 0, steps_s - 1), 0)),
            ],
            out_specs=pl.BlockSpec(memory_space=pl.ANY),
            scratch_shapes=[pltpu.SemaphoreType.DMA(()), pltpu.VMEM((MOE_ROWS, D_MODEL), F32),
                            pltpu.SemaphoreType.DMA(())],
        ),
        out_shape=jax.ShapeDtypeStruct((rows, D_MODEL), F32),
        compiler_params=_params(("arbitrary",)),
        name="dispatch",
    )(dest, pad_edges, n_used, h2p, h2s)


def _moe_kernel(be_ref, nu_ref, x_ref, wgu_ref, bgu_ref, wdn_ref, bdn_ref, y_ref, wgu_bf, wdn_bf):
    i = pl.program_id(0)

    @pl.when(i < nu_ref[0])
    def _():
        @pl.when((i == 0) | (be_ref[i] != be_ref[jnp.maximum(i - 1, 0)]))
        def _():
            wgu_bf[...] = wgu_ref[0].astype(BF16)
            wdn_bf[...] = wdn_ref[0].astype(BF16)

        gu = jnp.dot(x_ref[...].astype(BF16), wgu_bf[...], preferred_element_type=F32) + bgu_ref[0]
        g = jnp.minimum(gu[:, :D_FF], SWIGLU_LIMIT)
        u = jnp.clip(gu[:, D_FF:], -SWIGLU_LIMIT, SWIGLU_LIMIT)
        act = ((u + 1.0) * g * _sigmoid(SWIGLU_ALPHA * g)).astype(BF16)
        y_ref[...] = jnp.dot(act, wdn_bf[...], preferred_element_type=F32) + bdn_ref[0]

    @pl.when(i >= nu_ref[0])
    def _():
        y_ref[...] = jnp.zeros_like(y_ref)


def _moe_experts(xs, blk_e, n_used, w_gu, b_gu, w_down, b_down):
    n_blocks = blk_e.shape[0]
    return pl.pallas_call(
        _moe_kernel,
        grid_spec=pltpu.PrefetchScalarGridSpec(
            num_scalar_prefetch=2,
            grid=(n_blocks,),
            in_specs=[
                pl.BlockSpec((MOE_ROWS, D_MODEL), lambda i, be, nu: (i, 0)),
                pl.BlockSpec((1, D_MODEL, 2 * D_FF), lambda i, be, nu: (be[i], 0, 0)),
                pl.BlockSpec((1, 1, 2 * D_FF), lambda i, be, nu: (be[i], 0, 0)),
                pl.BlockSpec((1, D_FF, D_MODEL), lambda i, be, nu: (be[i], 0, 0)),
                pl.BlockSpec((1, 1, D_MODEL), lambda i, be, nu: (be[i], 0, 0)),
            ],
            out_specs=pl.BlockSpec((MOE_ROWS, D_MODEL), lambda i, be, nu: (i, 0)),
            scratch_shapes=[
                pltpu.VMEM((D_MODEL, 2 * D_FF), BF16),
                pltpu.VMEM((D_FF, D_MODEL), BF16),
            ],
        ),
        out_shape=jax.ShapeDtypeStruct((n_blocks * MOE_ROWS, D_MODEL), F32),
        compiler_params=_params(("arbitrary",)),
        name="moe_experts",
    )(blk_e, n_used, xs, w_gu, b_gu.reshape(N_EXPERTS, 1, -1), w_down, b_down.reshape(N_EXPERTS, 1, -1))


COMB_T = 128


def _combine_kernel(dest_ref, yb_ref, x1_ref, rt_ref, ga2_ref, gf_ref, y_ref, gbuf, sem, *, tok_off):
    i = pl.program_id(1)
    b = pl.program_id(0)
    n_i = pl.num_programs(1)
    step = b * n_i + i
    n_steps = pl.num_programs(0) * n_i
    tm = gbuf.shape[2]

    def start_gather(stp, slot):
        base = (tok_off + stp * tm) * TOP_K

        def body(r, carry):
            for k in range(TOP_K):
                row = dest_ref[base + r * TOP_K + k]
                pltpu.make_async_copy(yb_ref.at[pl.ds(row, 1), :], gbuf.at[slot, k, pl.ds(r, 1), :],
                                      sem.at[slot]).start()
            return carry

        lax.fori_loop(0, tm, body, 0)

    @pl.when(step == 0)
    def _():
        start_gather(0, 0)

    slot = step % 2

    @pl.when(step + 1 < n_steps)
    def _():
        start_gather(step + 1, 1 - slot)

    for k in range(TOP_K):
        pltpu.make_async_copy(yb_ref.at[pl.ds(0, tm), :], gbuf.at[slot, k], sem.at[slot]).wait()
    rec = rt_ref[0]
    moe = gbuf[slot, 0] * rec[:, ROUTE_W:ROUTE_W + 1]
    for k in range(1, TOP_K):
        moe = moe + gbuf[slot, k] * rec[:, ROUTE_W + k:ROUTE_W + k + 1]
    y_ref[0] = _rmsnorm(x1_ref[0] + ga2_ref[0] * moe, gf_ref[...])


def _combine(yb, dest, x1, route, ga2, g_final, tok_off):
    G, T, _ = x1.shape
    R = ga2.shape[1]
    tm = min(COMB_T, T)
    rm = 1 if R == 1 else tm
    mod_map = (lambda b, i, d: (b, 0, 0)) if R == 1 else (lambda b, i, d: (b, i, 0))
    return pl.pallas_call(
        functools.partial(_combine_kernel, tok_off=tok_off),
        grid_spec=pltpu.PrefetchScalarGridSpec(
            num_scalar_prefetch=1,
            grid=(G, T // tm),
            in_specs=[
                pl.BlockSpec(memory_space=pl.ANY),
                pl.BlockSpec((1, tm, D_MODEL), lambda b, i, d: (b, i, 0)),
                pl.BlockSpec((1, tm, LANES), lambda b, i, d: (b, i, 0)),
                pl.BlockSpec((1, rm, D_MODEL), mod_map),
                pl.BlockSpec((1, D_MODEL), lambda b, i, d: (0, 0)),
            ],
            out_specs=pl.BlockSpec((1, tm, D_MODEL), lambda b, i, d: (b, i, 0)),
            scratch_shapes=[
                pltpu.VMEM((2, TOP_K, tm, D_MODEL), F32),
                pltpu.SemaphoreType.DMA((2,)),
            ],
        ),
        out_shape=jax.ShapeDtypeStruct((G, T, D_MODEL), F32),
        compiler_params=_params(("arbitrary", "arbitrary")),
        name="combine",
    )(dest, yb, x1, route, ga2, g_final.reshape(1, -1))


def _routing(route, counts):
    n = route.shape[0]
    na = n * TOP_K
    topi = route[:, ROUTE_E:ROUTE_E + TOP_K].astype(jnp.int32)
    rank = route[:, ROUTE_R:ROUTE_R + TOP_K].astype(jnp.int32)
    cnt = counts.astype(jnp.int32)
    padded = (cnt + MOE_ROWS - 1) // MOE_ROWS * MOE_ROWS
    pad_end = jnp.cumsum(padded)
    pad_start = pad_end - padded
    onehot = topi[..., None] == jnp.arange(N_EXPERTS, dtype=jnp.int32)
    dest = (jnp.sum(jnp.where(onehot, pad_start, 0), axis=-1) + rank).reshape(na)
    n_blocks = -(-na // MOE_ROWS) + N_EXPERTS
    blk_start = jnp.arange(n_blocks, dtype=jnp.int32) * MOE_ROWS
    blk_e = jnp.minimum(jnp.sum(blk_start[:, None] >= pad_end[None, :], axis=1), N_EXPERTS - 1).astype(jnp.int32)
    n_used = (pad_end[-1] // MOE_ROWS).astype(jnp.int32).reshape(1)
    pad_edges = jnp.concatenate([jnp.zeros((1,), jnp.int32), pad_end.astype(jnp.int32)])
    return dest, blk_e, n_used, pad_edges


def _mods(mod):
    return [m[:, None, :] for m in jnp.split(mod, 6, axis=-1)]


def _layer(xp, xs, cp, cs, cache_k, cache_v, state, page_table, w_ada, b_ada, g_norm_mix, w_in, w_gk2, b_gk2,
           g_gla_norm, w_br_a, w_br_b, w_out, g_norm_ffn, w_router, b_router, w_gu, b_gu, w_down, b_down,
           g_final):
    B, T, _ = xp.shape
    S = xs.shape[0]
    c_all = jnp.concatenate([cp, cs], axis=0)
    pad = -c_all.shape[0] % 8
    mod = _ada_mod(jnp.pad(c_all, ((0, pad), (0, 0))), w_ada, b_ada)
    sh1p, sc1p, ga1p, sh2p, sc2p, ga2p = _mods(mod[:B])
    sh1s, sc1s, ga1s, sh2s, sc2s, ga2s = [m[None] for m in jnp.split(mod[B:B + S], 6, axis=-1)]

    w_perm = jnp.concatenate(
        [w_in[:, :COL_ZA], w_in[:, COL_ZA + G_LOWRANK:], w_in[:, COL_ZA:COL_ZA + G_LOWRANK],
         jnp.zeros((D_MODEL, PROJ_W - COL_LR - G_LOWRANK), F32)], axis=1).astype(BF16)
    wa, wb, wo = w_br_a.astype(BF16), w_br_b.astype(BF16), w_out.astype(BF16)

    xs3 = xs.reshape(1, S, D_MODEL)
    proj_p = _inproj(xp, sc1p, sh1p, g_norm_mix, w_perm, tm=1024)
    proj_s = _inproj(xs3, sc1s, sh1s, g_norm_mix, w_perm, tm=S)

    a_p = _moba_prompt(proj_p)
    og_p, s_p = _gla_prompt(proj_p, w_gk2, b_gk2, g_gla_norm)
    ps = proj_s[0]
    k_s = ps[:, COL_AK:COL_AK + A_KVW]
    v_s = ps[:, COL_AV:COL_AV + A_KVW]
    a_s = _moba_sample(ps[:, :A_QW], k_s, v_s, cache_k, cache_v, page_table)
    og_s, s_s = _gla_sample(proj_s.reshape(S, 1, PROJ_W), state, w_gk2, b_gk2, g_gla_norm)

    cnt0 = jnp.zeros((1, LANES), F32)
    x1p, h2p, rtp, cnt1 = _merge_router(xp, a_p, og_p, proj_p, ga1p, sc2p, sh2p, g_norm_ffn, wa, wb, wo,
                                        w_router, b_router, cnt0, tm=512)
    x1s, h2s, rts, cnt2 = _merge_router(xs3, a_s.astype(BF16)[None], og_s.reshape(1, S, -1).astype(BF16), proj_s,
                                        ga1s, sc2s, sh2s, g_norm_ffn, wa, wb, wo, w_router, b_router, cnt1, tm=S)

    n_p = B * T
    route = jnp.concatenate([rtp.reshape(n_p, LANES), rts.reshape(S, LANES)], axis=0)
    dest, blk_e, n_used, pad_edges = _routing(route, cnt2[0, :N_EXPERTS])
    xsort = _dispatch(h2p.reshape(n_p, D_MODEL), h2s.reshape(S, D_MODEL), dest, pad_edges, n_used,
                      rows=blk_e.shape[0] * MOE_ROWS)
    yb = _moe_experts(xsort, blk_e, n_used, w_gu, b_gu, w_down, b_down)
    y_p = _combine(yb, dest, x1p, rtp, ga2p, g_final, tok_off=0)
    y_s = _combine(yb, dest, x1s, rts, ga2s, g_final, tok_off=n_p)

    kp = proj_p[:, :, COL_AK:COL_AK + A_KVW].reshape(B, T, A_KV_HEADS, A_HEAD_DIM)
    vp = proj_p[:, :, COL_AV:COL_AV + A_KVW].reshape(B, T, A_KV_HEADS, A_HEAD_DIM)
    ks = k_s.reshape(S, 1, A_KV_HEADS, A_HEAD_DIM)
    vs = v_s.reshape(S, 1, A_KV_HEADS, A_HEAD_DIM)
    return y_p, y_s.reshape(S, 1, D_MODEL), kp, vp, s_p, ks, vs, s_s


def kernel(x_prompt, x_sample, c_prompt, c_sample, cache_k, cache_v, state_gla, page_table, w_ada, b_ada, g_norm_mix, w_in, w_gk2, b_gk2, g_gla_norm, w_br_a, w_br_b, w_out, g_norm_ffn, w_router, b_router, w_gu, b_gu, w_down, b_down, g_final):
    assert w_ada.shape[0] == 1, "single-layer step"
    outs = _layer(x_prompt, x_sample.reshape(x_sample.shape[0], D_MODEL), c_prompt, c_sample, cache_k[0],
                  cache_v[0], state_gla[0], page_table, w_ada[0], b_ada[0], g_norm_mix[0], w_in[0], w_gk2[0],
                  b_gk2[0], g_gla_norm[0], w_br_a[0], w_br_b[0], w_out[0], g_norm_ffn[0], w_router[0],
                  b_router[0], w_gu[0], b_gu[0], w_down[0], b_down[0], g_final)
    y_p, y_s, kp, vp, s_p, ks, vs, s_s = outs
    return (y_p, y_s, kp[None], vp[None], s_p[None], ks[None], vs[None], s_s[None])
```

```python
import functools

import jax
import jax.numpy as jnp
from jax import lax
from jax.experimental import pallas as pl
from jax.experimental.pallas import tpu as pltpu

F32 = jnp.float32
BF16 = jnp.bfloat16

D_MODEL = 1024
A_HEADS = 8
A_KV_HEADS = 4
A_GROUP = A_HEADS // A_KV_HEADS
A_HEAD_DIM = 64
A_QW = A_HEADS * A_HEAD_DIM
A_KVW = A_KV_HEADS * A_HEAD_DIM
A_BLOCK = 256
A_TOPK = 3
PAGE_SIZE = 128
G_HEADS = 4
G_KEY_DIM = D_MODEL // 2
G_VAL_DIM = D_MODEL
G_DK = G_KEY_DIM // G_HEADS
G_DV = G_VAL_DIM // G_HEADS
G_LOWRANK = 16
G_NORMALIZER = 16.0
G_CHUNK = 64
N_EXPERTS = 32
TOP_K = 4
D_FF = D_MODEL
SWIGLU_LIMIT = 7.0
SWIGLU_ALPHA = 1.702
EPS = 1e-6

LANES = 128
COL_AQ = 0
COL_AK = COL_AQ + A_QW
COL_AV = COL_AK + A_KVW
COL_GQ = COL_AV + A_KVW
COL_GK = COL_GQ + G_KEY_DIM
COL_GV = COL_GK + G_KEY_DIM
COL_GG = COL_GV + G_VAL_DIM
COL_ZA = COL_GG + G_VAL_DIM
COL_ZB = COL_ZA + D_MODEL
COL_LR = COL_ZB + D_MODEL
PROJ_TN = 1280
PROJ_W = 5 * PROJ_TN
assert PROJ_W >= COL_LR + LANES
NEG_BIG = -1e30
VMEM_LIMIT = 56 * 1024 * 1024

NT_DIMS = (((1,), (1,)), ((), ()))
NN_DIMS = (((1,), (0,)), ((), ()))


def _params(sem):
    return pltpu.CompilerParams(dimension_semantics=sem, vmem_limit_bytes=VMEM_LIMIT)


def _split(a):
    hi = a.astype(BF16)
    lo = (a - hi.astype(F32)).astype(BF16)
    return hi, lo


def _dot3(a, b, dims=NN_DIMS):
    ah, al = _split(a)
    bh, bl = _split(b)
    d = lambda x, y: lax.dot_general(x, y, dims, preferred_element_type=F32)
    return d(ah, bh) + d(al, bh) + d(ah, bl)


def _sigmoid(x):
    return 1.0 / (1.0 + jnp.exp(-x))


def _log_sigmoid(x):
    return -(jnp.maximum(-x, 0.0) + jnp.log(1.0 + jnp.exp(-jnp.abs(x))))


def _rmsnorm(x, g):
    return x * lax.rsqrt(jnp.mean(x * x, axis=-1, keepdims=True) + EPS) * g


def _top_cols(score, colf, n_pick):
    vals, cols = [], []
    big = float(score.shape[1])
    for _ in range(n_pick):
        mx = jnp.max(score, axis=1, keepdims=True)
        first = jnp.min(jnp.where(score == mx, colf, big), axis=1, keepdims=True)
        vals.append(mx)
        cols.append(first)
        score = jnp.where(colf == first, -jnp.inf, score)
    return vals, cols


def _ada_kernel(c_ref, w_ref, b_ref, o_ref):
    c = c_ref[...]
    o_ref[...] = _dot3(c * _sigmoid(c), w_ref[...]) + b_ref[...]


def _ada_mod(c, w_ada, b_ada):
    rows = c.shape[0]
    tn = 1536
    return pl.pallas_call(
        _ada_kernel,
        grid=(6 * D_MODEL // tn,),
        in_specs=[
            pl.BlockSpec((rows, D_MODEL), lambda j: (0, 0)),
            pl.BlockSpec((D_MODEL, tn), lambda j: (0, j)),
            pl.BlockSpec((1, tn), lambda j: (0, j)),
        ],
        out_specs=pl.BlockSpec((rows, tn), lambda j: (0, j)),
        out_shape=jax.ShapeDtypeStruct((rows, 6 * D_MODEL), F32),
        compiler_params=_params(("arbitrary",)),
        name="ada_mod",
    )(c, w_ada, b_ada.reshape(1, -1))


def _inproj_kernel(x_ref, sc_ref, sh_ref, g_ref, w_ref, o_ref, h_ref):
    @pl.when(pl.program_id(2) == 0)
    def _():
        y = _rmsnorm(x_ref[0], g_ref[...])
        h_ref[...] = (y * (1.0 + sc_ref[0]) + sh_ref[0]).astype(BF16)

    o_ref[0] = jnp.dot(h_ref[...], w_ref[...], preferred_element_type=F32)


def _inproj(x, sc, sh, g, w_perm, tm):
    G, T, _ = x.shape
    R = sc.shape[1]
    rm = 1 if R == 1 else tm
    mod_map = (lambda b, i, j: (b, 0, 0)) if R == 1 else (lambda b, i, j: (b, i, 0))
    return pl.pallas_call(
        _inproj_kernel,
        grid=(G, T // tm, PROJ_W // PROJ_TN),
        in_specs=[
            pl.BlockSpec((1, tm, D_MODEL), lambda b, i, j: (b, i, 0)),
            pl.BlockSpec((1, rm, D_MODEL), mod_map),
            pl.BlockSpec((1, rm, D_MODEL), mod_map),
            pl.BlockSpec((1, D_MODEL), lambda b, i, j: (0, 0)),
            pl.BlockSpec((D_MODEL, PROJ_TN), lambda b, i, j: (0, j)),
        ],
        out_specs=pl.BlockSpec((1, tm, PROJ_TN), lambda b, i, j: (b, i, j)),
        out_shape=jax.ShapeDtypeStruct((G, T, PROJ_W), F32),
        scratch_shapes=[pltpu.VMEM((tm, D_MODEL), BF16)],
        compiler_params=_params(("arbitrary", "arbitrary", "arbitrary")),
        name="inproj",
    )(x, sc, sh, g.reshape(1, -1), w_perm)


PREP_T = 2048
AUG_NB = 32
AUG_REST = LANES - A_HEAD_DIM - AUG_NB
KV_GROUP = 4
Q_BLOCKS = 4
Q_TILE = Q_BLOCKS * A_BLOCK


def _moba_prep_kernel(k_ref, v_ref, ka_ref, vt_ref, km_ref):
    t = pl.program_id(1)
    k = k_ref[0]
    v = v_ref[0]
    nblk = PREP_T // A_BLOCK
    km_ref[0] = jnp.mean(k.reshape(nblk, A_BLOCK, A_KVW), axis=1)
    half = LANES - A_HEAD_DIM
    row = lax.broadcasted_iota(jnp.int32, (A_BLOCK, half), 0)
    col = lax.broadcasted_iota(jnp.int32, (A_BLOCK, half), 1)
    vextra = jnp.where(col == 0, 1.0, 0.0)
    for jb in range(nblk):
        blk = t * nblk + jb
        kextra = jnp.where(col == blk, 1.0, 0.0)
        kextra = jnp.where((col == AUG_NB) | (col == AUG_NB + 1), 1.0, kextra)
        kextra = jnp.where(col == AUG_NB + 2, row.astype(F32), kextra)
        kextra = jnp.where(col == AUG_NB + 3, (blk * A_BLOCK).astype(F32), kextra)
        rows = slice(jb * A_BLOCK, (jb + 1) * A_BLOCK)
        for h in range(A_KV_HEADS):
            cols = slice(h * A_HEAD_DIM, (h + 1) * A_HEAD_DIM)
            ka_ref[0, h, jb] = jnp.concatenate([k[rows, cols], kextra], axis=1).astype(BF16)
            vt_ref[0, h, jb] = jnp.concatenate([v[rows, cols], vextra], axis=1).T.astype(BF16)


def _moba_prep(proj):
    B, T, _ = proj.shape
    nblk = PREP_T // A_BLOCK
    nb = T // A_BLOCK
    assert nb <= AUG_NB
    return pl.pallas_call(
        _moba_prep_kernel,
        grid=(B, T // PREP_T),
        in_specs=[
            pl.BlockSpec((1, PREP_T, A_KVW), lambda b, t: (b, t, COL_AK // A_KVW)),
            pl.BlockSpec((1, PREP_T, A_KVW), lambda b, t: (b, t, COL_AV // A_KVW)),
        ],
        out_specs=[
            pl.BlockSpec((1, A_KV_HEADS, nblk, A_BLOCK, LANES), lambda b, t: (b, 0, t, 0, 0)),
            pl.BlockSpec((1, A_KV_HEADS, nblk, LANES, A_BLOCK), lambda b, t: (b, 0, t, 0, 0)),
            pl.BlockSpec((1, nblk, A_KVW), lambda b, t: (b, t, 0)),
        ],
        out_shape=[
            jax.ShapeDtypeStruct((B, A_KV_HEADS, nb, A_BLOCK, LANES), BF16),
            jax.ShapeDtypeStruct((B, A_KV_HEADS, nb, LANES, A_BLOCK), BF16),
            jax.ShapeDtypeStruct((B, nb, A_KVW), F32),
        ],
        compiler_params=_params(("arbitrary", "arbitrary")),
        name="moba_prep",
    )(proj, proj)


def _slope_of(head_plus_one, shape):
    bits = (127 - head_plus_one) << 23
    if shape is not None:
        bits = jnp.full(shape, bits, jnp.int32)
    return lax.bitcast_convert_type(bits, F32)


def _top_rows(score, rowf, n_pick, n_valid):
    picked = jnp.zeros(score.shape, jnp.bool_)
    big = float(score.shape[0])
    for r in range(n_pick):
        mx = jnp.max(score, axis=0, keepdims=True)
        first = jnp.min(jnp.where(score == mx, rowf, big), axis=0, keepdims=True)
        hit = rowf == first
        picked = picked | (hit & ((jnp.zeros_like(rowf) + n_valid) > r))
        score = jnp.where(hit, -jnp.inf, score)
    return picked


def _moba_attn_kernel(q_ref, km_ref, ka_ref, vt_ref, o_ref, qa_ref, qo_ref, m_ref, acc_ref, sa_ref, sb_ref):
    kvh = pl.program_id(1)
    t = pl.program_id(2)
    nb = km_ref.shape[2]
    qt = q_ref[0].T
    km = km_ref[0, 0]
    wide = A_GROUP * Q_TILE
    q2 = jnp.concatenate([qt[hh * A_HEAD_DIM:(hh + 1) * A_HEAD_DIM, :] for hh in range(A_GROUP)], axis=1)
    rowf = lax.broadcasted_iota(jnp.int32, (nb, wide), 0).astype(F32)
    lane_nb = lax.broadcasted_iota(jnp.int32, (nb, wide), 1)
    qblk_nb = (t * Q_BLOCKS + (lane_nb % Q_TILE) // A_BLOCK).astype(F32)
    gate = jnp.where(rowf < qblk_nb, _dot3(km, q2), -jnp.inf)
    selbias = jnp.where(_top_rows(gate, rowf, A_TOPK, qblk_nb), 0.0, NEG_BIG)
    if nb < AUG_NB:
        selbias = jnp.concatenate([selbias, jnp.zeros((AUG_NB - nb, wide), F32)], axis=0)
    erow = lax.broadcasted_iota(jnp.int32, (AUG_REST, wide), 0)
    lane = lax.broadcasted_iota(jnp.int32, (AUG_REST, wide), 1)
    qq = (lane % A_BLOCK).astype(F32)
    qstart = ((t * Q_BLOCKS + (lane % Q_TILE) // A_BLOCK) * A_BLOCK).astype(F32)
    slope = _slope_of(kvh * A_GROUP + lane // Q_TILE + 1, None)
    alibi = jnp.where(erow == 0, -slope * qq, 0.0)
    alibi = jnp.where(erow == 1, -slope * qstart, alibi)
    alibi = jnp.where((erow == 2) | (erow == 3), slope, alibi)
    qs = q2 * (A_HEAD_DIM ** -0.5)
    qa2 = jnp.concatenate([qs, selbias, alibi], axis=0).astype(BF16)
    qo2 = jnp.concatenate([qs, jnp.zeros((AUG_NB, wide), F32), alibi], axis=0).astype(BF16)
    for hh in range(A_GROUP):
        qa_ref[hh] = qa2[:, hh * Q_TILE:(hh + 1) * Q_TILE]
        qo_ref[hh] = qo2[:, hh * Q_TILE:(hh + 1) * Q_TILE]

    for j in range(KV_GROUP):
        for hh in range(A_GROUP):
            sa_ref[hh, j] = jnp.dot(ka_ref[0, 0, j], qa_ref[hh], preferred_element_type=F32)

    key = lax.broadcasted_iota(jnp.int32, (A_BLOCK, A_BLOCK), 0)
    qry = lax.broadcasted_iota(jnp.int32, (A_BLOCK, A_BLOCK), 1)
    for e in range(Q_BLOCKS):
        kb = ka_ref[0, 0, t * Q_BLOCKS + e]
        vb = vt_ref[0, 0, t * Q_BLOCKS + e]
        cols = slice(e * A_BLOCK, (e + 1) * A_BLOCK)
        for hh in range(A_GROUP):
            s = jnp.dot(kb, qo_ref[hh, :, cols], preferred_element_type=F32)
            s = jnp.where(key <= qry, s, NEG_BIG)
            m = jnp.max(s, axis=0, keepdims=True)
            p = jnp.exp(s - m)
            m_ref[hh, :, cols] = m
            acc_ref[hh, :, cols] = jnp.dot(vb, p.astype(BF16), preferred_element_type=F32)

    n_groups = (t * Q_BLOCKS + Q_BLOCKS - 1 + KV_GROUP - 1) // KV_GROUP

    def scores(g, s_ref):
        for j in range(KV_GROUP):
            kj = ka_ref[0, 0, g * KV_GROUP + j]
            for hh in range(A_GROUP):
                s_ref[hh, j] = jnp.dot(kj, qa_ref[hh], preferred_element_type=F32)

    def accumulate(g, s_ref):
        for hh in range(A_GROUP):
            m_old = m_ref[hh]
            m_new = m_old
            for j in range(KV_GROUP):
                m_new = jnp.maximum(m_new, jnp.max(s_ref[hh, j], axis=0, keepdims=True))
            acc = jnp.exp(m_old - m_new) * acc_ref[hh]
            for j in range(KV_GROUP):
                p = jnp.exp(s_ref[hh, j] - m_new).astype(BF16)
                acc = acc + jnp.dot(vt_ref[0, 0, g * KV_GROUP + j], p, preferred_element_type=F32)
            acc_ref[hh] = acc
            m_ref[hh] = m_new

    def on_parity(g, fn):
        @pl.when(g % 2 == 0)
        def _():
            fn(sa_ref, sb_ref)

        @pl.when(g % 2 == 1)
        def _():
            fn(sb_ref, sa_ref)

    def body(g, carry):
        def step(cur, nxt):
            scores(g + 1, nxt)
            accumulate(g, cur)

        on_parity(g, step)
        return carry

    lax.fori_loop(0, n_groups - 1, body, 0)
    on_parity(n_groups - 1, lambda cur, nxt: accumulate(n_groups - 1, cur))

    outs = []
    for hh in range(A_GROUP):
        acc = acc_ref[hh]
        outs.append((acc[:A_HEAD_DIM] / acc[A_HEAD_DIM:A_HEAD_DIM + 1]).T)
    o_ref[0] = jnp.concatenate(outs, axis=1).astype(BF16)


def _moba_prompt(proj):
    B, T, _ = proj.shape
    nb = T // A_BLOCK
    assert nb % KV_GROUP == 0 and nb % Q_BLOCKS == 0 and Q_BLOCKS >= 2
    kaug, vaug_t, km = _moba_prep(proj)
    km = km.reshape(B, nb, A_KV_HEADS, A_HEAD_DIM).transpose(0, 2, 1, 3)
    qw = A_GROUP * A_HEAD_DIM
    return pl.pallas_call(
        _moba_attn_kernel,
        grid=(B, A_KV_HEADS, nb // Q_BLOCKS),
        in_specs=[
            pl.BlockSpec((1, Q_TILE, qw), lambda b, h, i: (b, i, h)),
            pl.BlockSpec((1, 1, nb, A_HEAD_DIM), lambda b, h, i: (b, h, 0, 0)),
            pl.BlockSpec((1, 1, nb, A_BLOCK, LANES), lambda b, h, i: (b, h, 0, 0, 0)),
            pl.BlockSpec((1, 1, nb, LANES, A_BLOCK), lambda b, h, i: (b, h, 0, 0, 0)),
        ],
        out_specs=pl.BlockSpec((1, Q_TILE, qw), lambda b, h, i: (b, i, h)),
        out_shape=jax.ShapeDtypeStruct((B, T, A_QW), BF16),
        scratch_shapes=[
            pltpu.VMEM((A_GROUP, LANES, Q_TILE), BF16),
            pltpu.VMEM((A_GROUP, LANES, Q_TILE), BF16),
            pltpu.VMEM((A_GROUP, 1, Q_TILE), F32),
            pltpu.VMEM((A_GROUP, LANES, Q_TILE), F32),
            pltpu.VMEM((A_GROUP, KV_GROUP, A_BLOCK, Q_TILE), F32),
            pltpu.VMEM((A_GROUP, KV_GROUP, A_BLOCK, Q_TILE), F32),
        ],
        compiler_params=_params(("arbitrary", "arbitrary", "arbitrary")),
        name="moba_attn",
    )(proj, km, kaug, vaug_t)


GLA_T = 2048
GLA_UNROLL = 8


def _gla_kernel(q_ref, k_ref, v_ref, gg_ref, lr_ref, w2_ref, b2_ref, gn_ref, og_ref, st_ref, s_ref, g_ref):
    t = pl.program_id(2)

    @pl.when(t == 0)
    def _():
        s_ref[...] = jnp.zeros_like(s_ref)

    z = _dot3(lr_ref[0][:, :G_LOWRANK], w2_ref[...]) + b2_ref[...]
    g_ref[...] = _log_sigmoid(z) * (1.0 / G_NORMALIZER)
    C = G_CHUNK
    rr = lax.broadcasted_iota(jnp.int32, (C, C), 0)
    cc = lax.broadcasted_iota(jnp.int32, (C, C), 1)
    causal = cc <= rr
    tril = jnp.where(causal, 1.0, 0.0).astype(BF16)
    er = lax.broadcasted_iota(jnp.int32, (G_DK, G_DK), 0)
    ec = lax.broadcasted_iota(jnp.int32, (G_DK, G_DK), 1)
    eye = er == ec
    gn = gn_ref[...]

    def group(gi, s):
        sls = [pl.ds(pl.multiple_of((gi * GLA_UNROLL + c) * C, C), C) for c in range(GLA_UNROLL)]
        cums = []
        for sl in sls:
            gh, gl = _split(g_ref[sl, :])
            cums.append(jnp.dot(tril, gh, preferred_element_type=F32) + jnp.dot(tril, gl, preferred_element_type=F32))
        parts = []
        for sl, cum in zip(sls, cums):
            last = cum[C - 1:C, :]
            kc = k_ref[0, sl, :]
            vc = v_ref[0, sl, :].astype(BF16)
            qe = (q_ref[0, sl, :] * (G_DK ** -0.5) * jnp.exp(cum)).astype(BF16)
            ke = (kc * jnp.exp(-cum)).astype(BF16)
            kd = kc * jnp.exp(last - cum)
            att = lax.dot_general(qe, ke, NT_DIMS, preferred_element_type=F32)
            att = jnp.where(causal, att, 0.0).astype(BF16)
            intra = jnp.dot(att, vc, preferred_element_type=F32)
            inc = jnp.dot(kd.T.astype(BF16), vc, preferred_element_type=F32)
            decay = jnp.sum(jnp.where(eye, jnp.exp(last), 0.0), axis=1, keepdims=True)
            parts.append((qe, intra, inc, decay))
        for sl, (qe, intra, inc, decay) in zip(sls, parts):
            o = jnp.dot(qe, s.astype(BF16), preferred_element_type=F32) + intra
            s = s * decay + inc
            on = _rmsnorm(o, gn)
            gate = gg_ref[0, sl, :]
            og_ref[0, sl, :] = (on * (gate * _sigmoid(gate))).astype(BF16)
        return s

    s_ref[...] = lax.fori_loop(0, GLA_T // (C * GLA_UNROLL), group, s_ref[...])

    @pl.when(t == pl.num_programs(2) - 1)
    def _():
        st_ref[0, 0] = s_ref[...]


def _gla_prompt(proj, w_gk2, b_gk2, g_gla_norm):
    B, T, _ = proj.shape
    tt = min(GLA_T, T)
    assert tt == GLA_T
    return pl.pallas_call(
        _gla_kernel,
        grid=(B, G_HEADS, T // tt),
        in_specs=[
            pl.BlockSpec((1, tt, G_DK), lambda b, h, t: (b, t, COL_GQ // G_DK + h)),
            pl.BlockSpec((1, tt, G_DK), lambda b, h, t: (b, t, COL_GK // G_DK + h)),
            pl.BlockSpec((1, tt, G_DV), lambda b, h, t: (b, t, COL_GV // G_DV + h)),
            pl.BlockSpec((1, tt, G_DV), lambda b, h, t: (b, t, COL_GG // G_DV + h)),
            pl.BlockSpec((1, tt, LANES), lambda b, h, t: (b, t, COL_LR // LANES)),
            pl.BlockSpec((G_LOWRANK, G_DK), lambda b, h, t: (0, h)),
            pl.BlockSpec((1, G_DK), lambda b, h, t: (0, h)),
            pl.BlockSpec((1, G_DV), lambda b, h, t: (0, 0)),
        ],
        out_specs=[
            pl.BlockSpec((1, tt, G_DV), lambda b, h, t: (b, t, h)),
            pl.BlockSpec((1, 1, G_DK, G_DV), lambda b, h, t: (b, h, 0, 0)),
        ],
        out_shape=[
            jax.ShapeDtypeStruct((B, T, G_VAL_DIM), BF16),
            jax.ShapeDtypeStruct((B, G_HEADS, G_DK, G_DV), F32),
        ],
        scratch_shapes=[pltpu.VMEM((G_DK, G_DV), F32), pltpu.VMEM((tt, G_DK), F32)],
        compiler_params=_params(("arbitrary", "arbitrary", "arbitrary")),
        name="gla_prompt",
    )(proj, proj, proj, proj, proj, w_gk2, b_gk2.reshape(1, -1), g_gla_norm.reshape(1, -1))


PPB = A_BLOCK // PAGE_SIZE


def _smoba_kernel(pt_ref, qb_ref, q8_ref, kn_ref, vn_ref, ck_ref, cv_ref, o_ref,
                  kbuf, vbuf, lg_ref, p_ref, pown_ref, idx_v, idx_s, ksem, vsem, isem):
    s = pl.program_id(0)
    n_seq = pl.num_programs(0) - 1
    n_pages = kbuf.shape[1]
    nb = n_pages // PPB
    past = n_pages * PAGE_SIZE
    hd = A_HEAD_DIM

    def start_keys(seq, slot):
        for p in range(n_pages):
            pltpu.make_async_copy(ck_ref.at[pt_ref[seq, p]], kbuf.at[slot, p], ksem.at[slot]).start()

    def value_copy(seq, slot, h, r, e):
        page = pt_ref[seq, idx_s[slot, h, r] * PPB + e]
        return pltpu.make_async_copy(cv_ref.at[page, pl.ds((h // A_GROUP) * hd, hd), :],
                                     vbuf.at[slot, h, r, e], vsem.at[slot])

    @pl.when(s == 0)
    def _():
        start_keys(0, 0)

    @pl.when(s + 1 < n_seq)
    def _():
        start_keys(s + 1, (s + 1) % 2)

    @pl.when(s < n_seq)
    def _():
        slot = s % 2
        pltpu.make_async_copy(ck_ref.at[pl.ds(0, n_pages)], kbuf.at[slot], ksem.at[slot]).wait()

        def page_logits(p, carry):
            kp = kbuf[slot, p]
            rows = []
            for h in range(A_HEADS):
                g = h // A_GROUP
                rows.append(jnp.sum(qb_ref[0, h] * kp[g * hd:(g + 1) * hd, :], axis=0, keepdims=True))
            lg_ref[slot, p] = jnp.concatenate(rows, axis=0)
            return carry

        lax.fori_loop(0, n_pages, page_logits, 0)
        lg = lg_ref[slot]
        gate = jnp.sum(jnp.sum(lg.reshape(nb, PPB, A_HEADS, PAGE_SIZE), axis=1), axis=2, keepdims=True)
        bidx = lax.broadcasted_iota(jnp.int32, (nb, A_HEADS, 1), 0).astype(F32)
        picked = jnp.zeros((nb, A_HEADS, 1), F32)
        firsts = []
        for r in range(A_TOPK):
            mx = jnp.max(gate, axis=0, keepdims=True)
            first = jnp.min(jnp.where(gate == mx, bidx, float(nb)), axis=0, keepdims=True)
            hit = bidx == first
            picked = jnp.where(hit, 1.0, picked)
            gate = jnp.where(hit, -jnp.inf, gate)
            firsts.append(first.reshape(A_HEADS, 1))
        sel = jnp.broadcast_to(picked[:, None], (nb, PPB, A_HEADS, 1)).reshape(n_pages, A_HEADS, 1)
        head1 = lax.broadcasted_iota(jnp.int32, (1, A_HEADS, 1), 1) + 1
        slope = lax.bitcast_convert_type((127 - head1) << 23, F32)
        tpos = (lax.broadcasted_iota(jnp.int32, (n_pages, 1, PAGE_SIZE), 0) * PAGE_SIZE
                + lax.broadcasted_iota(jnp.int32, (n_pages, 1, PAGE_SIZE), 2))
        dist = (past - tpos).astype(F32)
        att = jnp.where(sel > 0.5, lg * (hd ** -0.5) - slope * dist, NEG_BIG)
        own = jnp.sum(q8_ref[0] * kn_ref[0], axis=1, keepdims=True) * (hd ** -0.5)
        m = jnp.maximum(jnp.max(jnp.max(att, axis=0), axis=1, keepdims=True), own)
        p = jnp.exp(att - m.reshape(1, A_HEADS, 1))
        p_own = jnp.exp(own - m)
        inv = 1.0 / (jnp.sum(jnp.sum(p, axis=0), axis=1, keepdims=True) + p_own)
        p_ref[slot] = p * inv.reshape(1, A_HEADS, 1)
        pown_ref[slot] = jnp.broadcast_to(p_own * inv, (A_HEADS, LANES))
        lane = lax.broadcasted_iota(jnp.int32, (A_HEADS, LANES), 1)
        it = jnp.zeros((A_HEADS, LANES), F32)
        for r in range(A_TOPK):
            it = jnp.where(lane == r, firsts[r], it)
        idx_v[...] = it.astype(jnp.int32)
        to_smem = pltpu.make_async_copy(idx_v, idx_s.at[slot], isem)
        to_smem.start()
        to_smem.wait()
        for h in range(A_HEADS):
            for r in range(A_TOPK):
                for e in range(PPB):
                    value_copy(s, slot, h, r, e).start()

    @pl.when(s >= 1)
    def _():
        sb = s - 1
        b = sb % 2
        for h in range(A_HEADS):
            for r in range(A_TOPK):
                for e in range(PPB):
                    value_copy(sb, b, h, r, e).wait()
        lane = lax.broadcasted_iota(jnp.int32, (hd, LANES), 1)
        cols = jnp.zeros((hd, LANES), F32)
        for h in range(A_HEADS):
            acc = jnp.zeros((hd, PAGE_SIZE), F32)
            for r in range(A_TOPK):
                for e in range(PPB):
                    pg = idx_s[b, h, r] * PPB + e
                    acc = acc + vbuf[b, h, r, e] * p_ref[b, pg, h:h + 1, :]
            cols = jnp.where(lane == h, jnp.sum(acc, axis=1, keepdims=True), cols)
        o_ref[0] = cols.T[:A_HEADS, :] + pown_ref[b][:, :hd] * vn_ref[0]


def _moba_sample(q, k_new, v_new, cache_k, cache_v, page_table):
    S = q.shape[0]
    n_pages = page_table.shape[1]
    n_pool = cache_k.shape[0]
    hd = A_HEAD_DIM
    ck = cache_k.transpose(0, 2, 3, 1).reshape(n_pool, A_KVW, PAGE_SIZE)
    cv = cache_v.transpose(0, 2, 3, 1).reshape(n_pool, A_KVW, PAGE_SIZE)
    q8 = q.reshape(S, A_HEADS, hd)
    qb = jnp.broadcast_to(q8[..., None], (S, A_HEADS, hd, LANES))
    kn8 = jnp.repeat(k_new.reshape(S, A_KV_HEADS, hd), A_GROUP, axis=1)
    vn8 = jnp.repeat(v_new.reshape(S, A_KV_HEADS, hd), A_GROUP, axis=1)
    seq = lambda s, pt: (jnp.minimum(s, S - 1), 0, 0)
    out = pl.pallas_call(
        _smoba_kernel,
        grid_spec=pltpu.PrefetchScalarGridSpec(
            num_scalar_prefetch=1,
            grid=(S + 1,),
            in_specs=[
                pl.BlockSpec((1, A_HEADS, hd, LANES), lambda s, pt: (jnp.minimum(s, S - 1), 0, 0, 0)),
                pl.BlockSpec((1, A_HEADS, hd), seq),
                pl.BlockSpec((1, A_HEADS, hd), seq),
                pl.BlockSpec((1, A_HEADS, hd), lambda s, pt: (jnp.maximum(s - 1, 0), 0, 0)),
                pl.BlockSpec(memory_space=pl.ANY),
                pl.BlockSpec(memory_space=pl.ANY),
            ],
            out_specs=pl.BlockSpec((1, A_HEADS, hd), lambda s, pt: (jnp.maximum(s - 1, 0), 0, 0)),
            scratch_shapes=[
                pltpu.VMEM((2, n_pages, A_KVW, PAGE_SIZE), F32),
                pltpu.VMEM((2, A_HEADS, A_TOPK, PPB, hd, PAGE_SIZE), F32),
                pltpu.VMEM((2, n_pages, A_HEADS, PAGE_SIZE), F32),
                pltpu.VMEM((2, n_pages, A_HEADS, PAGE_SIZE), F32),
                pltpu.VMEM((2, A_HEADS, LANES), F32),
                pltpu.VMEM((A_HEADS, LANES), jnp.int32),
                pltpu.SMEM((2, A_HEADS, LANES), jnp.int32),
                pltpu.SemaphoreType.DMA((2,)),
                pltpu.SemaphoreType.DMA((2,)),
                pltpu.SemaphoreType.DMA(()),
            ],
        ),
        out_shape=jax.ShapeDtypeStruct((S, A_HEADS, hd), F32),
        compiler_params=_params(("arbitrary",)),
        name="moba_sample",
    )(page_table, qb, q8, kn8, vn8, ck, cv)
    return out.reshape(S, A_QW)


SGLA_SEQS = 8


def _sgla_kernel(p_ref, s0_ref, w2_ref, b2_ref, gn_ref, og_ref, s1_ref):
    er = lax.broadcasted_iota(jnp.int32, (G_DK, G_DK), 0)
    ec = lax.broadcasted_iota(jnp.int32, (G_DK, G_DK), 1)
    eye = er == ec
    col_of = lambda r: jnp.sum(jnp.where(eye, r, 0.0), axis=1, keepdims=True)
    for j in range(p_ref.shape[0]):
        row = p_ref[j]
        lr = row[:, COL_LR:COL_LR + G_LOWRANK]
        z = _dot3(lr, w2_ref[...]) + b2_ref[...]
        g_all = _log_sigmoid(z) * (1.0 / G_NORMALIZER)
        outs = []
        for h in range(G_HEADS):
            g = g_all[:, h * G_DK:(h + 1) * G_DK]
            q = row[:, COL_GQ + h * G_DK:COL_GQ + (h + 1) * G_DK] * (G_DK ** -0.5)
            k = row[:, COL_GK + h * G_DK:COL_GK + (h + 1) * G_DK]
            v = row[:, COL_GV + h * G_DV:COL_GV + (h + 1) * G_DV]
            gate = row[:, COL_GG + h * G_DV:COL_GG + (h + 1) * G_DV]
            qe = q * jnp.exp(g)
            ke = k * jnp.exp(-g)
            att = jnp.sum(qe * ke, axis=1, keepdims=True)
            s0 = s0_ref[j, h]
            o = _dot3(qe, s0) + att * v
            s1_ref[j, h] = s0 * col_of(jnp.exp(g)) + col_of(k) * v
            on = _rmsnorm(o, gn_ref[...])
            outs.append(on * (gate * _sigmoid(gate)))
        og_ref[j] = jnp.concatenate(outs, axis=1)


def _gla_sample(proj_rows, state, w_gk2, b_gk2, g_gla_norm):
    S = proj_rows.shape[0]
    ns = SGLA_SEQS if S % SGLA_SEQS == 0 else 1
    return pl.pallas_call(
        _sgla_kernel,
        grid=(S // ns,),
        in_specs=[
            pl.BlockSpec((ns, 1, PROJ_W), lambda s: (s, 0, 0)),
            pl.BlockSpec((ns, G_HEADS, G_DK, G_DV), lambda s: (s, 0, 0, 0)),
            pl.BlockSpec((G_LOWRANK, G_KEY_DIM), lambda s: (0, 0)),
            pl.BlockSpec((1, G_KEY_DIM), lambda s: (0, 0)),
            pl.BlockSpec((1, G_DV), lambda s: (0, 0)),
        ],
        out_specs=[
            pl.BlockSpec((ns, 1, G_VAL_DIM), lambda s: (s, 0, 0)),
            pl.BlockSpec((ns, G_HEADS, G_DK, G_DV), lambda s: (s, 0, 0, 0)),
        ],
        out_shape=[
            jax.ShapeDtypeStruct((S, 1, G_VAL_DIM), F32),
            jax.ShapeDtypeStruct((S, G_HEADS, G_DK, G_DV), F32),
        ],
        compiler_params=_params(("arbitrary",)),
        name="gla_sample",
    )(proj_rows, state, w_gk2, b_gk2.reshape(1, -1), g_gla_norm.reshape(1, -1))


ROUTE_E = 0
ROUTE_W = TOP_K
ROUTE_R = 2 * TOP_K


def _merge_kernel(x_ref, a_ref, og_ref, za_ref, zb_ref, ga1_ref, sc2_ref, sh2_ref, g2_ref, wa_ref, wb_ref,
                  wo_ref, wr_ref, br_ref, c0_ref, x1_ref, h2_ref, rt_ref, cnt_ref, run_ref):
    first = (pl.program_id(0) == 0) & (pl.program_id(1) == 0)

    @pl.when(first)
    def _():
        run_ref[...] = c0_ref[...]

    pa = jnp.dot(a_ref[0], wa_ref[...], preferred_element_type=F32)
    pb = jnp.dot(og_ref[0], wb_ref[...], preferred_element_type=F32)
    merged = _sigmoid(za_ref[0]) * pa + _sigmoid(zb_ref[0]) * pb
    mix = jnp.dot(merged.astype(BF16), wo_ref[...], preferred_element_type=F32)
    x1 = x_ref[0] + ga1_ref[0] * mix
    x1_ref[0] = x1
    h2 = _rmsnorm(x1, g2_ref[...]) * (1.0 + sc2_ref[0]) + sh2_ref[0]
    h2_ref[0] = h2
    logits = _dot3(h2, wr_ref[...]) + br_ref[...]
    tm = logits.shape[0]
    colf = lax.broadcasted_iota(jnp.int32, (tm, N_EXPERTS), 1).astype(F32)
    vals, cols = _top_cols(logits, colf, TOP_K)
    ex = [jnp.exp(v - vals[0]) for v in vals]
    tot = ex[0] + ex[1] + ex[2] + ex[3]
    onehot = [jnp.where(colf == c, 1.0, 0.0) for c in cols]
    osum = onehot[0] + onehot[1] + onehot[2] + onehot[3]
    tr = lax.broadcasted_iota(jnp.int32, (tm, tm), 0)
    tc = lax.broadcasted_iota(jnp.int32, (tm, tm), 1)
    lower = jnp.where(tc < tr, 1.0, 0.0).astype(BF16)
    before = jnp.dot(lower, osum.astype(BF16), preferred_element_type=F32) + run_ref[:, :N_EXPERTS]
    lane = lax.broadcasted_iota(jnp.int32, (tm, LANES), 1)
    rec = jnp.zeros((tm, LANES), F32)
    for r in range(TOP_K):
        rank = jnp.sum(onehot[r] * before, axis=1, keepdims=True)
        rec = jnp.where(lane == ROUTE_E + r, cols[r], rec)
        rec = jnp.where(lane == ROUTE_W + r, ex[r] / tot, rec)
        rec = jnp.where(lane == ROUTE_R + r, rank, rec)
    rt_ref[0] = rec
    pad = jnp.zeros((1, LANES - N_EXPERTS), F32)
    run_ref[...] = run_ref[...] + jnp.concatenate([jnp.sum(osum, axis=0, keepdims=True), pad], axis=1)
    cnt_ref[...] = run_ref[...]


def _merge_router(x, a, og, proj, ga1, sc2, sh2, g2, wa, wb, wo, wr, br, cnt0, tm):
    G, T, _ = x.shape
    R = ga1.shape[1]
    rm = 1 if R == 1 else tm
    mod_map = (lambda b, i: (b, 0, 0)) if R == 1 else (lambda b, i: (b, i, 0))
    tok = lambda w: pl.BlockSpec((1, tm, w), lambda b, i: (b, i, 0))
    full = lambda r, c: pl.BlockSpec((r, c), lambda b, i: (0, 0))
    return pl.pallas_call(
        _merge_kernel,
        grid=(G, T // tm),
        in_specs=[
            tok(D_MODEL), tok(A_QW), tok(G_VAL_DIM),
            pl.BlockSpec((1, tm, D_MODEL), lambda b, i: (b, i, COL_ZA // D_MODEL)),
            pl.BlockSpec((1, tm, D_MODEL), lambda b, i: (b, i, COL_ZB // D_MODEL)),
            pl.BlockSpec((1, rm, D_MODEL), mod_map),
            pl.BlockSpec((1, rm, D_MODEL), mod_map),
            pl.BlockSpec((1, rm, D_MODEL), mod_map),
            full(1, D_MODEL), full(A_QW, D_MODEL), full(G_VAL_DIM, D_MODEL), full(D_MODEL, D_MODEL),
            full(D_MODEL, N_EXPERTS), full(1, N_EXPERTS), full(1, LANES),
        ],
        out_specs=[tok(D_MODEL), tok(D_MODEL), tok(LANES), full(1, LANES)],
        out_shape=[
            jax.ShapeDtypeStruct((G, T, D_MODEL), F32),
            jax.ShapeDtypeStruct((G, T, D_MODEL), F32),
            jax.ShapeDtypeStruct((G, T, LANES), F32),
            jax.ShapeDtypeStruct((1, LANES), F32),
        ],
        scratch_shapes=[pltpu.VMEM((1, LANES), F32)],
        compiler_params=_params(("arbitrary", "arbitrary")),
        name="merge_router",
    )(x, a, og, proj, proj, ga1, sc2, sh2, g2.reshape(1, -1), wa, wb, wo, wr, br.reshape(1, -1), cnt0)


MOE_ROWS = 256
DISP_T = 256


def _zero_unwritten(pe_ref, nu_ref, xs_ref, zbuf, zsem):
    zbuf[...] = jnp.zeros_like(zbuf)
    n_blocks = xs_ref.shape[0] // MOE_ROWS

    def block_copy(start):
        return pltpu.make_async_copy(zbuf, xs_ref.at[pl.ds(pl.multiple_of(start, MOE_ROWS), MOE_ROWS), :], zsem)

    todo = [(pe_ref[e + 1] > pe_ref[e], pe_ref[e + 1] - MOE_ROWS) for e in range(N_EXPERTS)]
    todo += [(n_blocks - 1 - u >= nu_ref[0], (n_blocks - 1 - u) * MOE_ROWS) for u in range(N_EXPERTS)]
    for cond, start in todo:
        @pl.when(cond)
        def _():
            block_copy(start).start()

    for cond, start in todo:
        @pl.when(cond)
        def _():
            block_copy(start).wait()


def _scatter_rows(dest_ref, h_ref, xs_ref, sem, first_tok):
    tm = h_ref.shape[0]
    base = first_tok * TOP_K

    def body(r, carry):
        for k in range(TOP_K):
            row = dest_ref[base + r * TOP_K + k]
            pltpu.make_async_copy(h_ref.at[pl.ds(r, 1), :], xs_ref.at[pl.ds(row, 1), :], sem).start()
        return carry

    lax.fori_loop(0, tm, body, 0)
    for k in range(TOP_K):
        pltpu.make_async_copy(h_ref, xs_ref.at[pl.ds(0, tm), :], sem).wait()


def _dispatch_kernel(dest_ref, pe_ref, nu_ref, hp_ref, hs_ref, xs_ref, sem, zbuf, zsem, *, steps_p):
    i = pl.program_id(0)

    @pl.when(i == 0)
    def _():
        _zero_unwritten(pe_ref, nu_ref, xs_ref, zbuf, zsem)

    @pl.when(i < steps_p)
    def _():
        _scatter_rows(dest_ref, hp_ref, xs_ref, sem, i * hp_ref.shape[0])

    @pl.when(i >= steps_p)
    def _():
        _scatter_rows(dest_ref, hs_ref, xs_ref, sem, steps_p * hp_ref.shape[0] + (i - steps_p) * hs_ref.shape[0])


def _dispatch(h2p, h2s, dest, pad_edges, n_used, rows):
    n_p, n_s = h2p.shape[0], h2s.shape[0]
    tp, ts = min(DISP_T, n_p), min(DISP_T, n_s)
    assert n_p % tp == 0 and n_s % ts == 0 and rows // MOE_ROWS >= N_EXPERTS
    steps_p, steps_s = n_p // tp, n_s // ts
    return pl.pallas_call(
        functools.partial(_dispatch_kernel, steps_p=steps_p),
        grid_spec=pltpu.PrefetchScalarGridSpec(
            num_scalar_prefetch=3,
            grid=(steps_p + steps_s,),
            in_specs=[
                pl.BlockSpec((tp, D_MODEL), lambda i, d, pe, nu: (jnp.minimum(i, steps_p - 1), 0)),
                pl.BlockSpec((ts, D_MODEL), lambda i, d, pe, nu: (jnp.clip(i - steps_p, 0, steps_s - 1), 0)),
            ],
            out_specs=pl.BlockSpec(memory_space=pl.ANY),
            scratch_shapes=[pltpu.SemaphoreType.DMA(()), pltpu.VMEM((MOE_ROWS, D_MODEL), F32),
                            pltpu.SemaphoreType.DMA(())],
        ),
        out_shape=jax.ShapeDtypeStruct((rows, D_MODEL), F32),
        compiler_params=_params(("arbitrary",)),
        name="dispatch",
    )(dest, pad_edges, n_used, h2p, h2s)


def _moe_kernel(be_ref, nu_ref, x_ref, wgu_ref, bgu_ref, wdn_ref, bdn_ref, y_ref, wgu_bf, wdn_bf):
    i = pl.program_id(0)

    @pl.when(i < nu_ref[0])
    def _():
        @pl.when((i == 0) | (be_ref[i] != be_ref[jnp.maximum(i - 1, 0)]))
        def _():
            wgu_bf[...] = wgu_ref[0].astype(BF16)
            wdn_bf[...] = wdn_ref[0].astype(BF16)

        gu = jnp.dot(x_ref[...].astype(BF16), wgu_bf[...], preferred_element_type=F32) + bgu_ref[0]
        g = jnp.minimum(gu[:, :D_FF], SWIGLU_LIMIT)
        u = jnp.clip(gu[:, D_FF:], -SWIGLU_LIMIT, SWIGLU_LIMIT)
        act = ((u + 1.0) * g * _sigmoid(SWIGLU_ALPHA * g)).astype(BF16)
        y_ref[...] = jnp.dot(act, wdn_bf[...], preferred_element_type=F32) + bdn_ref[0]

    @pl.when(i >= nu_ref[0])
    def _():
        y_ref[...] = jnp.zeros_like(y_ref)


def _moe_experts(xs, blk_e, n_used, w_gu, b_gu, w_down, b_down):
    n_blocks = blk_e.shape[0]
    return pl.pallas_call(
        _moe_kernel,
        grid_spec=pltpu.PrefetchScalarGridSpec(
            num_scalar_prefetch=2,
            grid=(n_blocks,),
            in_specs=[
                pl.BlockSpec((MOE_ROWS, D_MODEL), lambda i, be, nu: (i, 0)),
                pl.BlockSpec((1, D_MODEL, 2 * D_FF), lambda i, be, nu: (be[i], 0, 0)),
                pl.BlockSpec((1, 1, 2 * D_FF), lambda i, be, nu: (be[i], 0, 0)),
                pl.BlockSpec((1, D_FF, D_MODEL), lambda i, be, nu: (be[i], 0, 0)),
                pl.BlockSpec((1, 1, D_MODEL), lambda i, be, nu: (be[i], 0, 0)),
            ],
            out_specs=pl.BlockSpec((MOE_ROWS, D_MODEL), lambda i, be, nu: (i, 0)),
            scratch_shapes=[
                pltpu.VMEM((D_MODEL, 2 * D_FF), BF16),
                pltpu.VMEM((D_FF, D_MODEL), BF16),
            ],
        ),
        out_shape=jax.ShapeDtypeStruct((n_blocks * MOE_ROWS, D_MODEL), F32),
        compiler_params=_params(("arbitrary",)),
        name="moe_experts",
    )(blk_e, n_used, xs, w_gu, b_gu.reshape(N_EXPERTS, 1, -1), w_down, b_down.reshape(N_EXPERTS, 1, -1))


COMB_T = 256


def _combine_kernel(dest_ref, yb_ref, x1_ref, rt_ref, ga2_ref, gf_ref, y_ref, gbuf, sem, *, tok_off):
    i = pl.program_id(1)
    b = pl.program_id(0)
    n_i = pl.num_programs(1)
    step = b * n_i + i
    n_steps = pl.num_programs(0) * n_i
    tm = gbuf.shape[2]

    def start_gather(stp, slot):
        base = (tok_off + stp * tm) * TOP_K

        def body(r, carry):
            for k in range(TOP_K):
                row = dest_ref[base + r * TOP_K + k]
                pltpu.make_async_copy(yb_ref.at[pl.ds(row, 1), :], gbuf.at[slot, k, pl.ds(r, 1), :],
                                      sem.at[slot]).start()
            return carry

        lax.fori_loop(0, tm, body, 0)

    @pl.when(step == 0)
    def _():
        start_gather(0, 0)

    slot = step % 2

    @pl.when(step + 1 < n_steps)
    def _():
        start_gather(step + 1, 1 - slot)

    for k in range(TOP_K):
        pltpu.make_async_copy(yb_ref.at[pl.ds(0, tm), :], gbuf.at[slot, k], sem.at[slot]).wait()
    rec = rt_ref[0]
    moe = gbuf[slot, 0] * rec[:, ROUTE_W:ROUTE_W + 1]
    for k in range(1, TOP_K):
        moe = moe + gbuf[slot, k] * rec[:, ROUTE_W + k:ROUTE_W + k + 1]
    y_ref[0] = _rmsnorm(x1_ref[0] + ga2_ref[0] * moe, gf_ref[...])


def _combine(yb, dest, x1, route, ga2, g_final, tok_off):
    G, T, _ = x1.shape
    R = ga2.shape[1]
    tm = min(COMB_T, T)
    rm = 1 if R == 1 else tm
    mod_map = (lambda b, i, d: (b, 0, 0)) if R == 1 else (lambda b, i, d: (b, i, 0))
    return pl.pallas_call(
        functools.partial(_combine_kernel, tok_off=tok_off),
        grid_spec=pltpu.PrefetchScalarGridSpec(
            num_scalar_prefetch=1,
            grid=(G, T // tm),
            in_specs=[
                pl.BlockSpec(memory_space=pl.ANY),
                pl.BlockSpec((1, tm, D_MODEL), lambda b, i, d: (b, i, 0)),
                pl.BlockSpec((1, tm, LANES), lambda b, i, d: (b, i, 0)),
                pl.BlockSpec((1, rm, D_MODEL), mod_map),
                pl.BlockSpec((1, D_MODEL), lambda b, i, d: (0, 0)),
            ],
            out_specs=pl.BlockSpec((1, tm, D_MODEL), lambda b, i, d: (b, i, 0)),
            scratch_shapes=[
                pltpu.VMEM((2, TOP_K, tm, D_MODEL), F32),
                pltpu.SemaphoreType.DMA((2,)),
            ],
        ),
        out_shape=jax.ShapeDtypeStruct((G, T, D_MODEL), F32),
        compiler_params=_params(("arbitrary", "arbitrary")),
        name="combine",
    )(dest, yb, x1, route, ga2, g_final.reshape(1, -1))


def _routing(route, counts):
    n = route.shape[0]
    na = n * TOP_K
    topi = route[:, ROUTE_E:ROUTE_E + TOP_K].astype(jnp.int32)
    rank = route[:, ROUTE_R:ROUTE_R + TOP_K].astype(jnp.int32)
    cnt = counts.astype(jnp.int32)
    padded = (cnt + MOE_ROWS - 1) // MOE_ROWS * MOE_ROWS
    pad_end = jnp.cumsum(padded)
    pad_start = pad_end - padded
    onehot = topi[..., None] == jnp.arange(N_EXPERTS, dtype=jnp.int32)
    dest = (jnp.sum(jnp.where(onehot, pad_start, 0), axis=-1) + rank).reshape(na)
    n_blocks = -(-na // MOE_ROWS) + N_EXPERTS
    blk_start = jnp.arange(n_blocks, dtype=jnp.int32) * MOE_ROWS
    blk_e = jnp.minimum(jnp.sum(blk_start[:, None] >= pad_end[None, :], axis=1), N_EXPERTS - 1).astype(jnp.int32)
    n_used = (pad_end[-1] // MOE_ROWS).astype(jnp.int32).reshape(1)
    pad_edges = jnp.concatenate([jnp.zeros((1,), jnp.int32), pad_end.astype(jnp.int32)])
    return dest, blk_e, n_used, pad_edges


def _mods(mod):
    return [m[:, None, :] for m in jnp.split(mod, 6, axis=-1)]


def _layer(xp, xs, cp, cs, cache_k, cache_v, state, page_table, w_ada, b_ada, g_norm_mix, w_in, w_gk2, b_gk2,
           g_gla_norm, w_br_a, w_br_b, w_out, g_norm_ffn, w_router, b_router, w_gu, b_gu, w_down, b_down,
           g_final):
    B, T, _ = xp.shape
    S = xs.shape[0]
    c_all = jnp.concatenate([cp, cs], axis=0)
    pad = -c_all.shape[0] % 8
    mod = _ada_mod(jnp.pad(c_all, ((0, pad), (0, 0))), w_ada, b_ada)
    sh1p, sc1p, ga1p, sh2p, sc2p, ga2p = _mods(mod[:B])
    sh1s, sc1s, ga1s, sh2s, sc2s, ga2s = [m[None] for m in jnp.split(mod[B:B + S], 6, axis=-1)]

    w_perm = jnp.concatenate(
        [w_in[:, :COL_ZA], w_in[:, COL_ZA + G_LOWRANK:], w_in[:, COL_ZA:COL_ZA + G_LOWRANK],
         jnp.zeros((D_MODEL, PROJ_W - COL_LR - G_LOWRANK), F32)], axis=1).astype(BF16)
    wa, wb, wo = w_br_a.astype(BF16), w_br_b.astype(BF16), w_out.astype(BF16)

    xs3 = xs.reshape(1, S, D_MODEL)
    proj_p = _inproj(xp, sc1p, sh1p, g_norm_mix, w_perm, tm=1024)
    proj_s = _inproj(xs3, sc1s, sh1s, g_norm_mix, w_perm, tm=S)

    a_p = _moba_prompt(proj_p)
    og_p, s_p = _gla_prompt(proj_p, w_gk2, b_gk2, g_gla_norm)
    ps = proj_s[0]
    k_s = ps[:, COL_AK:COL_AK + A_KVW]
    v_s = ps[:, COL_AV:COL_AV + A_KVW]
    a_s = _moba_sample(ps[:, :A_QW], k_s, v_s, cache_k, cache_v, page_table)
    og_s, s_s = _gla_sample(proj_s.reshape(S, 1, PROJ_W), state, w_gk2, b_gk2, g_gla_norm)

    cnt0 = jnp.zeros((1, LANES), F32)
    x1p, h2p, rtp, cnt1 = _merge_router(xp, a_p, og_p, proj_p, ga1p, sc2p, sh2p, g_norm_ffn, wa, wb, wo,
                                        w_router, b_router, cnt0, tm=512)
    x1s, h2s, rts, cnt2 = _merge_router(xs3, a_s.astype(BF16)[None], og_s.reshape(1, S, -1).astype(BF16), proj_s,
                                        ga1s, sc2s, sh2s, g_norm_ffn, wa, wb, wo, w_router, b_router, cnt1, tm=S)

    n_p = B * T
    route = jnp.concatenate([rtp.reshape(n_p, LANES), rts.reshape(S, LANES)], axis=0)
    dest, blk_e, n_used, pad_edges = _routing(route, cnt2[0, :N_EXPERTS])
    xsort = _dispatch(h2p.reshape(n_p, D_MODEL), h2s.reshape(S, D_MODEL), dest, pad_edges, n_used,
                      rows=blk_e.shape[0] * MOE_ROWS)
    yb = _moe_experts(xsort, blk_e, n_used, w_gu, b_gu, w_down, b_down)
    y_p = _combine(yb, dest, x1p, rtp, ga2p, g_final, tok_off=0)
    y_s = _combine(yb, dest, x1s, rts, ga2s, g_final, tok_off=n_p)

    kp = proj_p[:, :, COL_AK:COL_AK + A_KVW].reshape(B, T, A_KV_HEADS, A_HEAD_DIM)
    vp = proj_p[:, :, COL_AV:COL_AV + A_KVW].reshape(B, T, A_KV_HEADS, A_HEAD_DIM)
    ks = k_s.reshape(S, 1, A_KV_HEADS, A_HEAD_DIM)
    vs = v_s.reshape(S, 1, A_KV_HEADS, A_HEAD_DIM)
    return y_p, y_s.reshape(S, 1, D_MODEL), kp, vp, s_p, ks, vs, s_s


def kernel(x_prompt, x_sample, c_prompt, c_sample, cache_k, cache_v, state_gla, page_table, w_ada, b_ada, g_norm_mix, w_in, w_gk2, b_gk2, g_gla_norm, w_br_a, w_br_b, w_out, g_norm_ffn, w_router, b_router, w_gu, b_gu, w_down, b_down, g_final):
    assert w_ada.shape[0] == 1, "single-layer step"
    outs = _layer(x_prompt, x_sample.reshape(x_sample.shape[0], D_MODEL), c_prompt, c_sample, cache_k[0],
                  cache_v[0], state_gla[0], page_table, w_ada[0], b_ada[0], g_norm_mix[0], w_in[0], w_gk2[0],
                  b_gk2[0], g_gla_norm[0], w_br_a[0], w_br_b[0], w_out[0], g_norm_ffn[0], w_router[0],
                  b_router[0], w_gu[0], b_gu[0], w_down[0], b_down[0], g_final)
    y_p, y_s, kp, vp, s_p, ks, vs, s_s = outs
    return (y_p, y_s, kp[None], vp[None], s_p[None], ks[None], vs[None], s_s[None])
```
